```python
import jax, jax.numpy as jnp
from jax import lax
import numpy as np

D_MODEL = 1024
BATCH = 8
SEQ = 2048
DEPTH = 1
DEC_BATCH = 32
DEC_SEQ = 1
PAST_LEN = 16384
PAGE_SIZE = 128

HEAD_DIM = 64
D_MIX = D_MODEL
C_CONV = D_MIX // 2
D_ATT = D_MIX - C_CONV
N_HEADS = D_ATT // HEAD_DIM
N_KV_HEADS = 2
GRP = N_HEADS // N_KV_HEADS
KV_W = N_KV_HEADS * HEAD_DIM
CONV_W = 31
CMP_LEN = 32
CMP_STRIDE = 16
N_SUB = CMP_LEN // CMP_STRIDE
CMP_HID = 2 * HEAD_DIM
SEL_BLK = 64
TOPK = 16
WINDOW = 512
ROT_DIM = HEAD_DIM // 4
ROPE_THETA = 500000.0
D_FF = 2816
N_IN = 2 * C_CONV + D_ATT + 6 * KV_W + 3 * N_HEADS
QBLK = 64
NEG_INF = -1e30
FORCE_BONUS = 1e4
EPS = 1e-6

kernel_name = 'hybrid_conformer_conv_nsa_macaron_step'


def rms_norm(x, g):
    xf = x.astype(jnp.float32)
    y = xf * lax.rsqrt(jnp.mean(xf * xf, axis=-1, keepdims=True) + EPS)
    return (y * g.astype(jnp.float32)).astype(x.dtype)


def layer_norm(x, g, b):
    xf = x.astype(jnp.float32)
    mu = jnp.mean(xf, axis=-1, keepdims=True)
    var = jnp.mean(jnp.square(xf - mu), axis=-1, keepdims=True)
    y = (xf - mu) * lax.rsqrt(var + EPS) * g.astype(jnp.float32) + b.astype(jnp.float32)
    return y.astype(x.dtype)


def swiglu(h, w_in, w_out):
    a, b = jnp.split(h @ w_in, 2, axis=-1)
    return (jax.nn.silu(a) * b) @ w_out


def masked_softmax(s, mask):
    s = jnp.where(mask, s, NEG_INF)
    m = jnp.max(s, axis=-1, keepdims=True)
    e = jnp.where(mask, jnp.exp(s - m), 0.0)
    return e / jnp.maximum(jnp.sum(e, axis=-1, keepdims=True), 1e-30)


def rope(x, pos):
    half = ROT_DIM // 2
    inv = ROPE_THETA ** (-(jnp.arange(half, dtype=jnp.float32) * 2.0 / ROT_DIM))
    ang = pos.astype(jnp.float32)[:, None] * inv[None, :]
    cos = jnp.cos(ang)[:, None, :]
    sin = jnp.sin(ang)[:, None, :]
    xr = x[..., :ROT_DIM].astype(jnp.float32)
    x1, x2 = xr[..., :half], xr[..., half:]
    rot = jnp.concatenate([x1 * cos - x2 * sin, x1 * sin + x2 * cos], axis=-1).astype(x.dtype)
    return jnp.concatenate([rot, x[..., ROT_DIM:]], axis=-1)


def compress(k_raw, pe, w1, w2):
    B, L, G, D = k_raw.shape
    n_chunk = L // CMP_STRIDE
    n_cmp = n_chunk - N_SUB + 1
    ch = k_raw[:, :n_chunk * CMP_STRIDE].reshape(B, n_chunk, CMP_STRIDE, G, D)
    pe = pe.reshape(N_SUB, CMP_STRIDE, D)
    w1 = w1.reshape(N_SUB, CMP_STRIDE, D, CMP_HID)
    h = jnp.einsum('bcsgd,sdh->bcgh', ch + pe[0][:, None, :], w1[0])[:, :n_cmp]
    for m in range(1, N_SUB):
        h = h + jnp.einsum('bcsgd,sdh->bcgh', ch + pe[m][:, None, :], w1[m])[:, m:m + n_cmp]
    return jnp.einsum('bngh,hd->bngd', jax.nn.silu(h), w2)


def cmp_to_sel(n_cmp, n_sel):
    cs = jnp.arange(n_cmp)[:, None] * CMP_STRIDE
    ss = jnp.arange(n_sel)[None, :] * SEL_BLK
    ov = jnp.clip(jnp.minimum(cs + CMP_LEN, ss + SEL_BLK) - jnp.maximum(cs, ss), 0, None)
    return ov.astype(jnp.float32) / CMP_LEN


def nsa_attention(q, gates, q_pos0, k_cmp, v_cmp, cmp_end, k_sel, v_sel, k_win, v_win, win_pos0):
    B, Tq = q.shape[0], q.shape[1]
    f32 = jnp.float32
    scale = HEAD_DIM ** -0.5
    qb = QBLK if Tq % QBLK == 0 else Tq
    n_qb = Tq // qb
    Ls = k_sel.shape[1]
    n_sel = -(-Ls // SEL_BLK)
    pad = n_sel * SEL_BLK - Ls

    def to_blocks(t):
        t = jnp.pad(t, ((0, 0), (0, pad), (0, 0), (0, 0)))
        return t.reshape(B, n_sel, SEL_BLK, N_KV_HEADS, HEAD_DIM).transpose(0, 3, 1, 2, 4)

    kb, vb = to_blocks(k_sel), to_blocks(v_sel)
    k_eff = min(TOPK, n_sel)
    ov = cmp_to_sel(k_cmp.shape[1], n_sel)
    kw_pad = jnp.pad(k_win, ((0, 0), (WINDOW, 0), (0, 0), (0, 0)))
    vw_pad = jnp.pad(v_win, ((0, 0), (WINDOW, 0), (0, 0), (0, 0)))
    kc32, vc32 = k_cmp.astype(f32), v_cmp.astype(f32)
    bi = jnp.arange(B)[:, None, None, None]
    gi = jnp.arange(N_KV_HEADS)[None, :, None, None]
    jsel = jnp.arange(n_sel)

    def block(i):
        off = i * qb
        qpos = q_pos0 + off + jnp.arange(qb)
        qg = lax.dynamic_slice_in_dim(q, off, qb, 1).reshape(B, qb, N_KV_HEADS, GRP, HEAD_DIM).astype(f32)
        g = lax.dynamic_slice_in_dim(gates, off, qb, 1).reshape(B, qb, N_KV_HEADS, GRP, 3).astype(f32)
        s_c = jnp.einsum('bqgrd,bngd->bqgrn', qg, kc32) * scale
        mask_c = (cmp_end[None, :] <= qpos[:, None])[None, :, None, None, :]
        p_c = masked_softmax(s_c, mask_c)
        o_c = jnp.einsum('bqgrn,bngd->bqgrd', p_c, vc32)
        imp = jnp.einsum('bqgn,ns->bqgs', jnp.sum(p_c, axis=3), ov)
        cur = qpos // SEL_BLK
        valid = jsel[None, :] <= cur[:, None]
        forced = (jsel[None, :] == 0) | (jsel[None, :] == cur[:, None]) | (jsel[None, :] == cur[:, None] - 1)
        score = jnp.where(valid[None, :, None, :],
                          imp + jnp.where(forced, FORCE_BONUS, 0.0)[None, :, None, :], NEG_INF)
        _, idx = lax.top_k(score, k_eff)
        idx_t = idx.transpose(0, 2, 1, 3)
        ks_g = kb[bi, gi, idx_t].reshape(B, N_KV_HEADS, qb, k_eff * SEL_BLK, HEAD_DIM)
        vs_g = vb[bi, gi, idx_t].reshape(B, N_KV_HEADS, qb, k_eff * SEL_BLK, HEAD_DIM)
        kpos = (idx_t[..., None] * SEL_BLK + jnp.arange(SEL_BLK)).reshape(B, N_KV_HEADS, qb, k_eff * SEL_BLK)
        mask_s = (kpos <= qpos[None, None, :, None]).transpose(0, 2, 1, 3)[:, :, :, None, :]
        s_s = jnp.einsum('bqgrd,bgqnd->bqgrn', qg, ks_g.astype(f32)) * scale
        p_s = masked_softmax(s_s, mask_s)
        o_s = jnp.einsum('bqgrn,bgqnd->bqgrd', p_s, vs_g.astype(f32))
        start = q_pos0 + off - win_pos0
        kw = lax.dynamic_slice_in_dim(kw_pad, start, WINDOW + qb, 1).astype(f32)
        vw = lax.dynamic_slice_in_dim(vw_pad, start, WINDOW + qb, 1).astype(f32)
        kpos_w = q_pos0 + off - WINDOW + jnp.arange(WINDOW + qb)
        dist = qpos[:, None] - kpos_w[None, :]
        mask_w = ((kpos_w[None, :] >= win_pos0) & (dist >= 0) & (dist < WINDOW))[None, :, None, None, :]
        s_w = jnp.einsum('bqgrd,bkgd->bqgrk', qg, kw) * scale
        p_w = masked_softmax(s_w, mask_w)
        o_w = jnp.einsum('bqgrk,bkgd->bqgrd', p_w, vw)
        o = g[..., 0:1] * o_c + g[..., 1:2] * o_s + g[..., 2:3] * o_w
        return o.reshape(B, qb, D_ATT).astype(q.dtype)

    o = lax.map(block, jnp.arange(n_qb))
    return jnp.moveaxis(o, 0, 1).reshape(B, Tq, D_ATT)


def token_mixers(h, pos0, conv_prev, pk_cmp, pv_cmp, pk_sel, pv_sel, win_k_prev, win_v_prev, p):
    B, T, _ = h.shape
    widths = [2 * C_CONV, D_ATT, KV_W, KV_W, KV_W, KV_W, KV_W, KV_W, 3 * N_HEADS]
    bounds = np.cumsum(widths)[:-1].tolist()
    glu_in, q, k_c, v_c, k_s, v_s, k_w, v_w, g_logit = jnp.split(h @ p['w_in'], bounds, axis=-1)

    a, b = jnp.split(glu_in, 2, axis=-1)
    u = a * jax.nn.sigmoid(b)
    ubuf = jnp.concatenate([conv_prev, u], axis=1)
    c = lax.conv_general_dilated(ubuf, p['conv_w'][:, None, :], window_strides=(1,), padding='VALID',
                                 dimension_numbers=('NWC', 'WIO', 'NWC'),
                                 feature_group_count=C_CONV) + p['conv_b']
    c = jax.nn.silu(layer_norm(c, p['conv_ln_g'], p['conv_ln_b']))
    new_conv = ubuf[:, ubuf.shape[1] - (CONV_W - 1):]

    pos = pos0 + jnp.arange(T)
    q = rope(rms_norm(q.reshape(B, T, N_HEADS, HEAD_DIM), p['q_norm']), pos)
    k_c = k_c.reshape(B, T, N_KV_HEADS, HEAD_DIM)
    v_c = v_c.reshape(B, T, N_KV_HEADS, HEAD_DIM)
    k_s = rope(rms_norm(k_s.reshape(B, T, N_KV_HEADS, HEAD_DIM), p['k_sel_norm']), pos)
    v_s = v_s.reshape(B, T, N_KV_HEADS, HEAD_DIM)
    k_w = rope(rms_norm(k_w.reshape(B, T, N_KV_HEADS, HEAD_DIM), p['k_win_norm']), pos)
    v_w = v_w.reshape(B, T, N_KV_HEADS, HEAD_DIM)
    kc_all = jnp.concatenate([pk_cmp, k_c], axis=1)
    vc_all = jnp.concatenate([pv_cmp, v_c], axis=1)
    ks_all = jnp.concatenate([pk_sel, k_s], axis=1)
    vs_all = jnp.concatenate([pv_sel, v_s], axis=1)
    k_cmp_tok = compress(kc_all, p['cmp_k_pos'], p['cmp_k_w1'], p['cmp_k_w2'])
    v_cmp_tok = compress(vc_all, p['cmp_v_pos'], p['cmp_v_w1'], p['cmp_v_w2'])
    cmp_end = jnp.arange(k_cmp_tok.shape[1]) * CMP_STRIDE + CMP_LEN - 1
    k_cmp_tok = rope(rms_norm(k_cmp_tok, p['k_cmp_norm']), cmp_end)
    kw_all = jnp.concatenate([win_k_prev, k_w], axis=1)
    vw_all = jnp.concatenate([win_v_prev, v_w], axis=1)
    win_pos0 = pos0 - win_k_prev.shape[1]
    gates = jax.nn.sigmoid(g_logit.reshape(B, T, N_HEADS, 3))
    o = nsa_attention(q, gates, pos0, k_cmp_tok, v_cmp_tok, cmp_end, ks_all, vs_all, kw_all, vw_all, win_pos0)
    keep = min(WINDOW, kw_all.shape[1])
    new_kw = kw_all[:, kw_all.shape[1] - keep:]
    new_vw = vw_all[:, vw_all.shape[1] - keep:]

    mixed = jnp.concatenate([rms_norm(c, p['out_norm_conv']), rms_norm(o, p['out_norm_attn'])], axis=-1) @ p['w_out']
    return mixed, (k_c, v_c, k_s, v_s, new_kw, new_vw, new_conv)


def layer(x, pos0, conv_prev, pk_cmp, pv_cmp, pk_sel, pv_sel, win_k_prev, win_v_prev, p):
    x = x + 0.5 * swiglu(rms_norm(x, p['ffn1_norm']), p['ffn1_w_in'], p['ffn1_w_out'])
    mixed, new_state = token_mixers(rms_norm(x, p['mix_norm']), pos0, conv_prev, pk_cmp, pv_cmp,
                                    pk_sel, pv_sel, win_k_prev, win_v_prev, p)
    x = x + mixed
    x = x + 0.5 * swiglu(rms_norm(x, p['ffn2_norm']), p['ffn2_w_in'], p['ffn2_w_out'])
    return rms_norm(x, p['final_norm']), new_state


def setup_inputs(seed: int = 0) -> dict:
    key = jax.random.key(seed)
    keys = iter(jax.random.split(key, 48))

    def nrm(shape, scale):
        return jax.random.normal(next(keys), shape, jnp.float32) * scale

    def gain(shape):
        return 1.0 + nrm(shape, 0.02)

    n_pages = PAST_LEN // PAGE_SIZE
    n_used = DEC_BATCH * n_pages
    n_pool = n_used + (n_used + 3) // 4
    win_buf = min(WINDOW, PAST_LEN)
    pool_shape = (DEPTH, n_pool, PAGE_SIZE, N_KV_HEADS, HEAD_DIM)
    win_shape = (DEPTH, DEC_BATCH, win_buf, N_KV_HEADS, HEAD_DIM)
    x_prompt = nrm((BATCH, SEQ, D_MODEL), 1.0)
    x_sample = nrm((DEC_BATCH, DEC_SEQ, D_MODEL), 1.0)
    cache_k_cmp = nrm(pool_shape, 1.0)
    cache_v_cmp = nrm(pool_shape, 1.0)
    cache_k_sel = nrm(pool_shape, 1.0)
    cache_v_sel = nrm(pool_shape, 1.0)
    state_k_win = nrm(win_shape, 1.0)
    state_v_win = nrm(win_shape, 1.0)
    state_conv = nrm((DEPTH, DEC_BATCH, CONV_W - 1, C_CONV), 0.5)
    page_table = jax.random.permutation(next(keys), n_pool)[:n_used].reshape(DEC_BATCH, n_pages).astype(jnp.int32)
    return {
        'x_prompt': x_prompt,
        'x_sample': x_sample,
        'cache_k_cmp': cache_k_cmp,
        'cache_v_cmp': cache_v_cmp,
        'cache_k_sel': cache_k_sel,
        'cache_v_sel': cache_v_sel,
        'state_k_win': state_k_win,
        'state_v_win': state_v_win,
        'state_conv': state_conv,
        'page_table': page_table,
        'ffn1_norm': gain((DEPTH, D_MODEL)),
        'ffn1_w_in': nrm((DEPTH, D_MODEL, 2 * D_FF), D_MODEL ** -0.5),
        'ffn1_w_out': nrm((DEPTH, D_FF, D_MODEL), D_FF ** -0.5),
        'mix_norm': gain((DEPTH, D_MODEL)),
        'w_in': nrm((DEPTH, D_MODEL, N_IN), D_MODEL ** -0.5),
        'conv_w': nrm((DEPTH, CONV_W, C_CONV), CONV_W ** -0.5),
        'conv_b': nrm((DEPTH, C_CONV), 0.01),
        'conv_ln_g': gain((DEPTH, C_CONV)),
        'conv_ln_b': nrm((DEPTH, C_CONV), 0.01),
        'q_norm': gain((DEPTH, HEAD_DIM)),
        'k_cmp_norm': gain((DEPTH, HEAD_DIM)),
        'k_sel_norm': gain((DEPTH, HEAD_DIM)),
        'k_win_norm': gain((DEPTH, HEAD_DIM)),
        'cmp_k_pos': nrm((DEPTH, CMP_LEN, HEAD_DIM), 0.1),
        'cmp_k_w1': nrm((DEPTH, CMP_LEN, HEAD_DIM, CMP_HID), (CMP_LEN * HEAD_DIM) ** -0.5),
        'cmp_k_w2': nrm((DEPTH, CMP_HID, HEAD_DIM), CMP_HID ** -0.5),
        'cmp_v_pos': nrm((DEPTH, CMP_LEN, HEAD_DIM), 0.1),
        'cmp_v_w1': nrm((DEPTH, CMP_LEN, HEAD_DIM, CMP_HID), (CMP_LEN * HEAD_DIM) ** -0.5),
        'cmp_v_w2': nrm((DEPTH, CMP_HID, HEAD_DIM), CMP_HID ** -0.5),
        'out_norm_conv': gain((DEPTH, C_CONV)),
        'out_norm_attn': gain((DEPTH, D_ATT)),
        'w_out': nrm((DEPTH, D_MIX, D_MODEL), D_MIX ** -0.5),
        'ffn2_norm': gain((DEPTH, D_MODEL)),
        'ffn2_w_in': nrm((DEPTH, D_MODEL, 2 * D_FF), D_MODEL ** -0.5),
        'ffn2_w_out': nrm((DEPTH, D_FF, D_MODEL), D_FF ** -0.5),
        'final_norm': gain((DEPTH, D_MODEL)),
    }


def reference(x_prompt, x_sample, cache_k_cmp, cache_v_cmp, cache_k_sel, cache_v_sel, state_k_win, state_v_win,
              state_conv, page_table, ffn1_norm, ffn1_w_in, ffn1_w_out, mix_norm, w_in, conv_w, conv_b, conv_ln_g,
              conv_ln_b, q_norm, k_cmp_norm, k_sel_norm, k_win_norm, cmp_k_pos, cmp_k_w1, cmp_k_w2, cmp_v_pos,
              cmp_v_w1, cmp_v_w2, out_norm_conv, out_norm_attn, w_out, ffn2_norm, ffn2_w_in, ffn2_w_out, final_norm):
    b_p = x_prompt.shape[0]
    b_s = x_sample.shape[0]
    past_len = page_table.shape[1] * cache_k_cmp.shape[2]
    yp, ys = x_prompt, x_sample
    new_p, new_s = [], []
    for l in range(DEPTH):
        p = {
            'ffn1_norm': ffn1_norm[l], 'ffn1_w_in': ffn1_w_in[l], 'ffn1_w_out': ffn1_w_out[l],
            'mix_norm': mix_norm[l], 'w_in': w_in[l], 'conv_w': conv_w[l], 'conv_b': conv_b[l],
            'conv_ln_g': conv_ln_g[l], 'conv_ln_b': conv_ln_b[l], 'q_norm': q_norm[l],
            'k_cmp_norm': k_cmp_norm[l], 'k_sel_norm': k_sel_norm[l], 'k_win_norm': k_win_norm[l],
            'cmp_k_pos': cmp_k_pos[l], 'cmp_k_w1': cmp_k_w1[l], 'cmp_k_w2': cmp_k_w2[l],
            'cmp_v_pos': cmp_v_pos[l], 'cmp_v_w1': cmp_v_w1[l], 'cmp_v_w2': cmp_v_w2[l],
            'out_norm_conv': out_norm_conv[l], 'out_norm_attn': out_norm_attn[l], 'w_out': w_out[l],
            'ffn2_norm': ffn2_norm[l], 'ffn2_w_in': ffn2_w_in[l], 'ffn2_w_out': ffn2_w_out[l],
            'final_norm': final_norm[l],
        }
        empty = jnp.zeros((b_p, 0, N_KV_HEADS, HEAD_DIM), x_prompt.dtype)
        conv0 = jnp.zeros((b_p, CONV_W - 1, C_CONV), x_prompt.dtype)
        yp, st_p = layer(yp, 0, conv0, empty, empty, empty, empty, empty, empty, p)
        new_p.append(st_p)
        past = [c[l][page_table].reshape(b_s, past_len, N_KV_HEADS, HEAD_DIM)
                for c in (cache_k_cmp, cache_v_cmp, cache_k_sel, cache_v_sel)]
        ys, st_s = layer(ys, past_len, state_conv[l], past[0], past[1], past[2], past[3],
                         state_k_win[l], state_v_win[l], p)
        new_s.append(st_s)
    p_k_cmp, p_v_cmp, p_k_sel, p_v_sel, p_k_win, p_v_win, p_conv = [jnp.stack(t) for t in zip(*new_p)]
    s_k_cmp, s_v_cmp, s_k_sel, s_v_sel, s_k_win, s_v_win, s_conv = [jnp.stack(t) for t in zip(*new_s)]
    return (yp, ys, p_k_cmp, p_v_cmp, p_k_sel, p_v_sel, p_k_win, p_v_win, p_conv,
            s_k_cmp, s_v_cmp, s_k_sel, s_v_sel, s_k_win, s_v_win, s_conv)
```

```python
import functools

import numpy as np
import jax
import jax.numpy as jnp
from jax import lax
from jax.experimental import pallas as pl
from jax.experimental.pallas import tpu as pltpu

F32 = jnp.float32
BF16 = jnp.bfloat16

HEAD_DIM = 64
N_HEADS = 8
N_KV_HEADS = 2
GRP = N_HEADS // N_KV_HEADS
KV_W = N_KV_HEADS * HEAD_DIM
C_CONV = 512
D_ATT = 512
CONV_W = 31
CMP_LEN = 32
CMP_STRIDE = 16
N_SUB = CMP_LEN // CMP_STRIDE
CMP_HID = 2 * HEAD_DIM
SEL_BLK = 64
TOPK = 16
WINDOW = 512
ROT_DIM = HEAD_DIM // 4
ROT_HALF = ROT_DIM // 2
ROPE_THETA = 500000.0
NEG_INF = -1e30
FORCE_BONUS = 1e4
EPS = 1e-6
SCALE = HEAD_DIM ** -0.5
LANES = 128
CHUNK_W = CMP_STRIDE * KV_W
VMEM_LIMIT = 56 * 1024 * 1024


def _params(n_grid_dims):
    return pltpu.CompilerParams(
        dimension_semantics=("arbitrary",) * n_grid_dims, vmem_limit_bytes=VMEM_LIMIT)


def _dot(a, b):
    return jnp.dot(a, b, preferred_element_type=F32)


def _dot_t(a, b):
    return lax.dot_general(a, b, (((1,), (1,)), ((), ())), preferred_element_type=F32)


def _split_dot(x, m_bf16):
    hi = x.astype(BF16)
    lo = (x - hi.astype(F32)).astype(BF16)
    return _dot(hi, m_bf16) + _dot(lo, m_bf16)


def _sigmoid(x):
    return 1.0 / (1.0 + jnp.exp(-x))


def _rms(x, g):
    return x * lax.rsqrt(jnp.mean(x * x, axis=-1, keepdims=True) + EPS) * g


def _group_rms(x, g_tiled, ones_blockdiag):
    ss = _split_dot(x * x, ones_blockdiag)
    return x * lax.rsqrt(ss * (1.0 / HEAD_DIM) + EPS) * g_tiled


def _rope(x, c, sa, sb):
    n = x.shape[-1]
    return x * c + pltpu.roll(x, n - ROT_HALF, 1) * sa + pltpu.roll(x, ROT_HALF, 1) * sb


def _tile_lanes(t, reps):
    return jnp.concatenate([t] * reps, axis=-1) if reps > 1 else t


def _masked_softmax(s, mask):
    s = jnp.where(mask, s, NEG_INF)
    m = jnp.max(s, axis=-1, keepdims=True)
    e = jnp.where(mask, jnp.exp(s - m), 0.0)
    return e / jnp.maximum(jnp.sum(e, axis=-1, keepdims=True), 1e-30)


def _head_to_group_lanes(q, h, lane):
    g, p = h // GRP, h // 2
    qp = q[:, LANES * p:LANES * (p + 1)]
    src = qp if (h % 2) == g else pltpu.roll(qp, HEAD_DIM, 1)
    keep = (lane < HEAD_DIM) if g == 0 else (lane >= HEAD_DIM)
    return jnp.where(keep, src, 0.0)


def _pair_tile(o_even, o_odd, h_even, lane):
    g = h_even // GRP
    a = o_even if g == 0 else pltpu.roll(o_even, HEAD_DIM, 1)
    b = pltpu.roll(o_odd, HEAD_DIM, 1) if g == 0 else o_odd
    return jnp.where(lane < HEAD_DIM, a, b)


def _ffn_body(*refs, n_chunk, tf, ff, final):
    if final:
        x_ref, g_ref, wi_ref, wo_ref, fg_ref, o_ref = refs
    else:
        x_ref, g_ref, wi_ref, wo_ref, o_ref = refs
    x = x_ref[...]
    xn = _rms(x, g_ref[...]).astype(BF16)
    acc = jnp.zeros(x.shape, F32)
    for c in range(n_chunk):
        a = _dot(xn, wi_ref[:, c * tf:(c + 1) * tf])
        b = _dot(xn, wi_ref[:, ff + c * tf:ff + (c + 1) * tf])
        h = (a * _sigmoid(a)) * b
        acc = acc + _dot(h.astype(BF16), wo_ref[c * tf:(c + 1) * tf, :])
    y = x + 0.5 * acc
    if final:
        y = _rms(y, fg_ref[...])
    o_ref[...] = y


def _ffn(x2, norm_g, w_in, w_out, final_g, tm):
    rows, d = x2.shape
    ff = w_out.shape[0]
    tf = 256
    assert rows % tm == 0 and ff % tf == 0 and w_in.shape == (d, 2 * ff)
    const = lambda i: (0, 0)
    single = pl.Buffered(1)
    in_specs = [
        pl.BlockSpec((tm, d), lambda i: (i, 0)),
        pl.BlockSpec((1, d), const),
        pl.BlockSpec((d, 2 * ff), const, pipeline_mode=single),
        pl.BlockSpec((ff, d), const, pipeline_mode=single),
    ]
    args = [x2, norm_g.reshape(1, d), w_in.astype(BF16), w_out.astype(BF16)]
    if final_g is not None:
        in_specs.append(pl.BlockSpec((1, d), const))
        args.append(final_g.reshape(1, d))
    body = functools.partial(_ffn_body, n_chunk=ff // tf, tf=tf, ff=ff, final=final_g is not None)
    return pl.pallas_call(
        body, out_shape=jax.ShapeDtypeStruct((rows, d), F32), grid=(rows // tm,),
        in_specs=in_specs, out_specs=pl.BlockSpec((tm, d), lambda i: (i, 0)),
        compiler_params=_params(1), name="ffn")(*args)


_SEG_GLU = 2 * C_CONV
_SEG_Q = _SEG_GLU + D_ATT
N_IN = _SEG_Q + 6 * KV_W + 3 * N_HEADS
N_IN_PAD = _SEG_Q + 7 * KV_W


def _proj_body(x_ref, g_ref, w_ref, qg_ref, ksg_ref, kwg_ref, bd4_ref, bd1_ref, c_ref, sa_ref, sb_ref,
               u_ref, q_ref, kc_ref, vc_ref, ks_ref, vs_ref, kw_ref, vw_ref, gt_ref):
    xn = _rms(x_ref[...], g_ref[...]).astype(BF16)

    def seg(lo, width):
        return _dot(xn, w_ref[:, lo:lo + width])

    c, sa, sb = c_ref[...], sa_ref[...], sb_ref[...]
    reps = D_ATT // LANES
    glu = seg(0, _SEG_GLU)
    u_ref[...] = glu[:, :C_CONV] * _sigmoid(glu[:, C_CONV:])
    q = _group_rms(seg(_SEG_GLU, D_ATT), qg_ref[...], bd4_ref[...])
    q_ref[...] = _rope(q, _tile_lanes(c, reps), _tile_lanes(sa, reps), _tile_lanes(sb, reps))
    base = _SEG_Q
    kc_ref[...] = seg(base, KV_W)
    vc_ref[...] = seg(base + KV_W, KV_W)
    ks_ref[...] = _rope(_group_rms(seg(base + 2 * KV_W, KV_W), ksg_ref[...], bd1_ref[...]), c, sa, sb)
    vs_ref[...] = seg(base + 3 * KV_W, KV_W)
    kw_ref[...] = _rope(_group_rms(seg(base + 4 * KV_W, KV_W), kwg_ref[...], bd1_ref[...]), c, sa, sb)
    vw_ref[...] = seg(base + 5 * KV_W, KV_W)
    gt_ref[...] = _sigmoid(seg(base + 6 * KV_W, KV_W))


def _blockdiag_ones(width):
    idx = np.arange(width) // HEAD_DIM
    return jnp.asarray(idx[:, None] == idx[None, :], dtype=BF16)


def _rope_tables(pos):
    inv = ROPE_THETA ** (-(jnp.arange(ROT_HALF, dtype=F32) * 2.0 / ROT_DIM))
    ang = pos.astype(F32)[:, None] * inv[None, :]
    cos, sin = jnp.cos(ang), jnp.sin(ang)
    n = pos.shape[0]
    ones = jnp.ones((n, HEAD_DIM - ROT_DIM), F32)
    zeros = jnp.zeros((n, HEAD_DIM - ROT_DIM), F32)
    z8 = jnp.zeros((n, ROT_HALF), F32)
    c = jnp.concatenate([cos, cos, ones], axis=1)
    sa = jnp.concatenate([-sin, z8, zeros], axis=1)
    sb = jnp.concatenate([z8, sin, zeros], axis=1)
    return tuple(jnp.tile(t, (1, N_KV_HEADS)) for t in (c, sa, sb))


def _proj(x2, p, tabs, tm):
    rows, d = x2.shape
    n_tab = tabs[0].shape[0] // tm
    w = jnp.pad(p['w_in'], ((0, 0), (0, N_IN_PAD - N_IN))).astype(BF16)
    const = lambda i: (0, 0)
    row = lambda i: (i, 0)
    tab = lambda i: (i % n_tab, 0)
    in_specs = [
        pl.BlockSpec((tm, d), row), pl.BlockSpec((1, d), const),
        pl.BlockSpec((d, N_IN_PAD), const, pipeline_mode=pl.Buffered(1)),
        pl.BlockSpec((1, D_ATT), const), pl.BlockSpec((1, KV_W), const), pl.BlockSpec((1, KV_W), const),
        pl.BlockSpec((D_ATT, D_ATT), const), pl.BlockSpec((KV_W, KV_W), const),
        pl.BlockSpec((tm, LANES), tab), pl.BlockSpec((tm, LANES), tab), pl.BlockSpec((tm, LANES), tab),
    ]
    widths = [C_CONV, D_ATT] + [KV_W] * 7
    out_shape = [jax.ShapeDtypeStruct((rows, wd), F32) for wd in widths]
    out_specs = [pl.BlockSpec((tm, wd), row) for wd in widths]
    args = [x2, p['mix_norm'].reshape(1, d), w,
            jnp.tile(p['q_norm'], N_HEADS).reshape(1, D_ATT),
            jnp.tile(p['k_sel_norm'], N_KV_HEADS).reshape(1, KV_W),
            jnp.tile(p['k_win_norm'], N_KV_HEADS).reshape(1, KV_W),
            _blockdiag_ones(D_ATT), _blockdiag_ones(KV_W), *tabs]
    return pl.pallas_call(
        _proj_body, out_shape=out_shape, grid=(rows // tm,), in_specs=in_specs, out_specs=out_specs,
        compiler_params=_params(1), name="proj")(*args)


def _conv_post(acc, lg, lb, og):
    mu = jnp.mean(acc, axis=-1, keepdims=True)
    xc = acc - mu
    var = jnp.mean(xc * xc, axis=-1, keepdims=True)
    y = xc * lax.rsqrt(var + EPS) * lg + lb
    y = y * _sigmoid(y)
    return _rms(y, og)


def _conv_body(ub_ref, cw_ref, cb_ref, lg_ref, lb_ref, og_ref, o_ref, *, n_tiles, tt):
    halo = 32

    def step(i, carry):
        t0 = pl.multiple_of(i * tt, tt)
        win = ub_ref[0, pl.ds(t0, tt + halo), :]
        acc = jnp.zeros((tt, C_CONV), F32) + cb_ref[...]
        for r in range(8):
            sh = win[r:r + tt + halo - 8, :]
            for a in range(halo // 8):
                w = 8 * a + r
                if w < CONV_W:
                    acc = acc + sh[8 * a:8 * a + tt, :] * cw_ref[w:w + 1, :]
        o_ref[0, pl.ds(t0, tt), :] = _conv_post(acc, lg_ref[...], lb_ref[...], og_ref[...])
        return carry

    lax.fori_loop(0, n_tiles, step, 0)


def _conv_prompt(u, p, tt=128):
    b, t, c = u.shape
    ub = jnp.pad(u, ((0, 0), (CONV_W - 1, 2), (0, 0)))
    cw = jnp.pad(p['conv_w'], ((0, 1), (0, 0)))
    const = lambda i: (0, 0)
    vec = lambda a: a.reshape(1, c)
    return pl.pallas_call(
        functools.partial(_conv_body, n_tiles=t // tt, tt=tt),
        out_shape=jax.ShapeDtypeStruct((b, t, c), F32), grid=(b,),
        in_specs=[pl.BlockSpec((1, t + 32, c), lambda i: (i, 0, 0)), pl.BlockSpec((32, c), const)]
        + [pl.BlockSpec((1, c), const)] * 4,
        out_specs=pl.BlockSpec((1, t, c), lambda i: (i, 0, 0)),
        compiler_params=_params(1), name="conv_prompt")(
            ub, cw, vec(p['conv_b']), vec(p['conv_ln_g']), vec(p['conv_ln_b']), vec(p['out_norm_conv']))


def _conv_sample_body(st_ref, u_ref, cw_ref, cb_ref, lg_ref, lb_ref, og_ref, o_ref):
    acc = u_ref[...] * cw_ref[CONV_W - 1:CONV_W, :] + cb_ref[...]
    for w in range(CONV_W - 1):
        acc = acc + st_ref[:, w * C_CONV:(w + 1) * C_CONV] * cw_ref[w:w + 1, :]
    o_ref[...] = _conv_post(acc, lg_ref[...], lb_ref[...], og_ref[...])


def _conv_sample(state, u, p):
    b = u.shape[0]
    c = C_CONV
    cw = jnp.pad(p['conv_w'], ((0, 1), (0, 0)))
    vec = lambda a: a.reshape(1, c)
    return pl.pallas_call(
        _conv_sample_body, out_shape=jax.ShapeDtypeStruct((b, c), F32),
        compiler_params=pltpu.CompilerParams(vmem_limit_bytes=VMEM_LIMIT), name="conv_sample")(
            state.reshape(b, (CONV_W - 1) * c), u, cw, vec(p['conv_b']), vec(p['conv_ln_g']),
            vec(p['conv_ln_b']), vec(p['out_norm_conv']))


def _compress_weights(pe, w1, w2):
    assert N_SUB == 2
    w1r = w1.reshape(N_SUB, CMP_STRIDE, HEAD_DIM, CMP_HID).transpose(1, 2, 0, 3)
    eye = jnp.eye(N_KV_HEADS, dtype=w1.dtype)
    wf = jnp.einsum('sdmh,gk->sgdmkh', w1r, eye).reshape(CHUNK_W, N_SUB * N_KV_HEADS * CMP_HID)
    per = jnp.broadcast_to(pe.reshape(N_SUB, CMP_STRIDE, 1, HEAD_DIM),
                           (N_SUB, CMP_STRIDE, N_KV_HEADS, HEAD_DIM)).reshape(N_SUB, CHUNK_W)
    w2f = jnp.einsum('hd,gk->ghkd', w2, eye).reshape(N_KV_HEADS * CMP_HID, KV_W)
    return per, wf.astype(BF16), w2f.astype(BF16)


def _compress_math(x, pe_ref, w1_ref, w2_ref):
    half = N_KV_HEADS * CMP_HID
    a0 = _dot((x + pe_ref[0:1, :]).astype(BF16), w1_ref[:, :half])
    a1 = _dot((x + pe_ref[1:2, :]).astype(BF16), w1_ref[:, half:])
    n = x.shape[0]
    h = a0 + pltpu.roll(a1, n - 1, 0)
    t = h * _sigmoid(h)
    return _dot(t.astype(BF16), w2_ref[...])


def _finish_cmp(xk, xv, refs, ok_ref, ov_ref):
    pek, w1k, w2k, kng, bd1, c, sa, sb, pev, w1v, w2v = refs
    n = xk.shape[0]
    is_tok = lax.broadcasted_iota(jnp.int32, (n, KV_W), 0) < n - 1
    tk = _compress_math(xk, pek, w1k, w2k)
    tk = _rope(_group_rms(tk, kng[...], bd1[...]), c[...], sa[...], sb[...])
    ok_ref[0] = jnp.where(is_tok, tk, 0.0)
    ov_ref[0] = jnp.where(is_tok, _compress_math(xv, pev, w1v, w2v), 0.0)


def _cmp_prompt_body(xk_ref, xv_ref, *refs):
    _finish_cmp(xk_ref[0], xv_ref[0], refs[:11], refs[11], refs[12])


def _cmp_sample_body(pt_ref, kc_hbm, vc_hbm, *refs, n_pages):
    const_refs, ok_ref, ov_ref, kbuf, vbuf, sem = refs[:11], refs[11], refs[12], refs[13], refs[14], refs[15]
    b = pl.program_id(0)
    rows = kc_hbm.shape[1]

    def copies(page, slot):
        dst = pl.ds(pl.multiple_of(slot * rows, rows), rows)
        return (pltpu.make_async_copy(kc_hbm.at[page], kbuf.at[dst], sem.at[0]),
                pltpu.make_async_copy(vc_hbm.at[page], vbuf.at[dst], sem.at[1]))

    def issue(i, carry):
        for cp in copies(pt_ref[b * n_pages + i], i):
            cp.start()
        return carry

    def wait(i, carry):
        for cp in copies(0, i):
            cp.wait()
        return carry

    lax.fori_loop(0, n_pages, issue, 0)
    lax.fori_loop(0, n_pages, wait, 0)
    _finish_cmp(kbuf[...], vbuf[...], const_refs, ok_ref, ov_ref)


def _cmp_consts(p, n_chunk):
    pek, w1k, w2k = _compress_weights(p['cmp_k_pos'], p['cmp_k_w1'], p['cmp_k_w2'])
    pev, w1v, w2v = _compress_weights(p['cmp_v_pos'], p['cmp_v_w1'], p['cmp_v_w2'])
    cmp_end = jnp.arange(n_chunk) * CMP_STRIDE + CMP_LEN - 1
    c, sa, sb = _rope_tables(cmp_end)
    kng = jnp.tile(p['k_cmp_norm'], N_KV_HEADS).reshape(1, KV_W)
    args = [pek, w1k, w2k, kng, _blockdiag_ones(KV_W), c, sa, sb, pev, w1v, w2v]
    shapes = [a.shape for a in args]
    return args, shapes


def _cmp_prompt(kc, vc, p):
    b, t, _ = kc.shape
    n_chunk = t // CMP_STRIDE
    args, shapes = _cmp_consts(p, n_chunk)
    blk = pl.BlockSpec((1, n_chunk, CHUNK_W), lambda i: (i, 0, 0))
    out = pl.BlockSpec((1, n_chunk, KV_W), lambda i: (i, 0, 0))
    return pl.pallas_call(
        _cmp_prompt_body, out_shape=[jax.ShapeDtypeStruct((b, n_chunk, KV_W), F32)] * 2, grid=(b,),
        in_specs=[blk, blk] + [pl.BlockSpec(s, lambda i: (0, 0)) for s in shapes],
        out_specs=[out, out], compiler_params=_params(1), name="cmp_prompt")(
            kc.reshape(b, n_chunk, CHUNK_W), vc.reshape(b, n_chunk, CHUNK_W), *args)


def _cmp_sample(page_table, cache_k, cache_v, p):
    b, n_pages = page_table.shape
    n_pool, page = cache_k.shape[0], cache_k.shape[1]
    rows = page // CMP_STRIDE
    n_chunk = n_pages * rows
    args, shapes = _cmp_consts(p, n_chunk)
    out = pl.BlockSpec((1, n_chunk, KV_W), lambda i, pt: (i, 0, 0))
    grid_spec = pltpu.PrefetchScalarGridSpec(
        num_scalar_prefetch=1, grid=(b,),
        in_specs=[pl.BlockSpec(memory_space=pl.ANY)] * 2
        + [pl.BlockSpec(s, lambda i, pt: (0, 0)) for s in shapes],
        out_specs=[out, out],
        scratch_shapes=[pltpu.VMEM((n_chunk, CHUNK_W), F32), pltpu.VMEM((n_chunk, CHUNK_W), F32),
                        pltpu.SemaphoreType.DMA((2,))])
    return pl.pallas_call(
        functools.partial(_cmp_sample_body, n_pages=n_pages),
        out_shape=[jax.ShapeDtypeStruct((b, n_chunk, KV_W), F32)] * 2, grid_spec=grid_spec,
        compiler_params=_params(1), name="cmp_sample")(
            page_table.reshape(-1), cache_k.reshape(n_pool, rows, CHUNK_W),
            cache_v.reshape(n_pool, rows, CHUNK_W), *args)


def _cmp_to_sel(n_chunk, n_cmp, n_sel, width):
    cs = np.arange(n_chunk)[:, None] * CMP_STRIDE
    ss = np.arange(width)[None, :] * SEL_BLK
    ov = np.clip(np.minimum(cs + CMP_LEN, ss + SEL_BLK) - np.maximum(cs, ss), 0, None) / CMP_LEN
    ov = ov * (np.arange(n_chunk)[:, None] < n_cmp) * (np.arange(width)[None, :] < n_sel)
    return jnp.asarray(ov, dtype=BF16)


def _attn_prompt_body(q_ref, gt_ref, kc_ref, vc_ref, ks_ref, vs_ref, kw_ref, vw_ref, ov_ref, o_ref,
                      qh_sc, sel_sc, oc_sc, m_sc, l_sc, acc_sc, *, t, tq, kc_len, n_cmp, n_sel):
    off = pl.program_id(1) * tq
    lane = lax.broadcasted_iota(jnp.int32, (tq, LANES), 1)
    qpos = off + lax.broadcasted_iota(jnp.int32, (tq, 1), 0)
    q = q_ref[0]
    for h in range(N_HEADS):
        qh_sc[h] = _head_to_group_lanes(q, h, lane).astype(BF16)

    n_chunk = kc_ref.shape[1]
    kcm = kc_ref[0].astype(BF16)
    vcm = vc_ref[0].astype(BF16)
    ncol = lax.broadcasted_iota(jnp.int32, (tq, n_chunk), 1)
    mask_c = jnp.logical_and(ncol * CMP_STRIDE + (CMP_LEN - 1) <= qpos, ncol < n_cmp)
    psum = [jnp.zeros((tq, n_chunk), F32) for _ in range(N_KV_HEADS)]
    for h in range(N_HEADS):
        pc = _masked_softmax(_dot_t(qh_sc[h], kcm) * SCALE, mask_c)
        oc_sc[h] = _dot(pc.astype(BF16), vcm)
        psum[h // GRP] = psum[h // GRP] + pc

    cur = qpos // SEL_BLK
    valid = lane <= cur
    forced = jnp.logical_or(lane == 0, jnp.logical_or(lane == cur, lane == cur - 1))
    bonus = jnp.where(forced, FORCE_BONUS, 0.0)
    k_eff = min(TOPK, n_sel)
    for g in range(N_KV_HEADS):
        imp = _split_dot(psum[g], ov_ref[...])
        score = jnp.where(valid, imp + bonus, NEG_INF)
        rank = jnp.zeros((tq, LANES), F32)
        for j in range(n_sel):
            col = score[:, j:j + 1]
            tie = jnp.where(lane > j, 1.0, 0.0)
            rank = rank + jnp.where(col > score, 1.0, jnp.where(col == score, tie, 0.0))
        sel_sc[g] = jnp.where(rank < k_eff, 1.0, 0.0).astype(BF16)

    for h in range(N_HEADS):
        m_sc[h] = jnp.full((tq, 1), NEG_INF, F32)
        l_sc[h] = jnp.zeros((tq, 1), F32)
        acc_sc[h] = jnp.zeros((tq, LANES), F32)
    blk_row = lax.broadcasted_iota(jnp.int32, (LANES, kc_len), 0)
    key_col = lax.broadcasted_iota(jnp.int32, (LANES, kc_len), 1)
    for ci in range(t // kc_len):
        @pl.when(ci * kc_len < off + tq)
        def _(ci=ci):
            base = ci * kc_len
            expand = jnp.where(blk_row == (base + key_col) // SEL_BLK, 1.0, 0.0).astype(BF16)
            kpos = base + lax.broadcasted_iota(jnp.int32, (tq, kc_len), 1)
            causal = kpos <= qpos
            for g in range(N_KV_HEADS):
                k = ks_ref[0, base:base + kc_len, :].astype(BF16)
                v = vs_ref[0, base:base + kc_len, :].astype(BF16)
                mask = jnp.logical_and(_dot(sel_sc[g], expand) > 0.5, causal)
                for r in range(GRP):
                    h = g * GRP + r
                    s = jnp.where(mask, _dot_t(qh_sc[h], k) * SCALE, NEG_INF)
                    m_old = m_sc[h]
                    m_new = jnp.maximum(m_old, jnp.max(s, axis=-1, keepdims=True))
                    alpha = jnp.exp(m_old - m_new)
                    pe = jnp.where(mask, jnp.exp(s - m_new), 0.0)
                    l_sc[h] = alpha * l_sc[h] + jnp.sum(pe, axis=-1, keepdims=True)
                    acc_sc[h] = alpha * acc_sc[h] + _dot(pe.astype(BF16), v)
                    m_sc[h] = m_new

    wl = WINDOW + tq
    ws = pl.multiple_of(jnp.maximum(off - WINDOW, 0), tq)
    kw = kw_ref[0, pl.ds(ws, wl), :].astype(BF16)
    vw = vw_ref[0, pl.ds(ws, wl), :].astype(BF16)
    dist = qpos - (ws + lax.broadcasted_iota(jnp.int32, (tq, wl), 1))
    mask_w = jnp.logical_and(dist >= 0, dist < WINDOW)
    gt = gt_ref[0]
    outs = []
    for h in range(N_HEADS):
        pw = _masked_softmax(_dot_t(qh_sc[h], kw) * SCALE, mask_w)
        o_w = _dot(pw.astype(BF16), vw)
        o_s = acc_sc[h] / jnp.maximum(l_sc[h], 1e-30)
        outs.append(gt[:, 3 * h:3 * h + 1] * oc_sc[h] + gt[:, 3 * h + 1:3 * h + 2] * o_s
                    + gt[:, 3 * h + 2:3 * h + 3] * o_w)
    o_ref[0] = jnp.concatenate(
        [_pair_tile(outs[2 * p], outs[2 * p + 1], 2 * p, lane) for p in range(N_HEADS // 2)], axis=-1)


def _attn_prompt(q, gt, kcmp, vcmp, ks, vs, kw, vw, tq=128, kc_len=512):
    b, t, _ = q.shape
    n_chunk = kcmp.shape[1]
    n_cmp = n_chunk - N_SUB + 1
    n_sel = -(-t // SEL_BLK)
    assert t % kc_len == 0 and t >= WINDOW + tq and n_sel <= LANES and n_chunk % LANES == 0
    ov = _cmp_to_sel(n_chunk, n_cmp, n_sel, LANES)
    qblk = lambda i, j: (i, j, 0)
    full = lambda i, j: (i, 0, 0)
    kv = pl.BlockSpec((1, t, KV_W), full)
    cm = pl.BlockSpec((1, n_chunk, KV_W), full)
    body = functools.partial(_attn_prompt_body, t=t, tq=tq, kc_len=kc_len, n_cmp=n_cmp, n_sel=n_sel)
    return pl.pallas_call(
        body, out_shape=jax.ShapeDtypeStruct((b, t, D_ATT), F32), grid=(b, t // tq),
        in_specs=[pl.BlockSpec((1, tq, D_ATT), qblk), pl.BlockSpec((1, tq, LANES), qblk), cm, cm,
                  kv, kv, kv, kv, pl.BlockSpec((n_chunk, LANES), lambda i, j: (0, 0))],
        out_specs=pl.BlockSpec((1, tq, D_ATT), qblk),
        scratch_shapes=[pltpu.VMEM((N_HEADS, tq, LANES), BF16), pltpu.VMEM((N_KV_HEADS, tq, LANES), BF16),
                        pltpu.VMEM((N_HEADS, tq, LANES), F32), pltpu.VMEM((N_HEADS, tq, 1), F32),
                        pltpu.VMEM((N_HEADS, tq, 1), F32), pltpu.VMEM((N_HEADS, tq, LANES), F32)],
        compiler_params=_params(2), name="attn_prompt")(q, gt, kcmp, vcmp, ks, vs, kw, vw, ov)


def _sample_heads(q_row):
    lane = lax.broadcasted_iota(jnp.int32, (1, LANES), 1)
    return jnp.concatenate([_head_to_group_lanes(q_row, h, lane) for h in range(N_HEADS)], axis=0)


def _attn_s1_body(q_ref, kc_ref, vc_ref, ov_ref, oc_ref, idx_ref, *, n_cmp, n_sel, q_pos):
    qh = _sample_heads(q_ref[0])
    n_chunk = kc_ref.shape[1]
    ncol = lax.broadcasted_iota(jnp.int32, (N_HEADS, n_chunk), 1)
    mask = jnp.logical_and(ncol * CMP_STRIDE + (CMP_LEN - 1) <= q_pos, ncol < n_cmp)
    pc = _masked_softmax(_dot_t(qh.astype(BF16), kc_ref[0].astype(BF16)) * SCALE, mask)
    oc_ref[0] = _dot(pc.astype(BF16), vc_ref[0].astype(BF16))
    rows = [jnp.sum(pc[g * GRP:(g + 1) * GRP, :], axis=0, keepdims=True) for g in range(N_KV_HEADS)]
    psum = jnp.concatenate(rows + [jnp.zeros((N_HEADS - N_KV_HEADS, n_chunk), F32)], axis=0)
    imp = _split_dot(psum, ov_ref[...])
    width = imp.shape[1]
    jl = lax.broadcasted_iota(jnp.int32, (N_HEADS, width), 1)
    cur = q_pos // SEL_BLK
    forced = jnp.logical_or(jl == 0, jnp.logical_or(jl == cur, jl == cur - 1))
    score = jnp.where(jl <= cur, imp + jnp.where(forced, FORCE_BONUS, 0.0), NEG_INF)
    ii = lax.broadcasted_iota(jnp.int32, (width, width), 0)
    jj = lax.broadcasted_iota(jnp.int32, (width, width), 1)
    slot = lax.broadcasted_iota(jnp.int32, (width, LANES), 1).astype(F32)
    blk = lax.broadcasted_iota(jnp.int32, (width, LANES), 0).astype(F32)
    tie = jnp.where(jj < ii, 1.0, 0.0)
    out_rows = []
    for g in range(N_KV_HEADS):
        row = jnp.broadcast_to(score[g:g + 1, :], (width, width))
        col = jnp.sum(jnp.where(ii == jj, row, 0.0), axis=1, keepdims=True)
        beats = jnp.where(row > col, 1.0, jnp.where(row == col, tie, 0.0))
        rank = jnp.sum(beats, axis=1, keepdims=True)
        out_rows.append(jnp.sum(jnp.where(rank == slot, blk, 0.0), axis=0, keepdims=True))
    out_rows.append(jnp.zeros((N_HEADS - N_KV_HEADS, LANES), F32))
    idx_ref[0] = jnp.concatenate(out_rows, axis=0).astype(jnp.int32)


def _attn_s1(q, kcmp, vcmp, q_pos, n_sel):
    b = q.shape[0]
    n_chunk = kcmp.shape[1]
    n_cmp = n_chunk - N_SUB + 1
    width = -(-n_sel // LANES) * LANES
    ov = _cmp_to_sel(n_chunk, n_cmp, n_sel, width)
    blk3 = lambda i: (i, 0, 0)
    cm = pl.BlockSpec((1, n_chunk, KV_W), blk3)
    out = pl.BlockSpec((1, N_HEADS, LANES), blk3)
    return pl.pallas_call(
        functools.partial(_attn_s1_body, n_cmp=n_cmp, n_sel=n_sel, q_pos=q_pos),
        out_shape=[jax.ShapeDtypeStruct((b, N_HEADS, LANES), F32),
                   jax.ShapeDtypeStruct((b, N_HEADS, LANES), jnp.int32)],
        grid=(b,),
        in_specs=[pl.BlockSpec((1, 1, D_ATT), blk3), cm, cm, pl.BlockSpec(ov.shape, lambda i: (0, 0))],
        out_specs=[out, out], compiler_params=_params(1), name="attn_sample_select")(
            q.reshape(b, 1, D_ATT), kcmp, vcmp, ov)


def _extra_key_attention(qh, k, v, mask, k_new, v_new):
    qf = qh.astype(F32)
    s = jnp.where(mask, _dot_t(qh, k.astype(BF16)) * SCALE, NEG_INF)
    s_new = jnp.sum(qf * k_new, axis=-1, keepdims=True) * SCALE
    m = jnp.maximum(jnp.max(s, axis=-1, keepdims=True), s_new)
    e = jnp.where(mask, jnp.exp(s - m), 0.0)
    e_new = jnp.exp(s_new - m)
    denom = jnp.maximum(jnp.sum(e, axis=-1, keepdims=True) + e_new, 1e-30)
    return (_dot(e.astype(BF16), v.astype(BF16)) + e_new * v_new) / denom


def _attn_s2_body(idx_ref, pt_ref, q_ref, gt_ref, oc_ref, ksn_ref, vsn_ref, kwn_ref, vwn_ref,
                  kws_ref, vws_ref, kcache, vcache, o_ref, kbuf, vbuf, sem, *, n_pages, n_cache_blk, q_pos):
    b = pl.program_id(0)
    per_page = kcache.shape[1] // SEL_BLK

    def copies(g, k, blk):
        blk = jnp.minimum(blk, n_cache_blk - 1)
        page = pt_ref[b * n_pages + blk // per_page]
        src = pl.ds(pl.multiple_of((blk % per_page) * SEL_BLK, SEL_BLK), SEL_BLK)
        dst = pl.ds(k * SEL_BLK, SEL_BLK)
        return (pltpu.make_async_copy(kcache.at[page, src, :], kbuf.at[g, dst, :], sem.at[0]),
                pltpu.make_async_copy(vcache.at[page, src, :], vbuf.at[g, dst, :], sem.at[1]))

    blks = [[idx_ref[(b * N_KV_HEADS + g) * TOPK + k] for k in range(TOPK)] for g in range(N_KV_HEADS)]
    for g in range(N_KV_HEADS):
        for k in range(TOPK):
            for cp in copies(g, k, blks[g][k]):
                cp.start()
    for g in range(N_KV_HEADS):
        for k in range(TOPK):
            for cp in copies(g, k, blks[g][k]):
                cp.wait()

    qh = _sample_heads(q_ref[0]).astype(BF16)
    n_keys = TOPK * SEL_BLK
    row_grp = lax.broadcasted_iota(jnp.int32, (N_HEADS, n_keys), 0) // GRP
    slot = lax.broadcasted_iota(jnp.int32, (N_HEADS, n_keys), 1) // SEL_BLK
    in_cache = jnp.zeros((N_HEADS, n_keys), F32)
    for g in range(N_KV_HEADS):
        for k in range(TOPK):
            flag = jnp.where(blks[g][k] < n_cache_blk, 1.0, 0.0)
            in_cache = jnp.where(jnp.logical_and(row_grp == g, slot == k), flag, in_cache)
    mask_s = in_cache > 0.5
    o_sel = [_extra_key_attention(qh, kbuf[g], vbuf[g], mask_s, ksn_ref[0], vsn_ref[0])
             for g in range(N_KV_HEADS)]
    out_grp = lax.broadcasted_iota(jnp.int32, (N_HEADS, LANES), 0) // GRP
    o_s = jnp.where(out_grp == 0, o_sel[0], o_sel[1])

    n_win = kws_ref.shape[1]
    kpos = (q_pos - n_win) + lax.broadcasted_iota(jnp.int32, (N_HEADS, n_win), 1)
    dist = q_pos - kpos
    mask_w = jnp.logical_and(dist >= 0, dist < WINDOW)
    o_w = _extra_key_attention(qh, kws_ref[0], vws_ref[0], mask_w, kwn_ref[0], vwn_ref[0])

    lane8 = lax.broadcasted_iota(jnp.int32, (N_HEADS, LANES), 1)
    head8 = lax.broadcasted_iota(jnp.int32, (N_HEADS, LANES), 0)
    gt = jnp.broadcast_to(gt_ref[0], (N_HEADS, LANES))
    gate = [jnp.sum(jnp.where(lane8 == 3 * head8 + j, gt, 0.0), axis=-1, keepdims=True) for j in range(3)]
    o = gate[0] * oc_ref[0] + gate[1] * o_s + gate[2] * o_w
    lane = lax.broadcasted_iota(jnp.int32, (1, LANES), 1)
    o_ref[0] = jnp.concatenate(
        [_pair_tile(o[2 * p:2 * p + 1, :], o[2 * p + 1:2 * p + 2, :], 2 * p, lane)
         for p in range(N_HEADS // 2)], axis=-1)


def _attn_s2(idx, page_table, q, gt, oc, ks_new, vs_new, kw_new, vw_new, kw_state, vw_state,
             cache_k, cache_v, q_pos):
    b, n_pages = page_table.shape
    n_pool, page = cache_k.shape[0], cache_k.shape[1]
    n_win = kw_state.shape[1]
    assert n_win <= WINDOW
    blk3 = lambda i, *_: (i, 0, 0)
    row = lambda w: pl.BlockSpec((1, 1, w), blk3)
    grid_spec = pltpu.PrefetchScalarGridSpec(
        num_scalar_prefetch=2, grid=(b,),
        in_specs=[row(D_ATT), row(LANES), pl.BlockSpec((1, N_HEADS, LANES), blk3),
                  row(KV_W), row(KV_W), row(KV_W), row(KV_W),
                  pl.BlockSpec((1, n_win, KV_W), blk3), pl.BlockSpec((1, n_win, KV_W), blk3),
                  pl.BlockSpec(memory_space=pl.ANY), pl.BlockSpec(memory_space=pl.ANY)],
        out_specs=row(D_ATT),
        scratch_shapes=[pltpu.VMEM((N_KV_HEADS, TOPK * SEL_BLK, KV_W), F32),
                        pltpu.VMEM((N_KV_HEADS, TOPK * SEL_BLK, KV_W), F32),
                        pltpu.SemaphoreType.DMA((2,))])
    r3 = lambda a: a.reshape(b, 1, a.shape[-1])
    body = functools.partial(_attn_s2_body, n_pages=n_pages, n_cache_blk=n_pages * (page // SEL_BLK),
                             q_pos=q_pos)
    return pl.pallas_call(
        body, out_shape=jax.ShapeDtypeStruct((b, 1, D_ATT), F32), grid_spec=grid_spec,
        compiler_params=_params(1), name="attn_sample")(
            idx.reshape(-1), page_table.reshape(-1), r3(q), r3(gt), oc, r3(ks_new), r3(vs_new),
            r3(kw_new), r3(vw_new), kw_state, vw_state,
            cache_k.reshape(n_pool, page, KV_W), cache_v.reshape(n_pool, page, KV_W)).reshape(b, D_ATT)


def _out_body(x_ref, c_ref, o_ref, og_ref, wc_ref, wa_ref, y_ref):
    on = _rms(o_ref[...], og_ref[...])
    y_ref[...] = (x_ref[...] + _dot(c_ref[...].astype(BF16), wc_ref[...])
                  + _dot(on.astype(BF16), wa_ref[...]))


def _out_proj(x2, cn, o, p, tm):
    rows, d = x2.shape
    w = p['w_out'].astype(BF16)
    row = lambda i: (i, 0)
    const = lambda i: (0, 0)
    return pl.pallas_call(
        _out_body, out_shape=jax.ShapeDtypeStruct((rows, d), F32), grid=(rows // tm,),
        in_specs=[pl.BlockSpec((tm, d), row), pl.BlockSpec((tm, C_CONV), row), pl.BlockSpec((tm, D_ATT), row),
                  pl.BlockSpec((1, D_ATT), const), pl.BlockSpec((C_CONV, d), const),
                  pl.BlockSpec((D_ATT, d), const)],
        out_specs=pl.BlockSpec((tm, d), row), compiler_params=_params(1), name="out_proj")(
            x2, cn, o, p['out_norm_attn'].reshape(1, D_ATT), w[:C_CONV], w[C_CONV:])


def _prompt_layer(x, p, tm=512):
    b, t, d = x.shape
    x2 = _ffn(x.reshape(b * t, d), p['ffn1_norm'], p['ffn1_w_in'], p['ffn1_w_out'], None, tm)
    u, q, kc, vc, ks, vs, kw, vw, gt = _proj(x2, p, _rope_tables(jnp.arange(t)), tm)
    r3 = lambda a: a.reshape(b, t, a.shape[-1])
    u3 = r3(u)
    cn = _conv_prompt(u3, p)
    kcmp, vcmp = _cmp_prompt(r3(kc), r3(vc), p)
    o = _attn_prompt(r3(q), r3(gt), kcmp, vcmp, r3(ks), r3(vs), r3(kw), r3(vw))
    x2 = _out_proj(x2, cn.reshape(b * t, C_CONV), o.reshape(b * t, D_ATT), p, tm)
    y = _ffn(x2, p['ffn2_norm'], p['ffn2_w_in'], p['ffn2_w_out'], p['final_norm'], tm)
    r4 = lambda a: a.reshape(b, t, N_KV_HEADS, HEAD_DIM)
    keep = min(WINDOW, t)
    state = (r4(kc), r4(vc), r4(ks), r4(vs), r4(kw)[:, t - keep:], r4(vw)[:, t - keep:],
             u3[:, t - (CONV_W - 1):])
    return y.reshape(b, t, d), state


def _sample_layer(x, p, cache_k_cmp, cache_v_cmp, cache_k_sel, cache_v_sel, kw_state, vw_state,
                  conv_state, page_table):
    b, t, d = x.shape
    assert t == 1
    past_len = page_table.shape[1] * cache_k_cmp.shape[1]
    x2 = _ffn(x.reshape(b, d), p['ffn1_norm'], p['ffn1_w_in'], p['ffn1_w_out'], None, b)
    tabs = _rope_tables(jnp.full((b,), past_len, jnp.int32))
    u, q, kc, vc, ks, vs, kw, vw, gt = _proj(x2, p, tabs, b)
    cn = _conv_sample(conv_state, u, p)
    kcmp, vcmp = _cmp_sample(page_table, cache_k_cmp, cache_v_cmp, p)
    n_sel = -(-(past_len + 1) // SEL_BLK)
    oc, idx = _attn_s1(q, kcmp, vcmp, past_len, n_sel)
    n_win = kw_state.shape[1]
    o = _attn_s2(idx[:, :N_KV_HEADS, :TOPK], page_table, q, gt, oc, ks, vs, kw, vw,
                 kw_state.reshape(b, n_win, KV_W), vw_state.reshape(b, n_win, KV_W),
                 cache_k_sel, cache_v_sel, past_len)
    x2 = _out_proj(x2, cn, o, p, b)
    y = _ffn(x2, p['ffn2_norm'], p['ffn2_w_in'], p['ffn2_w_out'], p['final_norm'], b)
    r4 = lambda a: a.reshape(b, 1, N_KV_HEADS, HEAD_DIM)
    keep = min(WINDOW, n_win + 1)
    new_kw = jnp.concatenate([kw_state, r4(kw)], axis=1)[:, n_win + 1 - keep:]
    new_vw = jnp.concatenate([vw_state, r4(vw)], axis=1)[:, n_win + 1 - keep:]
    new_conv = jnp.concatenate([conv_state, u[:, None, :]], axis=1)[:, 1:]
    return y.reshape(b, 1, d), (r4(kc), r4(vc), r4(ks), r4(vs), new_kw, new_vw, new_conv)


_PARAM_NAMES = ('ffn1_norm', 'ffn1_w_in', 'ffn1_w_out', 'mix_norm', 'w_in', 'conv_w', 'conv_b', 'conv_ln_g',
                'conv_ln_b', 'q_norm', 'k_cmp_norm', 'k_sel_norm', 'k_win_norm', 'cmp_k_pos', 'cmp_k_w1',
                'cmp_k_w2', 'cmp_v_pos', 'cmp_v_w1', 'cmp_v_w2', 'out_norm_conv', 'out_norm_attn', 'w_out',
                'ffn2_norm', 'ffn2_w_in', 'ffn2_w_out', 'final_norm')


def kernel(x_prompt, x_sample, cache_k_cmp, cache_v_cmp, cache_k_sel, cache_v_sel, state_k_win, state_v_win,
           state_conv, page_table, ffn1_norm, ffn1_w_in, ffn1_w_out, mix_norm, w_in, conv_w, conv_b, conv_ln_g,
           conv_ln_b, q_norm, k_cmp_norm, k_sel_norm, k_win_norm, cmp_k_pos, cmp_k_w1, cmp_k_w2, cmp_v_pos,
           cmp_v_w1, cmp_v_w2, out_norm_conv, out_norm_attn, w_out, ffn2_norm, ffn2_w_in, ffn2_w_out, final_norm):
    stacked = dict(zip(_PARAM_NAMES, (
        ffn1_norm, ffn1_w_in, ffn1_w_out, mix_norm, w_in, conv_w, conv_b, conv_ln_g, conv_ln_b, q_norm,
        k_cmp_norm, k_sel_norm, k_win_norm, cmp_k_pos, cmp_k_w1, cmp_k_w2, cmp_v_pos, cmp_v_w1, cmp_v_w2,
        out_norm_conv, out_norm_attn, w_out, ffn2_norm, ffn2_w_in, ffn2_w_out, final_norm)))
    depth = ffn1_norm.shape[0]
    yp, ys = x_prompt, x_sample
    new_p, new_s = [], []
    for l in range(depth):
        p = {k: v[l] for k, v in stacked.items()}
        yp, st_p = _prompt_layer(yp, p)
        new_p.append(st_p)
        ys, st_s = _sample_layer(ys, p, cache_k_cmp[l], cache_v_cmp[l], cache_k_sel[l], cache_v_sel[l],
                                 state_k_win[l], state_v_win[l], state_conv[l], page_table)
        new_s.append(st_s)
    outs_p = [jnp.stack(tup) for tup in zip(*new_p)]
    outs_s = [jnp.stack(tup) for tup in zip(*new_s)]
    return (yp, ys, *outs_p, *outs_s)
```

```python
import functools

import numpy as np
import jax
import jax.numpy as jnp
from jax import lax
from jax.experimental import pallas as pl
from jax.experimental.pallas import tpu as pltpu

F32 = jnp.float32
BF16 = jnp.bfloat16

HEAD_DIM = 64
N_HEADS = 8
N_KV_HEADS = 2
GRP = N_HEADS // N_KV_HEADS
KV_W = N_KV_HEADS * HEAD_DIM
C_CONV = 512
D_ATT = 512
CONV_W = 31
CMP_LEN = 32
CMP_STRIDE = 16
N_SUB = CMP_LEN // CMP_STRIDE
CMP_HID = 2 * HEAD_DIM
SEL_BLK = 64
TOPK = 16
WINDOW = 512
ROT_DIM = HEAD_DIM // 4
ROT_HALF = ROT_DIM // 2
ROPE_THETA = 500000.0
NEG_INF = -1e30
FORCE_BONUS = 1e4
EPS = 1e-6
SCALE = HEAD_DIM ** -0.5
LOG2E = 1.4426950408889634
LANES = 128
ROW_TILE = 32
CHUNK_W = CMP_STRIDE * KV_W
VMEM_LIMIT = 56 * 1024 * 1024


def _params(n_grid_dims):
    return pltpu.CompilerParams(
        dimension_semantics=("arbitrary",) * n_grid_dims, vmem_limit_bytes=VMEM_LIMIT)


def _dot(a, b):
    return jnp.dot(a, b, preferred_element_type=F32)


def _dot_t(a, b):
    return lax.dot_general(a, b, (((1,), (1,)), ((), ())), preferred_element_type=F32)


def _split_dot(x, m_bf16):
    hi = x.astype(BF16)
    lo = (x - hi.astype(F32)).astype(BF16)
    return _dot(hi, m_bf16) + _dot(lo, m_bf16)


def _sigmoid(x):
    return 1.0 / (1.0 + jnp.exp(-x))


def _rms(x, g):
    return x * lax.rsqrt(jnp.mean(x * x, axis=-1, keepdims=True) + EPS) * g


def _group_rms(x, g_tiled, ones_blockdiag):
    ss = _split_dot(x * x, ones_blockdiag)
    return x * lax.rsqrt(ss * (1.0 / HEAD_DIM) + EPS) * g_tiled


def _rope(x, c, sa, sb):
    n = x.shape[-1]
    return x * c + pltpu.roll(x, n - ROT_HALF, 1) * sa + pltpu.roll(x, ROT_HALF, 1) * sb


def _tile_lanes(t, reps):
    return jnp.concatenate([t] * reps, axis=-1) if reps > 1 else t


def _masked_softmax(s, mask):
    s = jnp.where(mask, s, NEG_INF)
    m = jnp.max(s, axis=-1, keepdims=True)
    e = jnp.where(mask, jnp.exp(s - m), 0.0)
    return e / jnp.maximum(jnp.sum(e, axis=-1, keepdims=True), 1e-30)


def _head_to_group_lanes(q, h, lane):
    g, p = h // GRP, h // 2
    qp = q[:, LANES * p:LANES * (p + 1)]
    src = qp if (h % 2) == g else pltpu.roll(qp, HEAD_DIM, 1)
    keep = (lane < HEAD_DIM) if g == 0 else (lane >= HEAD_DIM)
    return jnp.where(keep, src, 0.0)


def _pair_tile(o_even, o_odd, h_even, lane):
    g = h_even // GRP
    a = o_even if g == 0 else pltpu.roll(o_even, HEAD_DIM, 1)
    b = pltpu.roll(o_odd, HEAD_DIM, 1) if g == 0 else o_odd
    return jnp.where(lane < HEAD_DIM, a, b)


def _ffn_body(*refs, n_chunk, tf, ff, final):
    if final:
        x_ref, g_ref, wi_ref, wo_ref, fg_ref, o_ref = refs
    else:
        x_ref, g_ref, wi_ref, wo_ref, o_ref = refs
    x = x_ref[...]
    xn = _rms(x, g_ref[...]).astype(BF16)
    acc = jnp.zeros(x.shape, F32)
    for c in range(n_chunk):
        a = _dot(xn, wi_ref[:, c * tf:(c + 1) * tf])
        b = _dot(xn, wi_ref[:, ff + c * tf:ff + (c + 1) * tf])
        h = (a * _sigmoid(a)) * b
        acc = acc + _dot(h.astype(BF16), wo_ref[c * tf:(c + 1) * tf, :])
    y = x + 0.5 * acc
    if final:
        y = _rms(y, fg_ref[...])
    o_ref[...] = y


def _ffn(x2, norm_g, w_in, w_out, final_g, tm):
    rows, d = x2.shape
    ff = w_out.shape[0]
    tf = 256
    assert rows % tm == 0 and ff % tf == 0 and w_in.shape == (d, 2 * ff)
    const = lambda i: (0, 0)
    single = pl.Buffered(1)
    in_specs = [
        pl.BlockSpec((tm, d), lambda i: (i, 0)),
        pl.BlockSpec((1, d), const),
        pl.BlockSpec((d, 2 * ff), const, pipeline_mode=single),
        pl.BlockSpec((ff, d), const, pipeline_mode=single),
    ]
    args = [x2, norm_g.reshape(1, d), w_in.astype(BF16), w_out.astype(BF16)]
    if final_g is not None:
        in_specs.append(pl.BlockSpec((1, d), const))
        args.append(final_g.reshape(1, d))
    body = functools.partial(_ffn_body, n_chunk=ff // tf, tf=tf, ff=ff, final=final_g is not None)
    return pl.pallas_call(
        body, out_shape=jax.ShapeDtypeStruct((rows, d), F32), grid=(rows // tm,),
        in_specs=in_specs, out_specs=pl.BlockSpec((tm, d), lambda i: (i, 0)),
        compiler_params=_params(1), name="ffn")(*args)


_SEG_GLU = 2 * C_CONV
_SEG_Q = _SEG_GLU + D_ATT
N_IN = _SEG_Q + 6 * KV_W + 3 * N_HEADS
N_IN_PAD = _SEG_Q + 7 * KV_W


def _proj_body(x_ref, g_ref, w_ref, qg_ref, ksg_ref, kwg_ref, bd4_ref, bd1_ref, c_ref, sa_ref, sb_ref,
               u_ref, q_ref, gt_ref, kc_ref, vc_ref, *kv_refs, transposed):
    xn = _rms(x_ref[...], g_ref[...]).astype(BF16)

    def seg(lo, width):
        return _dot(xn, w_ref[:, lo:lo + width])

    c, sa, sb = c_ref[...], sa_ref[...], sb_ref[...]
    reps = D_ATT // LANES
    glu = seg(0, _SEG_GLU)
    u_ref[...] = glu[:, :C_CONV] * _sigmoid(glu[:, C_CONV:])
    q = _group_rms(seg(_SEG_GLU, D_ATT), qg_ref[...], bd4_ref[...])
    q_ref[...] = _rope(q, _tile_lanes(c, reps), _tile_lanes(sa, reps), _tile_lanes(sb, reps))
    base = _SEG_Q
    kc = seg(base, KV_W)
    vc = seg(base + KV_W, KV_W)
    ks = _rope(_group_rms(seg(base + 2 * KV_W, KV_W), ksg_ref[...], bd1_ref[...]), c, sa, sb)
    vs = seg(base + 3 * KV_W, KV_W)
    kw = _rope(_group_rms(seg(base + 4 * KV_W, KV_W), kwg_ref[...], bd1_ref[...]), c, sa, sb)
    vw = seg(base + 5 * KV_W, KV_W)
    gt_ref[...] = _sigmoid(seg(base + 6 * KV_W, KV_W))
    kc_ref[...] = kc
    vc_ref[...] = vc
    if transposed:
        for ref, val in zip(kv_refs, (kc, vc, ks, vs, kw, vw)):
            ref[0] = val.T
    else:
        for ref, val in zip(kv_refs, (ks, vs, kw, vw)):
            ref[...] = val


def _blockdiag_ones(width):
    idx = np.arange(width) // HEAD_DIM
    return jnp.asarray(idx[:, None] == idx[None, :], dtype=BF16)


def _rope_tables(pos):
    inv = ROPE_THETA ** (-(jnp.arange(ROT_HALF, dtype=F32) * 2.0 / ROT_DIM))
    ang = pos.astype(F32)[:, None] * inv[None, :]
    cos, sin = jnp.cos(ang), jnp.sin(ang)
    n = pos.shape[0]
    ones = jnp.ones((n, HEAD_DIM - ROT_DIM), F32)
    zeros = jnp.zeros((n, HEAD_DIM - ROT_DIM), F32)
    z8 = jnp.zeros((n, ROT_HALF), F32)
    c = jnp.concatenate([cos, cos, ones], axis=1)
    sa = jnp.concatenate([-sin, z8, zeros], axis=1)
    sb = jnp.concatenate([z8, sin, zeros], axis=1)
    return tuple(jnp.tile(t, (1, N_KV_HEADS)) for t in (c, sa, sb))


def _proj(x2, p, tabs, tm, seq_len=None):
    rows, d = x2.shape
    n_tab = tabs[0].shape[0] // tm
    w = jnp.pad(p['w_in'], ((0, 0), (0, N_IN_PAD - N_IN))).astype(BF16)
    const = lambda i: (0, 0)
    row = lambda i: (i, 0)
    tab = lambda i: (i % n_tab, 0)
    in_specs = [
        pl.BlockSpec((tm, d), row), pl.BlockSpec((1, d), const),
        pl.BlockSpec((d, N_IN_PAD), const, pipeline_mode=pl.Buffered(1)),
        pl.BlockSpec((1, D_ATT), const), pl.BlockSpec((1, KV_W), const), pl.BlockSpec((1, KV_W), const),
        pl.BlockSpec((D_ATT, D_ATT), const), pl.BlockSpec((KV_W, KV_W), const),
        pl.BlockSpec((tm, LANES), tab), pl.BlockSpec((tm, LANES), tab), pl.BlockSpec((tm, LANES), tab),
    ]
    widths = [C_CONV, D_ATT, LANES, KV_W, KV_W]
    if seq_len is None:
        widths += [KV_W] * 4
    out_shape = [jax.ShapeDtypeStruct((rows, wd), F32) for wd in widths]
    out_specs = [pl.BlockSpec((tm, wd), row) for wd in widths]
    if seq_len is not None:
        nt = seq_len // tm
        assert seq_len % tm == 0 and rows % seq_len == 0
        out_shape += [jax.ShapeDtypeStruct((rows // seq_len, KV_W, seq_len), F32)] * 6
        out_specs += [pl.BlockSpec((1, KV_W, tm), lambda i: (i // nt, 0, i % nt))] * 6
    args = [x2, p['mix_norm'].reshape(1, d), w,
            jnp.tile(p['q_norm'], N_HEADS).reshape(1, D_ATT),
            jnp.tile(p['k_sel_norm'], N_KV_HEADS).reshape(1, KV_W),
            jnp.tile(p['k_win_norm'], N_KV_HEADS).reshape(1, KV_W),
            _blockdiag_ones(D_ATT), _blockdiag_ones(KV_W), *tabs]
    return pl.pallas_call(
        functools.partial(_proj_body, transposed=seq_len is not None), out_shape=out_shape,
        grid=(rows // tm,), in_specs=in_specs, out_specs=out_specs,
        compiler_params=_params(1), name="proj")(*args)


def _conv_post(acc, lg, lb, og):
    mu = jnp.mean(acc, axis=-1, keepdims=True)
    xc = acc - mu
    var = jnp.mean(xc * xc, axis=-1, keepdims=True)
    y = xc * lax.rsqrt(var + EPS) * lg + lb
    y = y * _sigmoid(y)
    return _rms(y, og)


def _conv_body(ub_ref, cw_ref, cb_ref, lg_ref, lb_ref, og_ref, o_ref, *, n_tiles, tt):
    halo = 32

    def step(i, carry):
        t0 = pl.multiple_of(i * tt, tt)
        win = ub_ref[0, pl.ds(t0, tt + halo), :]
        acc = jnp.zeros((tt, C_CONV), F32) + cb_ref[...]
        for r in range(8):
            sh = win[r:r + tt + halo - 8, :]
            for a in range(halo // 8):
                w = 8 * a + r
                if w < CONV_W:
                    acc = acc + sh[8 * a:8 * a + tt, :] * cw_ref[w:w + 1, :]
        o_ref[0, pl.ds(t0, tt), :] = _conv_post(acc, lg_ref[...], lb_ref[...], og_ref[...])
        return carry

    lax.fori_loop(0, n_tiles, step, 0)


def _conv_prompt(u, p, tt=128):
    b, t, c = u.shape
    ub = jnp.pad(u, ((0, 0), (CONV_W - 1, 2), (0, 0)))
    cw = jnp.pad(p['conv_w'], ((0, 1), (0, 0)))
    const = lambda i: (0, 0)
    vec = lambda a: a.reshape(1, c)
    return pl.pallas_call(
        functools.partial(_conv_body, n_tiles=t // tt, tt=tt),
        out_shape=jax.ShapeDtypeStruct((b, t, c), F32), grid=(b,),
        in_specs=[pl.BlockSpec((1, t + 32, c), lambda i: (i, 0, 0)), pl.BlockSpec((32, c), const)]
        + [pl.BlockSpec((1, c), const)] * 4,
        out_specs=pl.BlockSpec((1, t, c), lambda i: (i, 0, 0)),
        compiler_params=_params(1), name="conv_prompt")(
            ub, cw, vec(p['conv_b']), vec(p['conv_ln_g']), vec(p['conv_ln_b']), vec(p['out_norm_conv']))


def _conv_sample_body(st_ref, u_ref, cw_ref, cb_ref, lg_ref, lb_ref, og_ref, o_ref):
    acc = u_ref[...] * cw_ref[CONV_W - 1:CONV_W, :] + cb_ref[...]
    for w in range(CONV_W - 1):
        acc = acc + st_ref[w] * cw_ref[w:w + 1, :]
    o_ref[...] = _conv_post(acc, lg_ref[...], lb_ref[...], og_ref[...])


def _conv_sample(state, u, p):
    b = u.shape[0]
    c = C_CONV
    cw = jnp.pad(p['conv_w'], ((0, 1), (0, 0)))
    vec = lambda a: a.reshape(1, c)
    return pl.pallas_call(
        _conv_sample_body, out_shape=jax.ShapeDtypeStruct((b, c), F32),
        compiler_params=pltpu.CompilerParams(vmem_limit_bytes=VMEM_LIMIT), name="conv_sample")(
            state.transpose(1, 0, 2), u, cw, vec(p['conv_b']), vec(p['conv_ln_g']),
            vec(p['conv_ln_b']), vec(p['out_norm_conv']))


def _compress_weights(pe, w1, w2):
    assert N_SUB == 2
    w1r = w1.reshape(N_SUB, CMP_STRIDE, HEAD_DIM, CMP_HID).transpose(1, 2, 0, 3)
    eye = jnp.eye(N_KV_HEADS, dtype=w1.dtype)
    wf = jnp.einsum('sdmh,gk->sgdmkh', w1r, eye).reshape(CHUNK_W, N_SUB * N_KV_HEADS * CMP_HID)
    per = jnp.broadcast_to(pe.reshape(N_SUB, CMP_STRIDE, 1, HEAD_DIM),
                           (N_SUB, CMP_STRIDE, N_KV_HEADS, HEAD_DIM)).reshape(N_SUB, CHUNK_W)
    w2f = jnp.einsum('hd,gk->ghkd', w2, eye).reshape(N_KV_HEADS * CMP_HID, KV_W)
    return per, wf.astype(BF16), w2f.astype(BF16)


def _per_position_weights(per, wf):
    half = N_KV_HEADS * CMP_HID
    pe_s = per.reshape(N_SUB * CMP_STRIDE, KV_W)
    w_s = wf.reshape(CMP_STRIDE, KV_W, N_SUB, half).transpose(2, 0, 1, 3).reshape(N_SUB * CMP_STRIDE, KV_W, half)
    return pe_s, w_s


def _compress_tail(a0, a1, w2_ref):
    n = a0.shape[0]
    h = a0 + pltpu.roll(a1, n - 1, 0)
    t = h * _sigmoid(h)
    tok = _dot(t.astype(BF16), w2_ref[...])
    return jnp.where(lax.broadcasted_iota(jnp.int32, tok.shape, 0) < n - 1, tok, 0.0)


def _cmp_key_post(tok, kng, bd1, c, sa, sb):
    return _rope(_group_rms(tok, kng[...], bd1[...]), c[...], sa[...], sb[...])


def _compress_chunks(x, pe_ref, w1_ref, w2_ref):
    half = N_KV_HEADS * CMP_HID
    a0 = _dot((x + pe_ref[0:1, :]).astype(BF16), w1_ref[:, :half])
    a1 = _dot((x + pe_ref[1:2, :]).astype(BF16), w1_ref[:, half:])
    return _compress_tail(a0, a1, w2_ref)


def _cmp_prompt_body(xk_ref, xv_ref, pek, w1k, w2k, kng, bd1, c, sa, sb, pev, w1v, w2v, ok_ref, ov_ref):
    ok_ref[0] = _cmp_key_post(_compress_chunks(xk_ref[0], pek, w1k, w2k), kng, bd1, c, sa, sb)
    ov_ref[0] = _compress_chunks(xv_ref[0], pev, w1v, w2v)


def _cmp_prompt(kc, vc, p):
    b, t, _ = kc.shape
    n_chunk = t // CMP_STRIDE
    pek, w1k, w2k = _compress_weights(p['cmp_k_pos'], p['cmp_k_w1'], p['cmp_k_w2'])
    pev, w1v, w2v = _compress_weights(p['cmp_v_pos'], p['cmp_v_w1'], p['cmp_v_w2'])
    args = [pek, w1k, w2k, *_cmp_key_consts(p, n_chunk), pev, w1v, w2v]
    blk = pl.BlockSpec((1, n_chunk, CHUNK_W), lambda i: (i, 0, 0))
    out = pl.BlockSpec((1, n_chunk, KV_W), lambda i: (i, 0, 0))
    return pl.pallas_call(
        _cmp_prompt_body, out_shape=[jax.ShapeDtypeStruct((b, n_chunk, KV_W), F32)] * 2, grid=(b,),
        in_specs=[blk, blk] + [pl.BlockSpec(a.shape, lambda i: (0, 0)) for a in args],
        out_specs=[out, out], compiler_params=_params(1), name="cmp_prompt")(
            kc.reshape(b, n_chunk, CHUNK_W), vc.reshape(b, n_chunk, CHUNK_W), *args)


def _cmp_key_consts(p, n_chunk):
    cmp_end = jnp.arange(n_chunk) * CMP_STRIDE + CMP_LEN - 1
    kng = jnp.tile(p['k_cmp_norm'], N_KV_HEADS).reshape(1, KV_W)
    return [kng, _blockdiag_ones(KV_W), *_rope_tables(cmp_end)]


def _cmp_sample_body(pt_ref, cache_hbm, pe_ref, w1_ref, w2_ref, *rest, n_pages, n_batch, is_key):
    key_consts, (o_ref, buf, tok, sem) = rest[:-4], rest[-4:]
    b = pl.program_id(0)
    slot = b % 2
    page_len = cache_hbm.shape[2]
    n_pos = n_pages * page_len

    def page_copy(page, i, sl):
        dst = pl.ds(pl.multiple_of(i * page_len, page_len), page_len)
        return pltpu.make_async_copy(cache_hbm.at[page], buf.at[sl, :, dst], sem.at[sl])

    def gather(seq, sl):
        def issue(i, carry):
            page_copy(pt_ref[seq * n_pages + i], i, sl).start()
            return carry
        lax.fori_loop(0, n_pages, issue, 0)

    @pl.when(b == 0)
    def _():
        gather(0, 0)

    @pl.when(b + 1 < n_batch)
    def _():
        gather(b + 1, 1 - slot)

    def wait(i, carry):
        page_copy(0, i, slot).wait()
        return carry
    lax.fori_loop(0, n_pages, wait, 0)

    blk = min(n_pos, 8 * page_len)

    def to_rows(j, carry):
        o = pl.multiple_of(j * blk, blk)
        tok[pl.ds(o, blk), :] = buf[slot, :, pl.ds(o, blk)].T
        return carry
    lax.fori_loop(0, n_pos // blk, to_rows, 0)

    n_chunk = n_pos // CMP_STRIDE
    half = N_KV_HEADS * CMP_HID
    a0 = jnp.zeros((n_chunk, half), F32)
    a1 = jnp.zeros((n_chunk, half), F32)
    for s in range(CMP_STRIDE):
        xs = tok[pl.ds(s, n_chunk, stride=CMP_STRIDE), :]
        s1 = CMP_STRIDE + s
        a0 = a0 + _dot((xs + pe_ref[s:s + 1, :]).astype(BF16), w1_ref[s])
        a1 = a1 + _dot((xs + pe_ref[s1:s1 + 1, :]).astype(BF16), w1_ref[s1])
    out = _compress_tail(a0, a1, w2_ref)
    o_ref[0] = _cmp_key_post(out, *key_consts) if is_key else out


def _cmp_sample(page_table, cache_t, p, is_key):
    b, n_pages = page_table.shape
    page_len = cache_t.shape[2]
    n_pos = n_pages * page_len
    n_chunk = n_pos // CMP_STRIDE
    name = 'k' if is_key else 'v'
    per, wf, w2f = _compress_weights(p[f'cmp_{name}_pos'], p[f'cmp_{name}_w1'], p[f'cmp_{name}_w2'])
    pe_s, w_s = _per_position_weights(per, wf)
    args = [pe_s, w_s, w2f] + (_cmp_key_consts(p, n_chunk) if is_key else [])
    const = lambda a: pl.BlockSpec(a.shape, lambda i, pt: (0,) * a.ndim)
    grid_spec = pltpu.PrefetchScalarGridSpec(
        num_scalar_prefetch=1, grid=(b,),
        in_specs=[pl.BlockSpec(memory_space=pl.ANY)] + [const(a) for a in args],
        out_specs=pl.BlockSpec((1, n_chunk, KV_W), lambda i, pt: (i, 0, 0)),
        scratch_shapes=[pltpu.VMEM((2, KV_W, n_pos), F32), pltpu.VMEM((n_pos, KV_W), F32),
                        pltpu.SemaphoreType.DMA((2,))])
    return pl.pallas_call(
        functools.partial(_cmp_sample_body, n_pages=n_pages, n_batch=b, is_key=is_key),
        out_shape=jax.ShapeDtypeStruct((b, n_chunk, KV_W), F32), grid_spec=grid_spec,
        compiler_params=_params(1), name=f"cmp_sample_{name}")(page_table.reshape(-1), cache_t, *args)


def _cmp_to_sel(n_chunk, n_cmp, n_sel, width):
    cs = np.arange(n_chunk)[:, None] * CMP_STRIDE
    ss = np.arange(width)[None, :] * SEL_BLK
    ov = np.clip(np.minimum(cs + CMP_LEN, ss + SEL_BLK) - np.maximum(cs, ss), 0, None) / CMP_LEN
    ov = ov * (np.arange(n_chunk)[:, None] < n_cmp) * (np.arange(width)[None, :] < n_sel)
    return jnp.asarray(ov, dtype=BF16)


def _attn_prompt_body(q_ref, gt_ref, kc_ref, vc_ref, ks_ref, vs_ref, kw_ref, vw_ref, ov_ref, o_ref,
                      qg_sc, sel_sc, oc_sc, m_sc, l_sc, al_sc, acc_sc, s_sc, p_sc, bias_sc,
                      *, t, tq, kc_len, n_cmp, n_sel):
    off = pl.program_id(1) * tq
    rows = GRP * tq
    lane = lax.broadcasted_iota(jnp.int32, (tq, LANES), 1)
    qpos = off + lax.broadcasted_iota(jnp.int32, (tq, 1), 0)
    q = q_ref[0] * SCALE
    for h in range(N_HEADS):
        g, r = divmod(h, GRP)
        qg_sc[g, r * tq:(r + 1) * tq, :] = _head_to_group_lanes(q, h, lane).astype(BF16)

    def softmax_tiles(g, n_keys):
        def scores():
            s = s_sc[:, :n_keys].reshape(GRP, tq, n_keys)
            return s * LOG2E + bias_sc[:, :n_keys][None]

        m_old = m_sc[g].reshape(GRP, tq, 1)
        m_new = jnp.maximum(m_old, jnp.max(scores(), axis=-1, keepdims=True))
        m_sc[g] = m_new.reshape(rows, 1)
        p = jnp.exp2(scores() - m_new)
        alpha = jnp.exp2(m_old - m_new).reshape(rows, 1)
        l_sc[g] = alpha * l_sc[g] + jnp.sum(p, axis=-1, keepdims=True).reshape(rows, 1)
        al_sc[...] = alpha
        p_sc[:, :n_keys] = p.reshape(rows, n_keys).astype(BF16)

    def reset_softmax():
        m_sc[...] = jnp.full(m_sc.shape, NEG_INF, F32)
        l_sc[...] = jnp.zeros(l_sc.shape, F32)

    n_chunk = kc_ref.shape[1]
    kcm = kc_ref[0].astype(BF16)
    vcm = vc_ref[0].astype(BF16)
    ncol = lax.broadcasted_iota(jnp.int32, (tq, n_chunk), 1)
    mask_c = jnp.logical_and(ncol * CMP_STRIDE + (CMP_LEN - 1) <= qpos, ncol < n_cmp)[None]
    psum = []
    for g in range(N_KV_HEADS):
        sc = _dot_t(qg_sc[g], kcm).reshape(GRP, tq, n_chunk)
        pc = _masked_softmax(sc, mask_c)
        oc_sc[g] = _dot(pc.reshape(rows, n_chunk).astype(BF16), vcm)
        psum.append(jnp.sum(pc, axis=0))

    cur = qpos // SEL_BLK
    valid = lane <= cur
    forced = jnp.logical_or(lane == 0, jnp.logical_or(lane == cur, lane == cur - 1))
    bonus = jnp.where(forced, FORCE_BONUS, 0.0)
    k_eff = min(TOPK, n_sel)
    for g in range(N_KV_HEADS):
        imp = _split_dot(psum[g], ov_ref[...])
        score = jnp.where(valid, imp + bonus, NEG_INF)
        rank = jnp.zeros((tq, LANES), F32)
        for j in range(n_sel):
            col = score[:, j:j + 1]
            tie = jnp.where(lane > j, 1.0, 0.0)
            rank = rank + jnp.where(col > score, 1.0, jnp.where(col == score, tie, 0.0))
        sel_sc[g] = jnp.where(rank < k_eff, 1.0, 0.0).astype(BF16)

    reset_softmax()
    acc_sc[...] = jnp.zeros(acc_sc.shape, F32)
    blk_row = lax.broadcasted_iota(jnp.int32, (LANES, kc_len), 0)
    key_col = lax.broadcasted_iota(jnp.int32, (LANES, kc_len), 1)
    for ci in range(t // kc_len):
        @pl.when(ci * kc_len < off + tq)
        def _(ci=ci):
            base = ci * kc_len
            expand = jnp.where(blk_row == (base + key_col) // SEL_BLK, 1.0, 0.0).astype(BF16)
            kpos = base + lax.broadcasted_iota(jnp.int32, (tq, kc_len), 1)
            causal = kpos <= qpos
            kt = ks_ref[0, :, base:base + kc_len].astype(BF16)
            vt = vs_ref[0, :, base:base + kc_len].astype(BF16)
            for g in range(N_KV_HEADS):
                mask = jnp.logical_and(_dot(sel_sc[g], expand) > 0.5, causal)
                bias_sc[:, :kc_len] = jnp.where(mask, 0.0, -jnp.inf)
                s_sc[:, :kc_len] = _dot(qg_sc[g], kt)
                softmax_tiles(g, kc_len)
                acc_sc[g] = al_sc[...] * acc_sc[g] + _dot_t(p_sc[:, :kc_len], vt)
    o_sel = [acc_sc[g] / jnp.maximum(l_sc[g], 1e-30) for g in range(N_KV_HEADS)]

    wl = WINDOW + tq
    ws = pl.multiple_of(jnp.maximum(off - WINDOW, 0), LANES)
    kwt = kw_ref[0, :, pl.ds(ws, wl)].astype(BF16)
    vwt = vw_ref[0, :, pl.ds(ws, wl)].astype(BF16)
    dist = qpos - (ws + lax.broadcasted_iota(jnp.int32, (tq, wl), 1))
    bias_sc[:, :wl] = jnp.where(jnp.logical_and(dist >= 0, dist < WINDOW), 0.0, -jnp.inf)
    reset_softmax()
    gt = gt_ref[0]
    outs = []
    for g in range(N_KV_HEADS):
        s_sc[:, :wl] = _dot(qg_sc[g], kwt)
        softmax_tiles(g, wl)
        o_w = _dot_t(p_sc[:, :wl], vwt) / jnp.maximum(l_sc[g], 1e-30)
        o_s = o_sel[g]
        o_c = oc_sc[g]
        for r in range(GRP):
            h = g * GRP + r
            sl = slice(r * tq, (r + 1) * tq)
            outs.append(gt[:, 3 * h:3 * h + 1] * o_c[sl] + gt[:, 3 * h + 1:3 * h + 2] * o_s[sl]
                        + gt[:, 3 * h + 2:3 * h + 3] * o_w[sl])
    o_ref[0] = jnp.concatenate(
        [_pair_tile(outs[2 * p], outs[2 * p + 1], 2 * p, lane) for p in range(N_HEADS // 2)], axis=-1)


def _attn_prompt(q, gt, kcmp, vcmp, ks_t, vs_t, kw_t, vw_t, tq=128, kc_len=512):
    b, t, _ = q.shape
    n_chunk = kcmp.shape[1]
    n_cmp = n_chunk - N_SUB + 1
    n_sel = -(-t // SEL_BLK)
    assert t % kc_len == 0 and t >= WINDOW + tq and n_sel <= LANES and n_chunk % LANES == 0
    assert tq % LANES == 0
    ov = _cmp_to_sel(n_chunk, n_cmp, n_sel, LANES)
    qblk = lambda i, j: (i, j, 0)
    full = lambda i, j: (i, 0, 0)
    kv = pl.BlockSpec((1, KV_W, t), full)
    cm = pl.BlockSpec((1, n_chunk, KV_W), full)
    rows = GRP * tq
    n_keys = max(kc_len, WINDOW + tq)
    assert tq % ROW_TILE == 0
    body = functools.partial(_attn_prompt_body, t=t, tq=tq, kc_len=kc_len, n_cmp=n_cmp, n_sel=n_sel)
    return pl.pallas_call(
        body, out_shape=jax.ShapeDtypeStruct((b, t, D_ATT), F32), grid=(b, t // tq),
        in_specs=[pl.BlockSpec((1, tq, D_ATT), qblk), pl.BlockSpec((1, tq, LANES), qblk), cm, cm,
                  kv, kv, kv, kv, pl.BlockSpec((n_chunk, LANES), lambda i, j: (0, 0))],
        out_specs=pl.BlockSpec((1, tq, D_ATT), qblk),
        scratch_shapes=[pltpu.VMEM((N_KV_HEADS, rows, LANES), BF16), pltpu.VMEM((N_KV_HEADS, tq, LANES), BF16),
                        pltpu.VMEM((N_KV_HEADS, rows, LANES), F32), pltpu.VMEM((N_KV_HEADS, rows, 1), F32),
                        pltpu.VMEM((N_KV_HEADS, rows, 1), F32), pltpu.VMEM((rows, 1), F32),
                        pltpu.VMEM((N_KV_HEADS, rows, LANES), F32), pltpu.VMEM((rows, n_keys), F32),
                        pltpu.VMEM((rows, n_keys), BF16), pltpu.VMEM((tq, n_keys), F32)],
        compiler_params=_params(2), name="attn_prompt")(q, gt, kcmp, vcmp, ks_t, vs_t, kw_t, vw_t, ov)


def _sample_heads(q_row):
    lane = lax.broadcasted_iota(jnp.int32, (1, LANES), 1)
    return jnp.concatenate([_head_to_group_lanes(q_row, h, lane) for h in range(N_HEADS)], axis=0)


def _attn_s1_body(q_ref, kc_ref, vc_ref, ov_ref, oc_ref, idx_ref, *, n_cmp, n_sel, q_pos):
    qh = _sample_heads(q_ref[0])
    n_chunk = kc_ref.shape[1]
    ncol = lax.broadcasted_iota(jnp.int32, (N_HEADS, n_chunk), 1)
    mask = jnp.logical_and(ncol * CMP_STRIDE + (CMP_LEN - 1) <= q_pos, ncol < n_cmp)
    pc = _masked_softmax(_dot_t(qh.astype(BF16), kc_ref[0].astype(BF16)) * SCALE, mask)
    oc_ref[0] = _dot(pc.astype(BF16), vc_ref[0].astype(BF16))
    rows = [jnp.sum(pc[g * GRP:(g + 1) * GRP, :], axis=0, keepdims=True) for g in range(N_KV_HEADS)]
    psum = jnp.concatenate(rows + [jnp.zeros((N_HEADS - N_KV_HEADS, n_chunk), F32)], axis=0)
    imp = _split_dot(psum, ov_ref[...])
    width = imp.shape[1]
    jl = lax.broadcasted_iota(jnp.int32, (N_HEADS, width), 1)
    cur = q_pos // SEL_BLK
    forced = jnp.logical_or(jl == 0, jnp.logical_or(jl == cur, jl == cur - 1))
    score = jnp.where(jl <= cur, imp + jnp.where(forced, FORCE_BONUS, 0.0), NEG_INF)
    ii = lax.broadcasted_iota(jnp.int32, (width, width), 0)
    jj = lax.broadcasted_iota(jnp.int32, (width, width), 1)
    slot = lax.broadcasted_iota(jnp.int32, (width, LANES), 1).astype(F32)
    blk = lax.broadcasted_iota(jnp.int32, (width, LANES), 0).astype(F32)
    tie = jnp.where(jj < ii, 1.0, 0.0)
    out_rows = []
    for g in range(N_KV_HEADS):
        row = jnp.broadcast_to(score[g:g + 1, :], (width, width))
        col = jnp.sum(jnp.where(ii == jj, row, 0.0), axis=1, keepdims=True)
        beats = jnp.where(row > col, 1.0, jnp.where(row == col, tie, 0.0))
        rank = jnp.sum(beats, axis=1, keepdims=True)
        out_rows.append(jnp.sum(jnp.where(rank == slot, blk, 0.0), axis=0, keepdims=True))
    out_rows.append(jnp.zeros((N_HEADS - N_KV_HEADS, LANES), F32))
    idx_ref[0] = jnp.concatenate(out_rows, axis=0).astype(jnp.int32)


def _attn_s1(q, kcmp, vcmp, q_pos, n_sel):
    b = q.shape[0]
    n_chunk = kcmp.shape[1]
    n_cmp = n_chunk - N_SUB + 1
    width = -(-n_sel // LANES) * LANES
    ov = _cmp_to_sel(n_chunk, n_cmp, n_sel, width)
    blk3 = lambda i: (i, 0, 0)
    cm = pl.BlockSpec((1, n_chunk, KV_W), blk3)
    out = pl.BlockSpec((1, N_HEADS, LANES), blk3)
    return pl.pallas_call(
        functools.partial(_attn_s1_body, n_cmp=n_cmp, n_sel=n_sel, q_pos=q_pos),
        out_shape=[jax.ShapeDtypeStruct((b, N_HEADS, LANES), F32),
                   jax.ShapeDtypeStruct((b, N_HEADS, LANES), jnp.int32)],
        grid=(b,),
        in_specs=[pl.BlockSpec((1, 1, D_ATT), blk3), cm, cm, pl.BlockSpec(ov.shape, lambda i: (0, 0))],
        out_specs=[out, out], compiler_params=_params(1), name="attn_sample_select")(
            q.reshape(b, 1, D_ATT), kcmp, vcmp, ov)


def _extra_key_attention(qh, k_t, v_t, mask, k_new, v_new):
    qf = qh.astype(F32)
    s = jnp.where(mask, _dot(qh, k_t.astype(BF16)) * SCALE, NEG_INF)
    s_new = jnp.sum(qf * k_new, axis=-1, keepdims=True) * SCALE
    m = jnp.maximum(jnp.max(s, axis=-1, keepdims=True), s_new)
    e = jnp.where(mask, jnp.exp(s - m), 0.0)
    e_new = jnp.exp(s_new - m)
    denom = jnp.maximum(jnp.sum(e, axis=-1, keepdims=True) + e_new, 1e-30)
    return (_dot_t(e.astype(BF16), v_t.astype(BF16)) + e_new * v_new) / denom


def _attn_s2_body(idx_ref, pt_ref, q_ref, gt_ref, oc_ref, ksn_ref, vsn_ref, kwn_ref, vwn_ref,
                  kws_ref, vws_ref, kcache, vcache, o_ref, kbuf, vbuf, sem, *, n_pages, n_cache_blk, q_pos):
    b = pl.program_id(0)
    page_len = kcache.shape[2]
    per_page = page_len // SEL_BLK

    def copies(g, k, blk):
        blk = jnp.minimum(blk, n_cache_blk - 1)
        page = pt_ref[b * n_pages + blk // per_page]
        dst = pl.ds(k * page_len, page_len)
        return (pltpu.make_async_copy(kcache.at[page], kbuf.at[g, :, dst], sem.at[0]),
                pltpu.make_async_copy(vcache.at[page], vbuf.at[g, :, dst], sem.at[1]))

    blks = [[idx_ref[(b * N_KV_HEADS + g) * TOPK + k] for k in range(TOPK)] for g in range(N_KV_HEADS)]
    for g in range(N_KV_HEADS):
        for k in range(TOPK):
            for cp in copies(g, k, blks[g][k]):
                cp.start()
    for g in range(N_KV_HEADS):
        for k in range(TOPK):
            for cp in copies(g, k, blks[g][k]):
                cp.wait()

    qh = _sample_heads(q_ref[0]).astype(BF16)
    n_keys = TOPK * page_len
    row_grp = lax.broadcasted_iota(jnp.int32, (N_HEADS, n_keys), 0) // GRP
    key = lax.broadcasted_iota(jnp.int32, (N_HEADS, n_keys), 1)
    slot = key // page_len
    blk_in_page = (key % page_len) // SEL_BLK
    keep = jnp.zeros((N_HEADS, n_keys), F32)
    for g in range(N_KV_HEADS):
        for k in range(TOPK):
            blk = blks[g][k]
            in_cache = jnp.where(blk < n_cache_blk, 1.0, 0.0)
            hit = jnp.where(blk_in_page == blk % per_page, in_cache, 0.0)
            keep = jnp.where(jnp.logical_and(row_grp == g, slot == k), hit, keep)
    mask_s = keep > 0.5
    o_sel = [_extra_key_attention(qh, kbuf[g], vbuf[g], mask_s, ksn_ref[0], vsn_ref[0])
             for g in range(N_KV_HEADS)]
    out_grp = lax.broadcasted_iota(jnp.int32, (N_HEADS, LANES), 0) // GRP
    o_s = jnp.where(out_grp == 0, o_sel[0], o_sel[1])

    n_win = kws_ref.shape[2]
    kpos = (q_pos - n_win) + lax.broadcasted_iota(jnp.int32, (N_HEADS, n_win), 1)
    dist = q_pos - kpos
    mask_w = jnp.logical_and(dist >= 0, dist < WINDOW)
    o_w = _extra_key_attention(qh, kws_ref[0], vws_ref[0], mask_w, kwn_ref[0], vwn_ref[0])

    lane8 = lax.broadcasted_iota(jnp.int32, (N_HEADS, LANES), 1)
    head8 = lax.broadcasted_iota(jnp.int32, (N_HEADS, LANES), 0)
    gt = jnp.broadcast_to(gt_ref[0], (N_HEADS, LANES))
    gate = [jnp.sum(jnp.where(lane8 == 3 * head8 + j, gt, 0.0), axis=-1, keepdims=True) for j in range(3)]
    o = gate[0] * oc_ref[0] + gate[1] * o_s + gate[2] * o_w
    lane = lax.broadcasted_iota(jnp.int32, (1, LANES), 1)
    o_ref[0] = jnp.concatenate(
        [_pair_tile(o[2 * p:2 * p + 1, :], o[2 * p + 1:2 * p + 2, :], 2 * p, lane)
         for p in range(N_HEADS // 2)], axis=-1)


def _attn_s2(idx, page_table, q, gt, oc, ks_new, vs_new, kw_new, vw_new, kw_state_t, vw_state_t,
             cache_k_t, cache_v_t, q_pos):
    b, n_pages = page_table.shape
    page_len = cache_k_t.shape[2]
    n_win = kw_state_t.shape[2]
    assert n_win <= WINDOW
    blk3 = lambda i, *_: (i, 0, 0)
    row = lambda w: pl.BlockSpec((1, 1, w), blk3)
    grid_spec = pltpu.PrefetchScalarGridSpec(
        num_scalar_prefetch=2, grid=(b,),
        in_specs=[row(D_ATT), row(LANES), pl.BlockSpec((1, N_HEADS, LANES), blk3),
                  row(KV_W), row(KV_W), row(KV_W), row(KV_W),
                  pl.BlockSpec((1, KV_W, n_win), blk3), pl.BlockSpec((1, KV_W, n_win), blk3),
                  pl.BlockSpec(memory_space=pl.ANY), pl.BlockSpec(memory_space=pl.ANY)],
        out_specs=row(D_ATT),
        scratch_shapes=[pltpu.VMEM((N_KV_HEADS, KV_W, TOPK * page_len), F32),
                        pltpu.VMEM((N_KV_HEADS, KV_W, TOPK * page_len), F32),
                        pltpu.SemaphoreType.DMA((2,))])
    r3 = lambda a: a.reshape(b, 1, a.shape[-1])
    body = functools.partial(_attn_s2_body, n_pages=n_pages, n_cache_blk=n_pages * (page_len // SEL_BLK),
                             q_pos=q_pos)
    return pl.pallas_call(
        body, out_shape=jax.ShapeDtypeStruct((b, 1, D_ATT), F32), grid_spec=grid_spec,
        compiler_params=_params(1), name="attn_sample")(
            idx.reshape(-1), page_table.reshape(-1), r3(q), r3(gt), oc, r3(ks_new), r3(vs_new),
            r3(kw_new), r3(vw_new), kw_state_t, vw_state_t, cache_k_t, cache_v_t).reshape(b, D_ATT)


def _out_body(x_ref, c_ref, o_ref, og_ref, wc_ref, wa_ref, y_ref):
    on = _rms(o_ref[...], og_ref[...])
    y_ref[...] = (x_ref[...] + _dot(c_ref[...].astype(BF16), wc_ref[...])
                  + _dot(on.astype(BF16), wa_ref[...]))


def _out_proj(x2, cn, o, p, tm):
    rows, d = x2.shape
    w = p['w_out'].astype(BF16)
    row = lambda i: (i, 0)
    const = lambda i: (0, 0)
    return pl.pallas_call(
        _out_body, out_shape=jax.ShapeDtypeStruct((rows, d), F32), grid=(rows // tm,),
        in_specs=[pl.BlockSpec((tm, d), row), pl.BlockSpec((tm, C_CONV), row), pl.BlockSpec((tm, D_ATT), row),
                  pl.BlockSpec((1, D_ATT), const), pl.BlockSpec((C_CONV, d), const),
                  pl.BlockSpec((D_ATT, d), const)],
        out_specs=pl.BlockSpec((tm, d), row), compiler_params=_params(1), name="out_proj")(
            x2, cn, o, p['out_norm_attn'].reshape(1, D_ATT), w[:C_CONV], w[C_CONV:])


def _prompt_layer(x, p, tm=512):
    b, t, d = x.shape
    x2 = _ffn(x.reshape(b * t, d), p['ffn1_norm'], p['ffn1_w_in'], p['ffn1_w_out'], None, tm)
    u, q, gt, kc, vc, kc_t, vc_t, ks_t, vs_t, kw_t, vw_t = _proj(
        x2, p, _rope_tables(jnp.arange(t)), tm, seq_len=t)
    r3 = lambda a: a.reshape(b, t, a.shape[-1])
    u3 = r3(u)
    cn = _conv_prompt(u3, p)
    kcmp, vcmp = _cmp_prompt(r3(kc), r3(vc), p)
    o = _attn_prompt(r3(q), r3(gt), kcmp, vcmp, ks_t, vs_t, kw_t, vw_t)
    x2 = _out_proj(x2, cn.reshape(b * t, C_CONV), o.reshape(b * t, D_ATT), p, tm)
    y = _ffn(x2, p['ffn2_norm'], p['ffn2_w_in'], p['ffn2_w_out'], p['final_norm'], tm)
    r4 = lambda a: a.reshape(b, N_KV_HEADS, HEAD_DIM, a.shape[-1]).transpose(0, 3, 1, 2)
    keep = min(WINDOW, t)
    state = (r4(kc_t), r4(vc_t), r4(ks_t), r4(vs_t), r4(kw_t[:, :, t - keep:]), r4(vw_t[:, :, t - keep:]),
             u3[:, t - (CONV_W - 1):])
    return y.reshape(b, t, d), state


def _sample_layer(x, p, cache_k_cmp, cache_v_cmp, cache_k_sel, cache_v_sel, kw_state, vw_state,
                  conv_state, page_table):
    b, t, d = x.shape
    assert t == 1
    past_len = page_table.shape[1] * cache_k_cmp.shape[1]
    x2 = _ffn(x.reshape(b, d), p['ffn1_norm'], p['ffn1_w_in'], p['ffn1_w_out'], None, b)
    tabs = _rope_tables(jnp.full((b,), past_len, jnp.int32))
    u, q, gt, kc, vc, ks, vs, kw, vw = _proj(x2, p, tabs, b)
    cn = _conv_sample(conv_state, u, p)
    keys_on_lanes = lambda a: a.transpose(0, 2, 3, 1).reshape(a.shape[0], KV_W, a.shape[1])
    kcmp = _cmp_sample(page_table, keys_on_lanes(cache_k_cmp), p, True)
    vcmp = _cmp_sample(page_table, keys_on_lanes(cache_v_cmp), p, False)
    n_sel = -(-(past_len + 1) // SEL_BLK)
    oc, idx = _attn_s1(q, kcmp, vcmp, past_len, n_sel)
    n_win = kw_state.shape[1]
    o = _attn_s2(idx[:, :N_KV_HEADS, :TOPK], page_table, q, gt, oc, ks, vs, kw, vw,
                 keys_on_lanes(kw_state), keys_on_lanes(vw_state),
                 keys_on_lanes(cache_k_sel), keys_on_lanes(cache_v_sel), past_len)
    x2 = _out_proj(x2, cn, o, p, b)
    y = _ffn(x2, p['ffn2_norm'], p['ffn2_w_in'], p['ffn2_w_out'], p['final_norm'], b)
    r4 = lambda a: a.reshape(b, 1, N_KV_HEADS, HEAD_DIM)
    keep = min(WINDOW, n_win + 1)
    new_kw = jnp.concatenate([kw_state, r4(kw)], axis=1)[:, n_win + 1 - keep:]
    new_vw = jnp.concatenate([vw_state, r4(vw)], axis=1)[:, n_win + 1 - keep:]
    new_conv = jnp.concatenate([conv_state, u[:, None, :]], axis=1)[:, 1:]
    return y.reshape(b, 1, d), (r4(kc), r4(vc), r4(ks), r4(vs), new_kw, new_vw, new_conv)


_PARAM_NAMES = ('ffn1_norm', 'ffn1_w_in', 'ffn1_w_out', 'mix_norm', 'w_in', 'conv_w', 'conv_b', 'conv_ln_g',
                'conv_ln_b', 'q_norm', 'k_cmp_norm', 'k_sel_norm', 'k_win_norm', 'cmp_k_pos', 'cmp_k_w1',
                'cmp_k_w2', 'cmp_v_pos', 'cmp_v_w1', 'cmp_v_w2', 'out_norm_conv', 'out_norm_attn', 'w_out',
                'ffn2_norm', 'ffn2_w_in', 'ffn2_w_out', 'final_norm')


def kernel(x_prompt, x_sample, cache_k_cmp, cache_v_cmp, cache_k_sel, cache_v_sel, state_k_win, state_v_win,
           state_conv, page_table, ffn1_norm, ffn1_w_in, ffn1_w_out, mix_norm, w_in, conv_w, conv_b, conv_ln_g,
           conv_ln_b, q_norm, k_cmp_norm, k_sel_norm, k_win_norm, cmp_k_pos, cmp_k_w1, cmp_k_w2, cmp_v_pos,
           cmp_v_w1, cmp_v_w2, out_norm_conv, out_norm_attn, w_out, ffn2_norm, ffn2_w_in, ffn2_w_out, final_norm):
    stacked = dict(zip(_PARAM_NAMES, (
        ffn1_norm, ffn1_w_in, ffn1_w_out, mix_norm, w_in, conv_w, conv_b, conv_ln_g, conv_ln_b, q_norm,
        k_cmp_norm, k_sel_norm, k_win_norm, cmp_k_pos, cmp_k_w1, cmp_k_w2, cmp_v_pos, cmp_v_w1, cmp_v_w2,
        out_norm_conv, out_norm_attn, w_out, ffn2_norm, ffn2_w_in, ffn2_w_out, final_norm)))
    depth = ffn1_norm.shape[0]
    yp, ys = x_prompt, x_sample
    new_p, new_s = [], []
    for l in range(depth):
        p = {k: v[l] for k, v in stacked.items()}
        yp, st_p = _prompt_layer(yp, p)
        new_p.append(st_p)
        ys, st_s = _sample_layer(ys, p, cache_k_cmp[l], cache_v_cmp[l], cache_k_sel[l], cache_v_sel[l],
                                 state_k_win[l], state_v_win[l], state_conv[l], page_table)
        new_s.append(st_s)
    outs_p = [jnp.stack(tup) for tup in zip(*new_p)]
    outs_s = [jnp.stack(tup) for tup in zip(*new_s)]
    return (yp, ys, *outs_p, *outs_s)
```

```python
import functools

import numpy as np
import jax
import jax.numpy as jnp
from jax import lax
from jax.experimental import pallas as pl
from jax.experimental.pallas import tpu as pltpu

F32 = jnp.float32
BF16 = jnp.bfloat16

HEAD_DIM = 64
N_HEADS = 8
N_KV_HEADS = 2
GRP = N_HEADS // N_KV_HEADS
KV_W = N_KV_HEADS * HEAD_DIM
C_CONV = 512
D_ATT = 512
CONV_W = 31
CMP_LEN = 32
CMP_STRIDE = 16
N_SUB = CMP_LEN // CMP_STRIDE
CMP_HID = 2 * HEAD_DIM
SEL_BLK = 64
TOPK = 16
WINDOW = 512
ROT_DIM = HEAD_DIM // 4
ROT_HALF = ROT_DIM // 2
ROPE_THETA = 500000.0
NEG_INF = -1e30
FORCE_BONUS = 1e4
EPS = 1e-6
SCALE = HEAD_DIM ** -0.5
LOG2E = 1.4426950408889634
LANES = 128
CHUNK_W = CMP_STRIDE * KV_W
VMEM_LIMIT = 56 * 1024 * 1024


def _params(n_grid_dims):
    return pltpu.CompilerParams(
        dimension_semantics=("arbitrary",) * n_grid_dims, vmem_limit_bytes=VMEM_LIMIT)


def _dot(a, b):
    return jnp.dot(a, b, preferred_element_type=F32)


def _dot_t(a, b):
    return lax.dot_general(a, b, (((1,), (1,)), ((), ())), preferred_element_type=F32)


def _split_dot(x, m_bf16):
    hi = x.astype(BF16)
    lo = (x - hi.astype(F32)).astype(BF16)
    return _dot(hi, m_bf16) + _dot(lo, m_bf16)


def _sigmoid(x):
    return 1.0 / (1.0 + jnp.exp(-x))


def _rms(x, g):
    return x * lax.rsqrt(jnp.mean(x * x, axis=-1, keepdims=True) + EPS) * g


def _group_rms(x, g_tiled, ones_blockdiag):
    ss = _split_dot(x * x, ones_blockdiag)
    return x * lax.rsqrt(ss * (1.0 / HEAD_DIM) + EPS) * g_tiled


def _rope(x, c, sa, sb):
    n = x.shape[-1]
    return x * c + pltpu.roll(x, n - ROT_HALF, 1) * sa + pltpu.roll(x, ROT_HALF, 1) * sb


def _tile_lanes(t, reps):
    return jnp.concatenate([t] * reps, axis=-1) if reps > 1 else t


def _masked_softmax(s, mask):
    s = jnp.where(mask, s, NEG_INF)
    m = jnp.max(s, axis=-1, keepdims=True)
    e = jnp.where(mask, jnp.exp(s - m), 0.0)
    return e / jnp.maximum(jnp.sum(e, axis=-1, keepdims=True), 1e-30)


def _head_to_group_lanes(q, h, lane):
    g, p = h // GRP, h // 2
    qp = q[:, LANES * p:LANES * (p + 1)]
    src = qp if (h % 2) == g else pltpu.roll(qp, HEAD_DIM, 1)
    keep = (lane < HEAD_DIM) if g == 0 else (lane >= HEAD_DIM)
    return jnp.where(keep, src, 0.0)


def _pair_tile(o_even, o_odd, h_even, lane):
    g = h_even // GRP
    a = o_even if g == 0 else pltpu.roll(o_even, HEAD_DIM, 1)
    b = pltpu.roll(o_odd, HEAD_DIM, 1) if g == 0 else o_odd
    return jnp.where(lane < HEAD_DIM, a, b)


def _ffn_body(*refs, n_chunk, tf, ff, final):
    if final:
        x_ref, g_ref, wi_ref, wo_ref, fg_ref, o_ref = refs
    else:
        x_ref, g_ref, wi_ref, wo_ref, o_ref = refs
    x = x_ref[...]
    xn = _rms(x, g_ref[...]).astype(BF16)
    acc = jnp.zeros(x.shape, F32)
    for c in range(n_chunk):
        a = _dot(xn, wi_ref[:, c * tf:(c + 1) * tf])
        b = _dot(xn, wi_ref[:, ff + c * tf:ff + (c + 1) * tf])
        h = (a * _sigmoid(a)) * b
        acc = acc + _dot(h.astype(BF16), wo_ref[c * tf:(c + 1) * tf, :])
    y = x + 0.5 * acc
    if final:
        y = _rms(y, fg_ref[...])
    o_ref[...] = y


def _ffn(x2, norm_g, w_in, w_out, final_g, tm):
    rows, d = x2.shape
    ff = w_out.shape[0]
    tf = 256
    assert rows % tm == 0 and ff % tf == 0 and w_in.shape == (d, 2 * ff)
    const = lambda i: (0, 0)
    single = pl.Buffered(1)
    in_specs = [
        pl.BlockSpec((tm, d), lambda i: (i, 0)),
        pl.BlockSpec((1, d), const),
        pl.BlockSpec((d, 2 * ff), const, pipeline_mode=single),
        pl.BlockSpec((ff, d), const, pipeline_mode=single),
    ]
    args = [x2, norm_g.reshape(1, d), w_in.astype(BF16), w_out.astype(BF16)]
    if final_g is not None:
        in_specs.append(pl.BlockSpec((1, d), const))
        args.append(final_g.reshape(1, d))
    body = functools.partial(_ffn_body, n_chunk=ff // tf, tf=tf, ff=ff, final=final_g is not None)
    return pl.pallas_call(
        body, out_shape=jax.ShapeDtypeStruct((rows, d), F32), grid=(rows // tm,),
        in_specs=in_specs, out_specs=pl.BlockSpec((tm, d), lambda i: (i, 0)),
        compiler_params=_params(1), name="ffn")(*args)


_SEG_GLU = 2 * C_CONV
_SEG_Q = _SEG_GLU + D_ATT
N_IN = _SEG_Q + 6 * KV_W + 3 * N_HEADS
N_IN_PAD = _SEG_Q + 7 * KV_W


def _proj_body(x_ref, g_ref, w_ref, qg_ref, ksg_ref, kwg_ref, bd4_ref, bd1_ref, c_ref, sa_ref, sb_ref,
               u_ref, q_ref, gt_ref, kc_ref, vc_ref, *kv_refs, transposed):
    xn = _rms(x_ref[...], g_ref[...]).astype(BF16)

    def seg(lo, width):
        return _dot(xn, w_ref[:, lo:lo + width])

    c, sa, sb = c_ref[...], sa_ref[...], sb_ref[...]
    reps = D_ATT // LANES
    glu = seg(0, _SEG_GLU)
    u_ref[...] = glu[:, :C_CONV] * _sigmoid(glu[:, C_CONV:])
    q = _group_rms(seg(_SEG_GLU, D_ATT), qg_ref[...], bd4_ref[...])
    q_ref[...] = _rope(q, _tile_lanes(c, reps), _tile_lanes(sa, reps), _tile_lanes(sb, reps))
    base = _SEG_Q
    kc = seg(base, KV_W)
    vc = seg(base + KV_W, KV_W)
    ks = _rope(_group_rms(seg(base + 2 * KV_W, KV_W), ksg_ref[...], bd1_ref[...]), c, sa, sb)
    vs = seg(base + 3 * KV_W, KV_W)
    kw = _rope(_group_rms(seg(base + 4 * KV_W, KV_W), kwg_ref[...], bd1_ref[...]), c, sa, sb)
    vw = seg(base + 5 * KV_W, KV_W)
    gt_ref[...] = _sigmoid(seg(base + 6 * KV_W, KV_W))
    kc_ref[...] = kc
    vc_ref[...] = vc
    if transposed:
        kv_refs[0][...] = ks
        kv_refs[1][...] = kw
        for ref, val in zip(kv_refs[2:], (kc, vc, ks, vs, kw, vw)):
            ref[0] = val.T
    else:
        for ref, val in zip(kv_refs, (ks, vs, kw, vw)):
            ref[...] = val


def _blockdiag_ones(width):
    idx = np.arange(width) // HEAD_DIM
    return jnp.asarray(idx[:, None] == idx[None, :], dtype=BF16)


def _rope_tables(pos):
    inv = ROPE_THETA ** (-(jnp.arange(ROT_HALF, dtype=F32) * 2.0 / ROT_DIM))
    ang = pos.astype(F32)[:, None] * inv[None, :]
    cos, sin = jnp.cos(ang), jnp.sin(ang)
    n = pos.shape[0]
    ones = jnp.ones((n, HEAD_DIM - ROT_DIM), F32)
    zeros = jnp.zeros((n, HEAD_DIM - ROT_DIM), F32)
    z8 = jnp.zeros((n, ROT_HALF), F32)
    c = jnp.concatenate([cos, cos, ones], axis=1)
    sa = jnp.concatenate([-sin, z8, zeros], axis=1)
    sb = jnp.concatenate([z8, sin, zeros], axis=1)
    return tuple(jnp.tile(t, (1, N_KV_HEADS)) for t in (c, sa, sb))


def _proj(x2, p, tabs, tm, seq_len=None):
    rows, d = x2.shape
    n_tab = tabs[0].shape[0] // tm
    w = jnp.pad(p['w_in'], ((0, 0), (0, N_IN_PAD - N_IN))).astype(BF16)
    const = lambda i: (0, 0)
    row = lambda i: (i, 0)
    tab = lambda i: (i % n_tab, 0)
    in_specs = [
        pl.BlockSpec((tm, d), row), pl.BlockSpec((1, d), const),
        pl.BlockSpec((d, N_IN_PAD), const, pipeline_mode=pl.Buffered(1)),
        pl.BlockSpec((1, D_ATT), const), pl.BlockSpec((1, KV_W), const), pl.BlockSpec((1, KV_W), const),
        pl.BlockSpec((D_ATT, D_ATT), const), pl.BlockSpec((KV_W, KV_W), const),
        pl.BlockSpec((tm, LANES), tab), pl.BlockSpec((tm, LANES), tab), pl.BlockSpec((tm, LANES), tab),
    ]
    widths = [C_CONV, D_ATT, LANES, KV_W, KV_W] + [KV_W] * (4 if seq_len is None else 2)
    out_shape = [jax.ShapeDtypeStruct((rows, wd), F32) for wd in widths]
    out_specs = [pl.BlockSpec((tm, wd), row) for wd in widths]
    if seq_len is not None:
        nt = seq_len // tm
        assert seq_len % tm == 0 and rows % seq_len == 0
        out_shape += [jax.ShapeDtypeStruct((rows // seq_len, KV_W, seq_len), F32)] * 6
        out_specs += [pl.BlockSpec((1, KV_W, tm), lambda i: (i // nt, 0, i % nt))] * 6
    args = [x2, p['mix_norm'].reshape(1, d), w,
            jnp.tile(p['q_norm'], N_HEADS).reshape(1, D_ATT),
            jnp.tile(p['k_sel_norm'], N_KV_HEADS).reshape(1, KV_W),
            jnp.tile(p['k_win_norm'], N_KV_HEADS).reshape(1, KV_W),
            _blockdiag_ones(D_ATT), _blockdiag_ones(KV_W), *tabs]
    return pl.pallas_call(
        functools.partial(_proj_body, transposed=seq_len is not None), out_shape=out_shape,
        grid=(rows // tm,), in_specs=in_specs, out_specs=out_specs,
        compiler_params=_params(1), name="proj")(*args)


def _conv_post(acc, lg, lb, og):
    mu = jnp.mean(acc, axis=-1, keepdims=True)
    xc = acc - mu
    var = jnp.mean(xc * xc, axis=-1, keepdims=True)
    y = xc * lax.rsqrt(var + EPS) * lg + lb
    y = y * _sigmoid(y)
    return _rms(y, og)


def _conv_body(ub_ref, cw_ref, cb_ref, lg_ref, lb_ref, og_ref, o_ref, *, n_tiles, tt):
    halo = 32

    def step(i, carry):
        t0 = pl.multiple_of(i * tt, tt)
        win = ub_ref[0, pl.ds(t0, tt + halo), :]
        acc = jnp.zeros((tt, C_CONV), F32) + cb_ref[...]
        for r in range(8):
            sh = win if r == 0 else pltpu.roll(win, tt + halo - r, 0)
            for a in range(halo // 8):
                w = 8 * a + r
                if w < CONV_W:
                    acc = acc + sh[8 * a:8 * a + tt, :] * cw_ref[w:w + 1, :]
        o_ref[0, pl.ds(t0, tt), :] = _conv_post(acc, lg_ref[...], lb_ref[...], og_ref[...])
        return carry

    lax.fori_loop(0, n_tiles, step, 0)


def _conv_prompt(u, p, tt=256):
    b, t, c = u.shape
    ub = jnp.pad(u, ((0, 0), (CONV_W - 1, 2), (0, 0)))
    cw = jnp.pad(p['conv_w'], ((0, 1), (0, 0)))
    const = lambda i: (0, 0)
    vec = lambda a: a.reshape(1, c)
    return pl.pallas_call(
        functools.partial(_conv_body, n_tiles=t // tt, tt=tt),
        out_shape=jax.ShapeDtypeStruct((b, t, c), F32), grid=(b,),
        in_specs=[pl.BlockSpec((1, t + 32, c), lambda i: (i, 0, 0)), pl.BlockSpec((32, c), const)]
        + [pl.BlockSpec((1, c), const)] * 4,
        out_specs=pl.BlockSpec((1, t, c), lambda i: (i, 0, 0)),
        compiler_params=_params(1), name="conv_prompt")(
            ub, cw, vec(p['conv_b']), vec(p['conv_ln_g']), vec(p['conv_ln_b']), vec(p['out_norm_conv']))


def _conv_sample_body(st_ref, u_ref, cw_ref, cb_ref, lg_ref, lb_ref, og_ref, o_ref):
    acc = u_ref[...] * cw_ref[CONV_W - 1:CONV_W, :] + cb_ref[...]
    for w in range(CONV_W - 1):
        acc = acc + st_ref[w] * cw_ref[w:w + 1, :]
    o_ref[...] = _conv_post(acc, lg_ref[...], lb_ref[...], og_ref[...])


def _conv_sample(state, u, p):
    b = u.shape[0]
    c = C_CONV
    cw = jnp.pad(p['conv_w'], ((0, 1), (0, 0)))
    vec = lambda a: a.reshape(1, c)
    return pl.pallas_call(
        _conv_sample_body, out_shape=jax.ShapeDtypeStruct((b, c), F32),
        compiler_params=pltpu.CompilerParams(vmem_limit_bytes=VMEM_LIMIT), name="conv_sample")(
            state.transpose(1, 0, 2), u, cw, vec(p['conv_b']), vec(p['conv_ln_g']),
            vec(p['conv_ln_b']), vec(p['out_norm_conv']))


def _compress_weights(pe, w1, w2):
    assert N_SUB == 2
    w1r = w1.reshape(N_SUB, CMP_STRIDE, HEAD_DIM, CMP_HID).transpose(1, 2, 0, 3)
    eye = jnp.eye(N_KV_HEADS, dtype=w1.dtype)
    wf = jnp.einsum('sdmh,gk->sgdmkh', w1r, eye).reshape(CHUNK_W, N_SUB * N_KV_HEADS * CMP_HID)
    per = jnp.broadcast_to(pe.reshape(N_SUB, CMP_STRIDE, 1, HEAD_DIM),
                           (N_SUB, CMP_STRIDE, N_KV_HEADS, HEAD_DIM)).reshape(N_SUB, CHUNK_W)
    w2f = jnp.einsum('hd,gk->ghkd', w2, eye).reshape(N_KV_HEADS * CMP_HID, KV_W)
    return per, wf.astype(BF16), w2f.astype(BF16)


def _position_pair_weights(per, wf):
    half = N_KV_HEADS * CMP_HID
    pairs = CMP_STRIDE // 2
    pe_p = per.reshape(N_SUB * pairs, 2 * KV_W)
    w_p = wf.reshape(pairs, 2 * KV_W, N_SUB, half).transpose(2, 0, 1, 3).reshape(N_SUB * pairs, 2 * KV_W, half)
    return pe_p, w_p


def _compress_tail(a0, a1, w2_ref):
    n = a0.shape[0]
    h = a0 + pltpu.roll(a1, n - 1, 0)
    t = h * _sigmoid(h)
    tok = _dot(t.astype(BF16), w2_ref[...])
    return jnp.where(lax.broadcasted_iota(jnp.int32, tok.shape, 0) < n - 1, tok, 0.0)


def _cmp_key_post(tok, kng, bd1, c, sa, sb):
    return _rope(_group_rms(tok, kng[...], bd1[...]), c[...], sa[...], sb[...])


def _compress_chunks(x, pe_ref, w1_ref, w2_ref):
    half = N_KV_HEADS * CMP_HID
    a0 = _dot((x + pe_ref[0:1, :]).astype(BF16), w1_ref[:, :half])
    a1 = _dot((x + pe_ref[1:2, :]).astype(BF16), w1_ref[:, half:])
    return _compress_tail(a0, a1, w2_ref)


def _cmp_prompt_body(xk_ref, xv_ref, pek, w1k, w2k, kng, bd1, c, sa, sb, pev, w1v, w2v, ok_ref, ov_ref):
    ok_ref[0] = _cmp_key_post(_compress_chunks(xk_ref[0], pek, w1k, w2k), kng, bd1, c, sa, sb)
    ov_ref[0] = _compress_chunks(xv_ref[0], pev, w1v, w2v).T


def _cmp_prompt(kc, vc, p):
    b, t, _ = kc.shape
    n_chunk = t // CMP_STRIDE
    pek, w1k, w2k = _compress_weights(p['cmp_k_pos'], p['cmp_k_w1'], p['cmp_k_w2'])
    pev, w1v, w2v = _compress_weights(p['cmp_v_pos'], p['cmp_v_w1'], p['cmp_v_w2'])
    args = [pek, w1k, w2k, *_cmp_key_consts(p, n_chunk), pev, w1v, w2v]
    blk = pl.BlockSpec((1, n_chunk, CHUNK_W), lambda i: (i, 0, 0))
    b3 = lambda i: (i, 0, 0)
    return pl.pallas_call(
        _cmp_prompt_body,
        out_shape=[jax.ShapeDtypeStruct((b, n_chunk, KV_W), F32), jax.ShapeDtypeStruct((b, KV_W, n_chunk), F32)],
        grid=(b,), in_specs=[blk, blk] + [pl.BlockSpec(a.shape, lambda i: (0, 0)) for a in args],
        out_specs=[pl.BlockSpec((1, n_chunk, KV_W), b3), pl.BlockSpec((1, KV_W, n_chunk), b3)],
        compiler_params=_params(1), name="cmp_prompt")(
            kc.reshape(b, n_chunk, CHUNK_W), vc.reshape(b, n_chunk, CHUNK_W), *args)


def _cmp_key_consts(p, n_chunk):
    cmp_end = jnp.arange(n_chunk) * CMP_STRIDE + CMP_LEN - 1
    kng = jnp.tile(p['k_cmp_norm'], N_KV_HEADS).reshape(1, KV_W)
    return [kng, _blockdiag_ones(KV_W), *_rope_tables(cmp_end)]


def _cmp_sample_body(pt_ref, cache_hbm, pe_ref, w1_ref, w2_ref, *rest, n_pages, n_batch, is_key):
    key_consts, (o_ref, buf, tok, a_sc, sem) = rest[:-5], rest[-5:]
    b = pl.program_id(0)
    slot = b % 2
    page_len = cache_hbm.shape[2]
    n_pos = n_pages * page_len
    unroll = 8 if n_pages % 8 == 0 else 1

    def page_copy(page, i, sl):
        dst = pl.ds(pl.multiple_of(i * page_len, page_len), page_len)
        return pltpu.make_async_copy(cache_hbm.at[page], buf.at[sl, :, dst], sem.at[sl])

    def gather(seq, sl):
        def issue(i, carry):
            page_copy(pt_ref[seq * n_pages + i], i, sl).start()
            return carry
        lax.fori_loop(0, n_pages, issue, 0, unroll=unroll)

    @pl.when(b == 0)
    def _():
        gather(0, 0)

    @pl.when(b + 1 < n_batch)
    def _():
        gather(b + 1, 1 - slot)

    def wait(i, carry):
        page_copy(0, i, slot).wait()
        return carry
    lax.fori_loop(0, n_pages, wait, 0, unroll=unroll)

    half = N_KV_HEADS * CMP_HID
    pairs = CMP_STRIDE // 2
    n_blk = 4 if n_pos % (4 * 8 * page_len) == 0 else 1
    pos_blk = n_pos // n_blk
    chunk_blk = pos_blk // CMP_STRIDE
    xw = min(pos_blk, 8 * page_len)
    for j in range(n_blk):
        p0 = j * pos_blk
        for i in range(pos_blk // xw):
            o = p0 + i * xw
            tok[o:o + xw, :] = buf[slot, :, o:o + xw].T
        a0 = jnp.zeros((chunk_blk, half), F32)
        a1 = jnp.zeros((chunk_blk, half), F32)
        for sp in range(pairs):
            xs = jnp.concatenate(
                [tok[pl.ds(p0 + 2 * sp + e, chunk_blk, stride=CMP_STRIDE), :] for e in range(2)], axis=1)
            a0 = a0 + _dot((xs + pe_ref[sp:sp + 1, :]).astype(BF16), w1_ref[sp])
            a1 = a1 + _dot((xs + pe_ref[pairs + sp:pairs + sp + 1, :]).astype(BF16), w1_ref[pairs + sp])
        a_sc[0, j * chunk_blk:(j + 1) * chunk_blk, :] = a0
        a_sc[1, j * chunk_blk:(j + 1) * chunk_blk, :] = a1
    out = _compress_tail(a_sc[0], a_sc[1], w2_ref)
    o_ref[0] = _cmp_key_post(out, *key_consts) if is_key else out


def _cmp_sample(page_table, cache_t, p, is_key):
    b, n_pages = page_table.shape
    page_len = cache_t.shape[2]
    n_pos = n_pages * page_len
    n_chunk = n_pos // CMP_STRIDE
    name = 'k' if is_key else 'v'
    per, wf, w2f = _compress_weights(p[f'cmp_{name}_pos'], p[f'cmp_{name}_w1'], p[f'cmp_{name}_w2'])
    pe_p, w_p = _position_pair_weights(per, wf)
    args = [pe_p, w_p, w2f] + (_cmp_key_consts(p, n_chunk) if is_key else [])
    const = lambda a: pl.BlockSpec(a.shape, lambda i, pt: (0,) * a.ndim)
    grid_spec = pltpu.PrefetchScalarGridSpec(
        num_scalar_prefetch=1, grid=(b,),
        in_specs=[pl.BlockSpec(memory_space=pl.ANY)] + [const(a) for a in args],
        out_specs=pl.BlockSpec((1, n_chunk, KV_W), lambda i, pt: (i, 0, 0)),
        scratch_shapes=[pltpu.VMEM((2, KV_W, n_pos), F32), pltpu.VMEM((n_pos, KV_W), F32),
                        pltpu.VMEM((N_SUB, n_chunk, N_KV_HEADS * CMP_HID), F32),
                        pltpu.SemaphoreType.DMA((2,))])
    return pl.pallas_call(
        functools.partial(_cmp_sample_body, n_pages=n_pages, n_batch=b, is_key=is_key),
        out_shape=jax.ShapeDtypeStruct((b, n_chunk, KV_W), F32), grid_spec=grid_spec,
        compiler_params=_params(1), name=f"cmp_sample_{name}")(page_table.reshape(-1), cache_t, *args)


def _cmp_to_sel(n_chunk, n_cmp, n_sel, width):
    cs = np.arange(n_chunk)[:, None] * CMP_STRIDE
    ss = np.arange(width)[None, :] * SEL_BLK
    ov = np.clip(np.minimum(cs + CMP_LEN, ss + SEL_BLK) - np.maximum(cs, ss), 0, None) / CMP_LEN
    ov = ov * (np.arange(n_chunk)[:, None] < n_cmp) * (np.arange(width)[None, :] < n_sel)
    return jnp.asarray(ov, dtype=BF16)


def _attn_qlanes_body(q_ref, gt_ref, kc_ref, vct_ref, ks_ref, vst_ref, kw_ref, vwt_ref, ovt_ref, o_ref,
                      qt_sc, sel_sc, oct_sc, m_sc, l_sc, acc_sc, s_sc, p_sc, bias_sc,
                      *, t, tq, kc_len, n_cmp, n_sel):
    off = pl.program_id(1) * tq
    cols = GRP * tq
    lane = lax.broadcasted_iota(jnp.int32, (tq, LANES), 1)
    qpos = off + lax.broadcasted_iota(jnp.int32, (1, tq), 1)
    q = q_ref[0] * SCALE
    for h in range(N_HEADS):
        g, r = divmod(h, GRP)
        qt_sc[g, :, r * tq:(r + 1) * tq] = _head_to_group_lanes(q, h, lane).T.astype(BF16)

    def softmax_update(g, n_keys):
        for r in range(GRP):
            cs = slice(r * tq, (r + 1) * tq)

            def scores():
                return s_sc[:n_keys, cs] * LOG2E + bias_sc[:n_keys, :]

            m_old = m_sc[g, :, cs]
            m_new = jnp.maximum(m_old, jnp.max(scores(), axis=0, keepdims=True))
            m_sc[g, :, cs] = m_new
            p = jnp.exp2(scores() - m_new)
            alpha = jnp.exp2(m_old - m_new)
            l_sc[g, :, cs] = alpha * l_sc[g, :, cs] + jnp.sum(p, axis=0, keepdims=True)
            acc_sc[g, :, cs] = alpha * acc_sc[g, :, cs]
            p_sc[:n_keys, cs] = p.astype(BF16)

    def reset_softmax():
        m_sc[...] = jnp.full(m_sc.shape, NEG_INF, F32)
        l_sc[...] = jnp.zeros(l_sc.shape, F32)
        acc_sc[...] = jnp.zeros(acc_sc.shape, F32)

    n_chunk = kc_ref.shape[1]
    kcm = kc_ref[0].astype(BF16)
    vct = vct_ref[0].astype(BF16)
    nrow = lax.broadcasted_iota(jnp.int32, (n_chunk, tq), 0)
    mask_1 = jnp.logical_and(nrow * CMP_STRIDE + (CMP_LEN - 1) <= qpos, nrow < n_cmp)
    mask_c = jnp.concatenate([mask_1.astype(F32)] * GRP, axis=1) > 0.5
    psum = []
    for g in range(N_KV_HEADS):
        sc = jnp.where(mask_c, _dot(kcm, qt_sc[g]), NEG_INF)
        e = jnp.where(mask_c, jnp.exp(sc - jnp.max(sc, axis=0, keepdims=True)), 0.0)
        pc = e / jnp.maximum(jnp.sum(e, axis=0, keepdims=True), 1e-30)
        oct_sc[g] = _dot(vct, pc.astype(BF16))
        psum.append(sum(pc[:, r * tq:(r + 1) * tq] for r in range(GRP)))

    n_sel_pad = -(-n_sel // 8) * 8
    jrow = lax.broadcasted_iota(jnp.int32, (n_sel_pad, tq), 0)
    cur = qpos // SEL_BLK
    forced = jnp.logical_or(jrow == 0, jnp.logical_or(jrow == cur, jrow == cur - 1))
    bonus = jnp.where(forced, FORCE_BONUS, 0.0)
    k_eff = min(TOPK, n_sel)
    ovt = ovt_ref[...]
    for g in range(N_KV_HEADS):
        hi = psum[g].astype(BF16)
        lo = (psum[g] - hi.astype(F32)).astype(BF16)
        imp = _dot(ovt, hi) + _dot(ovt, lo)
        score = jnp.where(jrow <= cur, imp + bonus, NEG_INF)
        rank = jnp.zeros((n_sel_pad, tq), F32)
        for j in range(n_sel):
            row = score[j:j + 1, :]
            tie = jnp.where(jrow > j, 1.0, 0.0)
            rank = rank + jnp.where(row > score, 1.0, jnp.where(row == score, tie, 0.0))
        sel = jnp.where(rank < k_eff, 1.0, 0.0)
        sel_sc[g] = jnp.concatenate([sel, jnp.zeros((LANES - n_sel_pad, tq), F32)], axis=0).astype(BF16)

    reset_softmax()
    key_row = lax.broadcasted_iota(jnp.int32, (kc_len, LANES), 0)
    blk_col = lax.broadcasted_iota(jnp.int32, (kc_len, LANES), 1)
    for ci in range(t // kc_len):
        @pl.when(ci * kc_len < off + tq)
        def _(ci=ci):
            base = ci * kc_len
            expand = jnp.where(blk_col == (base + key_row) // SEL_BLK, 1.0, 0.0).astype(BF16)
            causal = base + lax.broadcasted_iota(jnp.int32, (kc_len, tq), 0) <= qpos
            k = ks_ref[0, base:base + kc_len, :].astype(BF16)
            vt = vst_ref[0, :, base:base + kc_len].astype(BF16)
            for g in range(N_KV_HEADS):
                mask = jnp.logical_and(_dot(expand, sel_sc[g]) > 0.5, causal)
                bias_sc[:kc_len, :] = jnp.where(mask, 0.0, -jnp.inf)
                s_sc[:kc_len, :] = _dot(k, qt_sc[g])
                softmax_update(g, kc_len)
                acc_sc[g] = acc_sc[g] + _dot(vt, p_sc[:kc_len, :])
    o_sel = [acc_sc[g] / jnp.maximum(l_sc[g], 1e-30) for g in range(N_KV_HEADS)]

    wl = WINDOW + tq
    ws = pl.multiple_of(jnp.maximum(off - WINDOW, 0), LANES)
    kw = kw_ref[0, pl.ds(ws, wl), :].astype(BF16)
    vwt = vwt_ref[0, :, pl.ds(ws, wl)].astype(BF16)
    dist = qpos - (ws + lax.broadcasted_iota(jnp.int32, (wl, tq), 0))
    bias_sc[:wl, :] = jnp.where(jnp.logical_and(dist >= 0, dist < WINDOW), 0.0, -jnp.inf)
    reset_softmax()
    gtt = gt_ref[0].T
    outs = []
    for g in range(N_KV_HEADS):
        s_sc[:wl, :] = _dot(kw, qt_sc[g])
        softmax_update(g, wl)
        o_w = _dot(vwt, p_sc[:wl, :]) / jnp.maximum(l_sc[g], 1e-30)
        o_s = o_sel[g]
        o_c = oct_sc[g]
        for r in range(GRP):
            h = g * GRP + r
            cs = slice(r * tq, (r + 1) * tq)
            o_h = (gtt[3 * h:3 * h + 1, :] * o_c[:, cs] + gtt[3 * h + 1:3 * h + 2, :] * o_s[:, cs]
                   + gtt[3 * h + 2:3 * h + 3, :] * o_w[:, cs])
            outs.append(o_h.T)
    o_ref[0] = jnp.concatenate(
        [_pair_tile(outs[2 * p], outs[2 * p + 1], 2 * p, lane) for p in range(N_HEADS // 2)], axis=-1)


def _attn_qlanes(q, gt, kcmp, vcmp_t, ks, vs_t, kw, vw_t, tq=128, kc_len=512):
    b, t, _ = q.shape
    n_chunk = kcmp.shape[1]
    n_cmp = n_chunk - N_SUB + 1
    n_sel = -(-t // SEL_BLK)
    assert t % kc_len == 0 and t >= WINDOW + tq and n_sel <= LANES and n_chunk % LANES == 0
    assert tq % LANES == 0
    n_sel_pad = -(-n_sel // 8) * 8
    ov_t = _cmp_to_sel(n_chunk, n_cmp, n_sel, n_sel_pad).T
    qblk = lambda i, j: (i, j, 0)
    full = lambda i, j: (i, 0, 0)
    tok = pl.BlockSpec((1, t, KV_W), full)
    tr = pl.BlockSpec((1, KV_W, t), full)
    cols = GRP * tq
    n_keys = max(kc_len, WINDOW + tq)
    body = functools.partial(_attn_qlanes_body, t=t, tq=tq, kc_len=kc_len, n_cmp=n_cmp, n_sel=n_sel)
    return pl.pallas_call(
        body, out_shape=jax.ShapeDtypeStruct((b, t, D_ATT), F32), grid=(b, t // tq),
        in_specs=[pl.BlockSpec((1, tq, D_ATT), qblk), pl.BlockSpec((1, tq, LANES), qblk),
                  pl.BlockSpec((1, n_chunk, KV_W), full), pl.BlockSpec((1, KV_W, n_chunk), full),
                  tok, tr, tok, tr, pl.BlockSpec(ov_t.shape, lambda i, j: (0, 0))],
        out_specs=pl.BlockSpec((1, tq, D_ATT), qblk),
        scratch_shapes=[pltpu.VMEM((N_KV_HEADS, KV_W, cols), BF16), pltpu.VMEM((N_KV_HEADS, LANES, tq), BF16),
                        pltpu.VMEM((N_KV_HEADS, KV_W, cols), F32), pltpu.VMEM((N_KV_HEADS, 1, cols), F32),
                        pltpu.VMEM((N_KV_HEADS, 1, cols), F32), pltpu.VMEM((N_KV_HEADS, KV_W, cols), F32),
                        pltpu.VMEM((n_keys, cols), F32), pltpu.VMEM((n_keys, cols), BF16),
                        pltpu.VMEM((n_keys, tq), F32)],
        compiler_params=_params(2), name="attn_prompt")(q, gt, kcmp, vcmp_t, ks, vs_t, kw, vw_t, ov_t)


def _sample_heads(q_row):
    lane = lax.broadcasted_iota(jnp.int32, (1, LANES), 1)
    return jnp.concatenate([_head_to_group_lanes(q_row, h, lane) for h in range(N_HEADS)], axis=0)


def _attn_s1_body(q_ref, kc_ref, vc_ref, ov_ref, oc_ref, idx_ref, *, n_cmp, n_sel, q_pos):
    qh = _sample_heads(q_ref[0])
    n_chunk = kc_ref.shape[1]
    ncol = lax.broadcasted_iota(jnp.int32, (N_HEADS, n_chunk), 1)
    mask = jnp.logical_and(ncol * CMP_STRIDE + (CMP_LEN - 1) <= q_pos, ncol < n_cmp)
    pc = _masked_softmax(_dot_t(qh.astype(BF16), kc_ref[0].astype(BF16)) * SCALE, mask)
    oc_ref[0] = _dot(pc.astype(BF16), vc_ref[0].astype(BF16))
    rows = [jnp.sum(pc[g * GRP:(g + 1) * GRP, :], axis=0, keepdims=True) for g in range(N_KV_HEADS)]
    psum = jnp.concatenate(rows + [jnp.zeros((N_HEADS - N_KV_HEADS, n_chunk), F32)], axis=0)
    imp = _split_dot(psum, ov_ref[...])
    width = imp.shape[1]
    jl = lax.broadcasted_iota(jnp.int32, (N_HEADS, width), 1)
    cur = q_pos // SEL_BLK
    forced = jnp.logical_or(jl == 0, jnp.logical_or(jl == cur, jl == cur - 1))
    score = jnp.where(jl <= cur, imp + jnp.where(forced, FORCE_BONUS, 0.0), NEG_INF)
    ii = lax.broadcasted_iota(jnp.int32, (width, width), 0)
    jj = lax.broadcasted_iota(jnp.int32, (width, width), 1)
    slot = lax.broadcasted_iota(jnp.int32, (width, LANES), 1).astype(F32)
    blk = lax.broadcasted_iota(jnp.int32, (width, LANES), 0).astype(F32)
    tie = jnp.where(jj < ii, 1.0, 0.0)
    out_rows = []
    for g in range(N_KV_HEADS):
        row = jnp.broadcast_to(score[g:g + 1, :], (width, width))
        col = jnp.sum(jnp.where(ii == jj, row, 0.0), axis=1, keepdims=True)
        beats = jnp.where(row > col, 1.0, jnp.where(row == col, tie, 0.0))
        rank = jnp.sum(beats, axis=1, keepdims=True)
        out_rows.append(jnp.sum(jnp.where(rank == slot, blk, 0.0), axis=0, keepdims=True))
    out_rows.append(jnp.zeros((N_HEADS - N_KV_HEADS, LANES), F32))
    idx_ref[0] = jnp.concatenate(out_rows, axis=0).astype(jnp.int32)


def _attn_s1(q, kcmp, vcmp, q_pos, n_sel):
    b = q.shape[0]
    n_chunk = kcmp.shape[1]
    n_cmp = n_chunk - N_SUB + 1
    width = -(-n_sel // LANES) * LANES
    ov = _cmp_to_sel(n_chunk, n_cmp, n_sel, width)
    blk3 = lambda i: (i, 0, 0)
    cm = pl.BlockSpec((1, n_chunk, KV_W), blk3)
    out = pl.BlockSpec((1, N_HEADS, LANES), blk3)
    return pl.pallas_call(
        functools.partial(_attn_s1_body, n_cmp=n_cmp, n_sel=n_sel, q_pos=q_pos),
        out_shape=[jax.ShapeDtypeStruct((b, N_HEADS, LANES), F32),
                   jax.ShapeDtypeStruct((b, N_HEADS, LANES), jnp.int32)],
        grid=(b,),
        in_specs=[pl.BlockSpec((1, 1, D_ATT), blk3), cm, cm, pl.BlockSpec(ov.shape, lambda i: (0, 0))],
        out_specs=[out, out], compiler_params=_params(1), name="attn_sample_select")(
            q.reshape(b, 1, D_ATT), kcmp, vcmp, ov)


def _extra_key_attention(qh, k_t, v_t, mask, k_new, v_new):
    qf = qh.astype(F32)
    s = jnp.where(mask, _dot(qh, k_t.astype(BF16)) * SCALE, NEG_INF)
    s_new = jnp.sum(qf * k_new, axis=-1, keepdims=True) * SCALE
    m = jnp.maximum(jnp.max(s, axis=-1, keepdims=True), s_new)
    e = jnp.where(mask, jnp.exp(s - m), 0.0)
    e_new = jnp.exp(s_new - m)
    denom = jnp.maximum(jnp.sum(e, axis=-1, keepdims=True) + e_new, 1e-30)
    return (_dot_t(e.astype(BF16), v_t.astype(BF16)) + e_new * v_new) / denom


def _attn_s2_body(idx_ref, pt_ref, q_ref, gt_ref, oc_ref, ksn_ref, vsn_ref, kwn_ref, vwn_ref,
                  kws_ref, vws_ref, kcache, vcache, o_ref, kbuf, vbuf, sem, *, n_pages, n_cache_blk, q_pos):
    b = pl.program_id(0)
    page_len = kcache.shape[2]
    per_page = page_len // SEL_BLK

    def copies(g, k, blk):
        blk = jnp.minimum(blk, n_cache_blk - 1)
        page = pt_ref[b * n_pages + blk // per_page]
        dst = pl.ds(k * page_len, page_len)
        return (pltpu.make_async_copy(kcache.at[page], kbuf.at[g, :, dst], sem.at[0]),
                pltpu.make_async_copy(vcache.at[page], vbuf.at[g, :, dst], sem.at[1]))

    blks = [[idx_ref[(b * N_KV_HEADS + g) * TOPK + k] for k in range(TOPK)] for g in range(N_KV_HEADS)]
    for g in range(N_KV_HEADS):
        for k in range(TOPK):
            for cp in copies(g, k, blks[g][k]):
                cp.start()
    for g in range(N_KV_HEADS):
        for k in range(TOPK):
            for cp in copies(g, k, blks[g][k]):
                cp.wait()

    qh = _sample_heads(q_ref[0]).astype(BF16)
    n_keys = TOPK * page_len
    row_grp = lax.broadcasted_iota(jnp.int32, (N_HEADS, n_keys), 0) // GRP
    key = lax.broadcasted_iota(jnp.int32, (N_HEADS, n_keys), 1)
    slot = key // page_len
    blk_in_page = (key % page_len) // SEL_BLK
    keep = jnp.zeros((N_HEADS, n_keys), F32)
    for g in range(N_KV_HEADS):
        for k in range(TOPK):
            blk = blks[g][k]
            in_cache = jnp.where(blk < n_cache_blk, 1.0, 0.0)
            hit = jnp.where(blk_in_page == blk % per_page, in_cache, 0.0)
            keep = jnp.where(jnp.logical_and(row_grp == g, slot == k), hit, keep)
    mask_s = keep > 0.5
    o_sel = [_extra_key_attention(qh, kbuf[g], vbuf[g], mask_s, ksn_ref[0], vsn_ref[0])
             for g in range(N_KV_HEADS)]
    out_grp = lax.broadcasted_iota(jnp.int32, (N_HEADS, LANES), 0) // GRP
    o_s = jnp.where(out_grp == 0, o_sel[0], o_sel[1])

    n_win = kws_ref.shape[2]
    kpos = (q_pos - n_win) + lax.broadcasted_iota(jnp.int32, (N_HEADS, n_win), 1)
    dist = q_pos - kpos
    mask_w = jnp.logical_and(dist >= 0, dist < WINDOW)
    o_w = _extra_key_attention(qh, kws_ref[0], vws_ref[0], mask_w, kwn_ref[0], vwn_ref[0])

    lane8 = lax.broadcasted_iota(jnp.int32, (N_HEADS, LANES), 1)
    head8 = lax.broadcasted_iota(jnp.int32, (N_HEADS, LANES), 0)
    gt = jnp.broadcast_to(gt_ref[0], (N_HEADS, LANES))
    gate = [jnp.sum(jnp.where(lane8 == 3 * head8 + j, gt, 0.0), axis=-1, keepdims=True) for j in range(3)]
    o = gate[0] * oc_ref[0] + gate[1] * o_s + gate[2] * o_w
    lane = lax.broadcasted_iota(jnp.int32, (1, LANES), 1)
    o_ref[0] = jnp.concatenate(
        [_pair_tile(o[2 * p:2 * p + 1, :], o[2 * p + 1:2 * p + 2, :], 2 * p, lane)
         for p in range(N_HEADS // 2)], axis=-1)


def _attn_s2(idx, page_table, q, gt, oc, ks_new, vs_new, kw_new, vw_new, kw_state_t, vw_state_t,
             cache_k_t, cache_v_t, q_pos):
    b, n_pages = page_table.shape
    page_len = cache_k_t.shape[2]
    n_win = kw_state_t.shape[2]
    assert n_win <= WINDOW
    blk3 = lambda i, *_: (i, 0, 0)
    row = lambda w: pl.BlockSpec((1, 1, w), blk3)
    grid_spec = pltpu.PrefetchScalarGridSpec(
        num_scalar_prefetch=2, grid=(b,),
        in_specs=[row(D_ATT), row(LANES), pl.BlockSpec((1, N_HEADS, LANES), blk3),
                  row(KV_W), row(KV_W), row(KV_W), row(KV_W),
                  pl.BlockSpec((1, KV_W, n_win), blk3), pl.BlockSpec((1, KV_W, n_win), blk3),
                  pl.BlockSpec(memory_space=pl.ANY), pl.BlockSpec(memory_space=pl.ANY)],
        out_specs=row(D_ATT),
        scratch_shapes=[pltpu.VMEM((N_KV_HEADS, KV_W, TOPK * page_len), F32),
                        pltpu.VMEM((N_KV_HEADS, KV_W, TOPK * page_len), F32),
                        pltpu.SemaphoreType.DMA((2,))])
    r3 = lambda a: a.reshape(b, 1, a.shape[-1])
    body = functools.partial(_attn_s2_body, n_pages=n_pages, n_cache_blk=n_pages * (page_len // SEL_BLK),
                             q_pos=q_pos)
    return pl.pallas_call(
        body, out_shape=jax.ShapeDtypeStruct((b, 1, D_ATT), F32), grid_spec=grid_spec,
        compiler_params=_params(1), name="attn_sample")(
            idx.reshape(-1), page_table.reshape(-1), r3(q), r3(gt), oc, r3(ks_new), r3(vs_new),
            r3(kw_new), r3(vw_new), kw_state_t, vw_state_t, cache_k_t, cache_v_t).reshape(b, D_ATT)


def _out_body(x_ref, c_ref, o_ref, og_ref, wc_ref, wa_ref, y_ref):
    on = _rms(o_ref[...], og_ref[...])
    y_ref[...] = (x_ref[...] + _dot(c_ref[...].astype(BF16), wc_ref[...])
                  + _dot(on.astype(BF16), wa_ref[...]))


def _out_proj(x2, cn, o, p, tm):
    rows, d = x2.shape
    w = p['w_out'].astype(BF16)
    row = lambda i: (i, 0)
    const = lambda i: (0, 0)
    return pl.pallas_call(
        _out_body, out_shape=jax.ShapeDtypeStruct((rows, d), F32), grid=(rows // tm,),
        in_specs=[pl.BlockSpec((tm, d), row), pl.BlockSpec((tm, C_CONV), row), pl.BlockSpec((tm, D_ATT), row),
                  pl.BlockSpec((1, D_ATT), const), pl.BlockSpec((C_CONV, d), const),
                  pl.BlockSpec((D_ATT, d), const)],
        out_specs=pl.BlockSpec((tm, d), row), compiler_params=_params(1), name="out_proj")(
            x2, cn, o, p['out_norm_attn'].reshape(1, D_ATT), w[:C_CONV], w[C_CONV:])


def _prompt_layer(x, p, tm=512):
    b, t, d = x.shape
    x2 = _ffn(x.reshape(b * t, d), p['ffn1_norm'], p['ffn1_w_in'], p['ffn1_w_out'], None, tm)
    u, q, gt, kc, vc, ks, kw, kc_t, vc_t, ks_t, vs_t, kw_t, vw_t = _proj(
        x2, p, _rope_tables(jnp.arange(t)), tm, seq_len=t)
    r3 = lambda a: a.reshape(b, t, a.shape[-1])
    u3 = r3(u)
    cn = _conv_prompt(u3, p)
    kcmp, vcmp_t = _cmp_prompt(r3(kc), r3(vc), p)
    o = _attn_qlanes(r3(q), r3(gt), kcmp, vcmp_t, r3(ks), vs_t, r3(kw), vw_t)
    x2 = _out_proj(x2, cn.reshape(b * t, C_CONV), o.reshape(b * t, D_ATT), p, tm)
    y = _ffn(x2, p['ffn2_norm'], p['ffn2_w_in'], p['ffn2_w_out'], p['final_norm'], tm)
    r4 = lambda a: a.reshape(b, N_KV_HEADS, HEAD_DIM, a.shape[-1]).transpose(0, 3, 1, 2)
    keep = min(WINDOW, t)
    state = (r4(kc_t), r4(vc_t), r4(ks_t), r4(vs_t), r4(kw_t[:, :, t - keep:]), r4(vw_t[:, :, t - keep:]),
             u3[:, t - (CONV_W - 1):])
    return y.reshape(b, t, d), state


def _sample_layer(x, p, cache_k_cmp, cache_v_cmp, cache_k_sel, cache_v_sel, kw_state, vw_state,
                  conv_state, page_table):
    b, t, d = x.shape
    assert t == 1
    past_len = page_table.shape[1] * cache_k_cmp.shape[1]
    x2 = _ffn(x.reshape(b, d), p['ffn1_norm'], p['ffn1_w_in'], p['ffn1_w_out'], None, b)
    tabs = _rope_tables(jnp.full((b,), past_len, jnp.int32))
    u, q, gt, kc, vc, ks, vs, kw, vw = _proj(x2, p, tabs, b)
    cn = _conv_sample(conv_state, u, p)
    keys_on_lanes = lambda a: a.transpose(0, 2, 3, 1).reshape(a.shape[0], KV_W, a.shape[1])
    kcmp = _cmp_sample(page_table, keys_on_lanes(cache_k_cmp), p, True)
    vcmp = _cmp_sample(page_table, keys_on_lanes(cache_v_cmp), p, False)
    n_sel = -(-(past_len + 1) // SEL_BLK)
    oc, idx = _attn_s1(q, kcmp, vcmp, past_len, n_sel)
    n_win = kw_state.shape[1]
    o = _attn_s2(idx[:, :N_KV_HEADS, :TOPK], page_table, q, gt, oc, ks, vs, kw, vw,
                 keys_on_lanes(kw_state), keys_on_lanes(vw_state),
                 keys_on_lanes(cache_k_sel), keys_on_lanes(cache_v_sel), past_len)
    x2 = _out_proj(x2, cn, o, p, b)
    y = _ffn(x2, p['ffn2_norm'], p['ffn2_w_in'], p['ffn2_w_out'], p['final_norm'], b)
    r4 = lambda a: a.reshape(b, 1, N_KV_HEADS, HEAD_DIM)
    keep = min(WINDOW, n_win + 1)
    new_kw = jnp.concatenate([kw_state, r4(kw)], axis=1)[:, n_win + 1 - keep:]
    new_vw = jnp.concatenate([vw_state, r4(vw)], axis=1)[:, n_win + 1 - keep:]
    new_conv = jnp.concatenate([conv_state, u[:, None, :]], axis=1)[:, 1:]
    return y.reshape(b, 1, d), (r4(kc), r4(vc), r4(ks), r4(vs), new_kw, new_vw, new_conv)


_PARAM_NAMES = ('ffn1_norm', 'ffn1_w_in', 'ffn1_w_out', 'mix_norm', 'w_in', 'conv_w', 'conv_b', 'conv_ln_g',
                'conv_ln_b', 'q_norm', 'k_cmp_norm', 'k_sel_norm', 'k_win_norm', 'cmp_k_pos', 'cmp_k_w1',
                'cmp_k_w2', 'cmp_v_pos', 'cmp_v_w1', 'cmp_v_w2', 'out_norm_conv', 'out_norm_attn', 'w_out',
                'ffn2_norm', 'ffn2_w_in', 'ffn2_w_out', 'final_norm')


def kernel(x_prompt, x_sample, cache_k_cmp, cache_v_cmp, cache_k_sel, cache_v_sel, state_k_win, state_v_win,
           state_conv, page_table, ffn1_norm, ffn1_w_in, ffn1_w_out, mix_norm, w_in, conv_w, conv_b, conv_ln_g,
           conv_ln_b, q_norm, k_cmp_norm, k_sel_norm, k_win_norm, cmp_k_pos, cmp_k_w1, cmp_k_w2, cmp_v_pos,
           cmp_v_w1, cmp_v_w2, out_norm_conv, out_norm_attn, w_out, ffn2_norm, ffn2_w_in, ffn2_w_out, final_norm):
    stacked = dict(zip(_PARAM_NAMES, (
        ffn1_norm, ffn1_w_in, ffn1_w_out, mix_norm, w_in, conv_w, conv_b, conv_ln_g, conv_ln_b, q_norm,
        k_cmp_norm, k_sel_norm, k_win_norm, cmp_k_pos, cmp_k_w1, cmp_k_w2, cmp_v_pos, cmp_v_w1, cmp_v_w2,
        out_norm_conv, out_norm_attn, w_out, ffn2_norm, ffn2_w_in, ffn2_w_out, final_norm)))
    depth = ffn1_norm.shape[0]
    yp, ys = x_prompt, x_sample
    new_p, new_s = [], []
    for l in range(depth):
        p = {k: v[l] for k, v in stacked.items()}
        yp, st_p = _prompt_layer(yp, p)
        new_p.append(st_p)
        ys, st_s = _sample_layer(ys, p, cache_k_cmp[l], cache_v_cmp[l], cache_k_sel[l], cache_v_sel[l],
                                 state_k_win[l], state_v_win[l], state_conv[l], page_table)
        new_s.append(st_s)
    outs_p = [jnp.stack(tup) for tup in zip(*new_p)]
    outs_s = [jnp.stack(tup) for tup in zip(*new_s)]
    return (yp, ys, *outs_p, *outs_s)
```

```python
import functools

import numpy as np
import jax
import jax.numpy as jnp
from jax import lax
from jax.experimental import pallas as pl
from jax.experimental.pallas import tpu as pltpu

F32 = jnp.float32
BF16 = jnp.bfloat16

HEAD_DIM = 64
N_HEADS = 8
N_KV_HEADS = 2
GRP = N_HEADS // N_KV_HEADS
KV_W = N_KV_HEADS * HEAD_DIM
C_CONV = 512
D_ATT = 512
CONV_W = 31
CMP_LEN = 32
CMP_STRIDE = 16
N_SUB = CMP_LEN // CMP_STRIDE
CMP_HID = 2 * HEAD_DIM
SEL_BLK = 64
TOPK = 16
WINDOW = 512
ROT_DIM = HEAD_DIM // 4
ROT_HALF = ROT_DIM // 2
ROPE_THETA = 500000.0
NEG_INF = -1e30
FORCE_BONUS = 1e4
EPS = 1e-6
SCALE = HEAD_DIM ** -0.5
LOG2E = 1.4426950408889634
LANES = 128
CHUNK_W = CMP_STRIDE * KV_W
VMEM_LIMIT = 56 * 1024 * 1024


def _params(n_grid_dims):
    return pltpu.CompilerParams(
        dimension_semantics=("arbitrary",) * n_grid_dims, vmem_limit_bytes=VMEM_LIMIT)


def _dot(a, b):
    return jnp.dot(a, b, preferred_element_type=F32)


def _dot_t(a, b):
    return lax.dot_general(a, b, (((1,), (1,)), ((), ())), preferred_element_type=F32)


def _split_dot(x, m_bf16):
    hi = x.astype(BF16)
    lo = (x - hi.astype(F32)).astype(BF16)
    return _dot(hi, m_bf16) + _dot(lo, m_bf16)


def _sigmoid(x):
    return 1.0 / (1.0 + jnp.exp(-x))


def _rms(x, g):
    return x * lax.rsqrt(jnp.mean(x * x, axis=-1, keepdims=True) + EPS) * g


def _group_rms(x, g_tiled, ones_blockdiag):
    ss = _split_dot(x * x, ones_blockdiag)
    return x * lax.rsqrt(ss * (1.0 / HEAD_DIM) + EPS) * g_tiled


def _rope(x, c, sa, sb):
    n = x.shape[-1]
    return x * c + pltpu.roll(x, n - ROT_HALF, 1) * sa + pltpu.roll(x, ROT_HALF, 1) * sb


def _tile_lanes(t, reps):
    return jnp.concatenate([t] * reps, axis=-1) if reps > 1 else t


def _masked_softmax(s, mask):
    s = jnp.where(mask, s, NEG_INF)
    m = jnp.max(s, axis=-1, keepdims=True)
    e = jnp.where(mask, jnp.exp(s - m), 0.0)
    return e / jnp.maximum(jnp.sum(e, axis=-1, keepdims=True), 1e-30)


def _head_to_group_lanes(q, h, lane):
    g, p = h // GRP, h // 2
    qp = q[:, LANES * p:LANES * (p + 1)]
    src = qp if (h % 2) == g else pltpu.roll(qp, HEAD_DIM, 1)
    keep = (lane < HEAD_DIM) if g == 0 else (lane >= HEAD_DIM)
    return jnp.where(keep, src, 0.0)


def _pair_tile(o_even, o_odd, h_even, lane):
    g = h_even // GRP
    a = o_even if g == 0 else pltpu.roll(o_even, HEAD_DIM, 1)
    b = pltpu.roll(o_odd, HEAD_DIM, 1) if g == 0 else o_odd
    return jnp.where(lane < HEAD_DIM, a, b)


def _ffn_body(*refs, n_chunk, tf, ff, final, mixed):
    refs = list(refs)
    x_ref = refs.pop(0)
    x = x_ref[...]
    if mixed:
        c_ref, a_ref, ag_ref, wc_ref, wa_ref = refs[:5]
        refs = refs[5:]
        x = (x + _dot(c_ref[...].astype(BF16), wc_ref[...])
             + _dot(_rms(a_ref[...], ag_ref[...]).astype(BF16), wa_ref[...]))
    g_ref, wi_ref, wo_ref = refs[:3]
    fg_ref = refs[3] if final else None
    o_ref = refs[-1]
    xn = _rms(x, g_ref[...]).astype(BF16)
    acc = jnp.zeros(x.shape, F32)
    for c in range(n_chunk):
        a = _dot(xn, wi_ref[:, c * tf:(c + 1) * tf])
        b = _dot(xn, wi_ref[:, ff + c * tf:ff + (c + 1) * tf])
        h = (a * _sigmoid(a)) * b
        acc = acc + _dot(h.astype(BF16), wo_ref[c * tf:(c + 1) * tf, :])
    y = x + 0.5 * acc
    if final:
        y = _rms(y, fg_ref[...])
    o_ref[...] = y


def _ffn(x2, norm_g, w_in, w_out, final_g, tm, mix=None):
    rows, d = x2.shape
    ff = w_out.shape[0]
    tf = 256
    assert rows % tm == 0 and ff % tf == 0 and w_in.shape == (d, 2 * ff)
    const = lambda i: (0, 0)
    row = lambda i: (i, 0)
    single = pl.Buffered(1)
    in_specs = [pl.BlockSpec((tm, d), row)]
    args = [x2]
    if mix is not None:
        cn, att, att_g, w_mix = mix
        w_mix = w_mix.astype(BF16)
        in_specs += [pl.BlockSpec((tm, C_CONV), row), pl.BlockSpec((tm, D_ATT), row),
                     pl.BlockSpec((1, D_ATT), const), pl.BlockSpec((C_CONV, d), const),
                     pl.BlockSpec((D_ATT, d), const)]
        args += [cn, att, att_g.reshape(1, D_ATT), w_mix[:C_CONV], w_mix[C_CONV:]]
    in_specs += [
        pl.BlockSpec((1, d), const),
        pl.BlockSpec((d, 2 * ff), const, pipeline_mode=single),
        pl.BlockSpec((ff, d), const, pipeline_mode=single),
    ]
    args += [norm_g.reshape(1, d), w_in.astype(BF16), w_out.astype(BF16)]
    if final_g is not None:
        in_specs.append(pl.BlockSpec((1, d), const))
        args.append(final_g.reshape(1, d))
    body = functools.partial(_ffn_body, n_chunk=ff // tf, tf=tf, ff=ff, final=final_g is not None,
                             mixed=mix is not None)
    return pl.pallas_call(
        body, out_shape=jax.ShapeDtypeStruct((rows, d), F32), grid=(rows // tm,),
        in_specs=in_specs, out_specs=pl.BlockSpec((tm, d), lambda i: (i, 0)),
        compiler_params=_params(1), name="ffn")(*args)


_SEG_GLU = 2 * C_CONV
_SEG_Q = _SEG_GLU + D_ATT
N_IN = _SEG_Q + 6 * KV_W + 3 * N_HEADS
N_IN_PAD = _SEG_Q + 7 * KV_W


def _proj_body(x_ref, g_ref, w_ref, qg_ref, ksg_ref, kwg_ref, bd4_ref, bd1_ref, c_ref, sa_ref, sb_ref,
               u_ref, q_ref, gt_ref, kc_ref, vc_ref, *kv_refs, transposed):
    xn = _rms(x_ref[...], g_ref[...]).astype(BF16)

    def seg(lo, width):
        return _dot(xn, w_ref[:, lo:lo + width])

    c, sa, sb = c_ref[...], sa_ref[...], sb_ref[...]
    reps = D_ATT // LANES
    glu = seg(0, _SEG_GLU)
    u_ref[...] = glu[:, :C_CONV] * _sigmoid(glu[:, C_CONV:])
    q = _group_rms(seg(_SEG_GLU, D_ATT), qg_ref[...], bd4_ref[...])
    q_ref[...] = _rope(q, _tile_lanes(c, reps), _tile_lanes(sa, reps), _tile_lanes(sb, reps))
    base = _SEG_Q
    kc = seg(base, KV_W)
    vc = seg(base + KV_W, KV_W)
    ks = _rope(_group_rms(seg(base + 2 * KV_W, KV_W), ksg_ref[...], bd1_ref[...]), c, sa, sb)
    vs = seg(base + 3 * KV_W, KV_W)
    kw = _rope(_group_rms(seg(base + 4 * KV_W, KV_W), kwg_ref[...], bd1_ref[...]), c, sa, sb)
    vw = seg(base + 5 * KV_W, KV_W)
    gt_ref[...] = _sigmoid(seg(base + 6 * KV_W, KV_W))
    kc_ref[...] = kc
    vc_ref[...] = vc
    if transposed:
        kv_refs[0][...] = ks
        kv_refs[1][...] = kw
        for ref, val in zip(kv_refs[2:], (kc, vc, ks, vs, kw, vw)):
            ref[0] = val.T
    else:
        for ref, val in zip(kv_refs, (ks, vs, kw, vw)):
            ref[...] = val


def _blockdiag_ones(width):
    idx = np.arange(width) // HEAD_DIM
    return jnp.asarray(idx[:, None] == idx[None, :], dtype=BF16)


def _rope_tables(pos):
    pos = np.asarray(pos, dtype=np.float64)
    inv = ROPE_THETA ** (-(np.arange(ROT_HALF, dtype=np.float64) * 2.0 / ROT_DIM))
    ang = pos[:, None] * inv[None, :]
    cos, sin = np.cos(ang).astype(np.float32), np.sin(ang).astype(np.float32)
    n = pos.shape[0]
    ones = np.ones((n, HEAD_DIM - ROT_DIM), np.float32)
    zeros = np.zeros((n, HEAD_DIM - ROT_DIM), np.float32)
    z8 = np.zeros((n, ROT_HALF), np.float32)
    c = np.concatenate([cos, cos, ones], axis=1)
    sa = np.concatenate([-sin, z8, zeros], axis=1)
    sb = np.concatenate([z8, sin, zeros], axis=1)
    return tuple(jnp.asarray(np.tile(t, (1, N_KV_HEADS))) for t in (c, sa, sb))


def _proj(x2, p, tabs, tm, seq_len=None):
    rows, d = x2.shape
    n_tab = tabs[0].shape[0] // tm
    w = jnp.pad(p['w_in'], ((0, 0), (0, N_IN_PAD - N_IN))).astype(BF16)
    const = lambda i: (0, 0)
    row = lambda i: (i, 0)
    tab = lambda i: (i % n_tab, 0)
    in_specs = [
        pl.BlockSpec((tm, d), row), pl.BlockSpec((1, d), const),
        pl.BlockSpec((d, N_IN_PAD), const, pipeline_mode=pl.Buffered(1)),
        pl.BlockSpec((1, D_ATT), const), pl.BlockSpec((1, KV_W), const), pl.BlockSpec((1, KV_W), const),
        pl.BlockSpec((D_ATT, D_ATT), const), pl.BlockSpec((KV_W, KV_W), const),
        pl.BlockSpec((tm, LANES), tab), pl.BlockSpec((tm, LANES), tab), pl.BlockSpec((tm, LANES), tab),
    ]
    widths = [C_CONV, D_ATT, LANES, KV_W, KV_W] + [KV_W] * (4 if seq_len is None else 2)
    out_shape = [jax.ShapeDtypeStruct((rows, wd), F32) for wd in widths]
    out_specs = [pl.BlockSpec((tm, wd), row) for wd in widths]
    if seq_len is not None:
        nt = seq_len // tm
        assert seq_len % tm == 0 and rows % seq_len == 0
        out_shape += [jax.ShapeDtypeStruct((rows // seq_len, KV_W, seq_len), F32)] * 6
        out_specs += [pl.BlockSpec((1, KV_W, tm), lambda i: (i // nt, 0, i % nt))] * 6
    args = [x2, p['mix_norm'].reshape(1, d), w,
            jnp.tile(p['q_norm'], N_HEADS).reshape(1, D_ATT),
            jnp.tile(p['k_sel_norm'], N_KV_HEADS).reshape(1, KV_W),
            jnp.tile(p['k_win_norm'], N_KV_HEADS).reshape(1, KV_W),
            _blockdiag_ones(D_ATT), _blockdiag_ones(KV_W), *tabs]
    return pl.pallas_call(
        functools.partial(_proj_body, transposed=seq_len is not None), out_shape=out_shape,
        grid=(rows // tm,), in_specs=in_specs, out_specs=out_specs,
        compiler_params=_params(1), name="proj")(*args)


def _conv_post(acc, lg, lb, og):
    mu = jnp.mean(acc, axis=-1, keepdims=True)
    xc = acc - mu
    var = jnp.mean(xc * xc, axis=-1, keepdims=True)
    y = xc * lax.rsqrt(var + EPS) * lg + lb
    y = y * _sigmoid(y)
    return _rms(y, og)


def _conv_body(u_ref, cw_ref, cb_ref, lg_ref, lb_ref, og_ref, o_ref, *, n_tiles, tt):
    halo = 32
    lead = halo - (CONV_W - 1)

    def tile(win, t0):
        acc = jnp.zeros((tt, C_CONV), F32) + cb_ref[...]
        for r in range(8):
            sh = win if r == 0 else pltpu.roll(win, tt + halo - r, 0)
            for a in range(halo // 8 + 1):
                w = 8 * a + r - lead
                if 0 <= w < CONV_W:
                    acc = acc + sh[8 * a:8 * a + tt, :] * cw_ref[w:w + 1, :]
        o_ref[0, pl.ds(t0, tt), :] = _conv_post(acc, lg_ref[...], lb_ref[...], og_ref[...])

    tile(jnp.concatenate([jnp.zeros((halo, C_CONV), F32), u_ref[0, 0:tt, :]], axis=0), 0)

    def step(i, carry):
        t0 = pl.multiple_of(i * tt, tt)
        tile(u_ref[0, pl.ds(pl.multiple_of(t0 - halo, halo), tt + halo), :], t0)
        return carry

    lax.fori_loop(1, n_tiles, step, 0)


def _conv_prompt(u, p, tt=256):
    b, t, c = u.shape
    cw = jnp.pad(p['conv_w'], ((0, 1), (0, 0)))
    const = lambda i: (0, 0)
    vec = lambda a: a.reshape(1, c)
    blk = pl.BlockSpec((1, t, c), lambda i: (i, 0, 0))
    return pl.pallas_call(
        functools.partial(_conv_body, n_tiles=t // tt, tt=tt),
        out_shape=jax.ShapeDtypeStruct((b, t, c), F32), grid=(b,),
        in_specs=[blk, pl.BlockSpec((32, c), const)] + [pl.BlockSpec((1, c), const)] * 4,
        out_specs=blk, compiler_params=_params(1), name="conv_prompt")(
            u, cw, vec(p['conv_b']), vec(p['conv_ln_g']), vec(p['conv_ln_b']), vec(p['out_norm_conv']))


def _conv_sample_body(st_ref, u_ref, cw_ref, cb_ref, lg_ref, lb_ref, og_ref, o_ref):
    acc = u_ref[...] * cw_ref[CONV_W - 1:CONV_W, :] + cb_ref[...]
    for w in range(CONV_W - 1):
        acc = acc + st_ref[w] * cw_ref[w:w + 1, :]
    o_ref[...] = _conv_post(acc, lg_ref[...], lb_ref[...], og_ref[...])


def _conv_sample(state, u, p):
    b = u.shape[0]
    c = C_CONV
    cw = jnp.pad(p['conv_w'], ((0, 1), (0, 0)))
    vec = lambda a: a.reshape(1, c)
    return pl.pallas_call(
        _conv_sample_body, out_shape=jax.ShapeDtypeStruct((b, c), F32),
        compiler_params=pltpu.CompilerParams(vmem_limit_bytes=VMEM_LIMIT), name="conv_sample")(
            state.transpose(1, 0, 2), u, cw, vec(p['conv_b']), vec(p['conv_ln_g']),
            vec(p['conv_ln_b']), vec(p['out_norm_conv']))


def _compress_weights(pe, w1, w2):
    assert N_SUB == 2
    w1r = w1.reshape(N_SUB, CMP_STRIDE, HEAD_DIM, CMP_HID).transpose(1, 2, 0, 3)
    eye = jnp.eye(N_KV_HEADS, dtype=w1.dtype)
    wf = jnp.einsum('sdmh,gk->sgdmkh', w1r, eye).reshape(CHUNK_W, N_SUB * N_KV_HEADS * CMP_HID)
    per = jnp.broadcast_to(pe.reshape(N_SUB, CMP_STRIDE, 1, HEAD_DIM),
                           (N_SUB, CMP_STRIDE, N_KV_HEADS, HEAD_DIM)).reshape(N_SUB, CHUNK_W)
    w2f = jnp.einsum('hd,gk->ghkd', w2, eye).reshape(N_KV_HEADS * CMP_HID, KV_W)
    return per, wf.astype(BF16), w2f.astype(BF16)


def _position_pair_weights(per, wf):
    half = N_KV_HEADS * CMP_HID
    pairs = CMP_STRIDE // 2
    pe_p = per.reshape(N_SUB * pairs, 2 * KV_W)
    w_p = wf.reshape(pairs, 2 * KV_W, N_SUB, half).transpose(2, 0, 1, 3).reshape(N_SUB * pairs, 2 * KV_W, half)
    return pe_p, w_p


def _compress_tail(a0, a1, w2_ref):
    n = a0.shape[0]
    h = a0 + pltpu.roll(a1, n - 1, 0)
    t = h * _sigmoid(h)
    tok = _dot(t.astype(BF16), w2_ref[...])
    return jnp.where(lax.broadcasted_iota(jnp.int32, tok.shape, 0) < n - 1, tok, 0.0)


def _cmp_key_post(tok, kng, bd1, c, sa, sb):
    return _rope(_group_rms(tok, kng[...], bd1[...]), c[...], sa[...], sb[...])


def _first_layer(rows_ref, row0, n_chunk, pe_ref, w1_ref):
    half = N_KV_HEADS * CMP_HID
    pairs = CMP_STRIDE // 2
    a0 = jnp.zeros((n_chunk, half), F32)
    a1 = jnp.zeros((n_chunk, half), F32)
    for sp in range(pairs):
        xs = jnp.concatenate(
            [rows_ref[pl.ds(row0 + 2 * sp + e, n_chunk, stride=CMP_STRIDE), :] for e in range(2)], axis=1)
        a0 = a0 + _dot((xs + pe_ref[sp:sp + 1, :]).astype(BF16), w1_ref[sp])
        a1 = a1 + _dot((xs + pe_ref[pairs + sp:pairs + sp + 1, :]).astype(BF16), w1_ref[pairs + sp])
    return a0, a1


def _cmp_prompt_body(xk_ref, xv_ref, pek, w1k, w2k, kng, bd1, c, sa, sb, pev, w1v, w2v, ok_ref, ov_ref):
    n_chunk = ok_ref.shape[1]
    tok_k = _compress_tail(*_first_layer(xk_ref.at[0], 0, n_chunk, pek, w1k), w2k)
    tok_v = _compress_tail(*_first_layer(xv_ref.at[0], 0, n_chunk, pev, w1v), w2v)
    ok_ref[0] = _cmp_key_post(tok_k, kng, bd1, c, sa, sb)
    ov_ref[0] = tok_v.T


def _cmp_weights(p, name):
    per, wf, w2f = _compress_weights(p[f'cmp_{name}_pos'], p[f'cmp_{name}_w1'], p[f'cmp_{name}_w2'])
    return [*_position_pair_weights(per, wf), w2f]


def _cmp_prompt(kc, vc, p):
    b, t, _ = kc.shape
    n_chunk = t // CMP_STRIDE
    args = [*_cmp_weights(p, 'k'), *_cmp_key_consts(p, n_chunk), *_cmp_weights(p, 'v')]
    blk = pl.BlockSpec((1, t, KV_W), lambda i: (i, 0, 0))
    b3 = lambda i: (i, 0, 0)
    const = lambda a: pl.BlockSpec(a.shape, lambda i: (0,) * a.ndim)
    return pl.pallas_call(
        _cmp_prompt_body,
        out_shape=[jax.ShapeDtypeStruct((b, n_chunk, KV_W), F32), jax.ShapeDtypeStruct((b, KV_W, n_chunk), F32)],
        grid=(b,), in_specs=[blk, blk] + [const(a) for a in args],
        out_specs=[pl.BlockSpec((1, n_chunk, KV_W), b3), pl.BlockSpec((1, KV_W, n_chunk), b3)],
        compiler_params=_params(1), name="cmp_prompt")(kc, vc, *args)


def _cmp_key_consts(p, n_chunk):
    cmp_end = np.arange(n_chunk) * CMP_STRIDE + CMP_LEN - 1
    kng = jnp.tile(p['k_cmp_norm'], N_KV_HEADS).reshape(1, KV_W)
    return [kng, _blockdiag_ones(KV_W), *_rope_tables(cmp_end)]


def _cmp_sample_body(pt_ref, cache_hbm, pe_ref, w1_ref, w2_ref, *rest, n_pages, n_batch, is_key):
    key_consts, (o_ref, buf, tok, a_sc, sem) = rest[:-5], rest[-5:]
    b = pl.program_id(0)
    slot = b % 2
    page_len = cache_hbm.shape[2]
    n_pos = n_pages * page_len
    unroll = 8 if n_pages % 8 == 0 else 1

    def page_copy(page, i, sl):
        dst = pl.ds(pl.multiple_of(i * page_len, page_len), page_len)
        return pltpu.make_async_copy(cache_hbm.at[page], buf.at[sl, :, dst], sem.at[sl])

    def gather(seq, sl):
        def issue(i, carry):
            page_copy(pt_ref[seq * n_pages + i], i, sl).start()
            return carry
        lax.fori_loop(0, n_pages, issue, 0, unroll=unroll)

    @pl.when(b == 0)
    def _():
        gather(0, 0)

    @pl.when(b + 1 < n_batch)
    def _():
        gather(b + 1, 1 - slot)

    def wait(i, carry):
        page_copy(0, i, slot).wait()
        return carry
    lax.fori_loop(0, n_pages, wait, 0, unroll=unroll)

    n_blk = 4 if n_pos % (4 * 8 * page_len) == 0 else 1
    pos_blk = n_pos // n_blk
    chunk_blk = pos_blk // CMP_STRIDE
    xw = min(pos_blk, 8 * page_len)
    for j in range(n_blk):
        p0 = j * pos_blk
        for i in range(pos_blk // xw):
            o = p0 + i * xw
            tok[o:o + xw, :] = buf[slot, :, o:o + xw].T
        a0, a1 = _first_layer(tok, p0, chunk_blk, pe_ref, w1_ref)
        a_sc[0, j * chunk_blk:(j + 1) * chunk_blk, :] = a0
        a_sc[1, j * chunk_blk:(j + 1) * chunk_blk, :] = a1
    out = _compress_tail(a_sc[0], a_sc[1], w2_ref)
    o_ref[0] = _cmp_key_post(out, *key_consts) if is_key else out


def _cmp_sample(page_table, cache_t, p, is_key):
    b, n_pages = page_table.shape
    page_len = cache_t.shape[2]
    n_pos = n_pages * page_len
    n_chunk = n_pos // CMP_STRIDE
    name = 'k' if is_key else 'v'
    args = _cmp_weights(p, name) + (_cmp_key_consts(p, n_chunk) if is_key else [])
    const = lambda a: pl.BlockSpec(a.shape, lambda i, pt: (0,) * a.ndim)
    grid_spec = pltpu.PrefetchScalarGridSpec(
        num_scalar_prefetch=1, grid=(b,),
        in_specs=[pl.BlockSpec(memory_space=pl.ANY)] + [const(a) for a in args],
        out_specs=pl.BlockSpec((1, n_chunk, KV_W), lambda i, pt: (i, 0, 0)),
        scratch_shapes=[pltpu.VMEM((2, KV_W, n_pos), F32), pltpu.VMEM((n_pos, KV_W), F32),
                        pltpu.VMEM((N_SUB, n_chunk, N_KV_HEADS * CMP_HID), F32),
                        pltpu.SemaphoreType.DMA((2,))])
    return pl.pallas_call(
        functools.partial(_cmp_sample_body, n_pages=n_pages, n_batch=b, is_key=is_key),
        out_shape=jax.ShapeDtypeStruct((b, n_chunk, KV_W), F32), grid_spec=grid_spec,
        compiler_params=_params(1), name=f"cmp_sample_{name}")(page_table.reshape(-1), cache_t, *args)


def _cmp_to_sel(n_chunk, n_cmp, n_sel, width):
    cs = np.arange(n_chunk)[:, None] * CMP_STRIDE
    ss = np.arange(width)[None, :] * SEL_BLK
    ov = np.clip(np.minimum(cs + CMP_LEN, ss + SEL_BLK) - np.maximum(cs, ss), 0, None) / CMP_LEN
    ov = ov * (np.arange(n_chunk)[:, None] < n_cmp) * (np.arange(width)[None, :] < n_sel)
    return jnp.asarray(ov, dtype=BF16)


def _attn_qlanes_body(q_ref, gt_ref, kc_ref, vct_ref, ks_ref, vst_ref, kw_ref, vwt_ref, ovt_ref, o_ref,
                      qt_sc, sel_sc, oct_sc, m_sc, l_sc, acc_sc, s_sc, p_sc, bias_sc,
                      *, t, tq, kc_len, n_cmp, n_sel):
    off = pl.program_id(1) * tq
    cols = GRP * tq
    lane = lax.broadcasted_iota(jnp.int32, (tq, LANES), 1)
    qpos = off + lax.broadcasted_iota(jnp.int32, (1, tq), 1)
    q = q_ref[0] * (SCALE * LOG2E)
    for h in range(N_HEADS):
        g, r = divmod(h, GRP)
        qt_sc[g, :, r * tq:(r + 1) * tq] = _head_to_group_lanes(q, h, lane).T.astype(BF16)

    def softmax_update(g, n_keys):
        for r in range(GRP):
            cs = slice(r * tq, (r + 1) * tq)

            def scores():
                return s_sc[:n_keys, cs] + bias_sc[:n_keys, :]

            m_old = m_sc[g, :, cs]
            m_new = jnp.maximum(m_old, jnp.max(scores(), axis=0, keepdims=True))
            m_sc[g, :, cs] = m_new
            p = jnp.exp2(scores() - m_new)
            alpha = jnp.exp2(m_old - m_new)
            l_sc[g, :, cs] = alpha * l_sc[g, :, cs] + jnp.sum(p, axis=0, keepdims=True)
            acc_sc[g, :, cs] = alpha * acc_sc[g, :, cs]
            p_sc[:n_keys, cs] = p.astype(BF16)

    def reset_softmax():
        m_sc[...] = jnp.full(m_sc.shape, NEG_INF, F32)
        l_sc[...] = jnp.zeros(l_sc.shape, F32)
        acc_sc[...] = jnp.zeros(acc_sc.shape, F32)

    n_chunk = kc_ref.shape[1]
    kcm = kc_ref[0].astype(BF16)
    vct = vct_ref[0].astype(BF16)
    nrow = lax.broadcasted_iota(jnp.int32, (n_chunk, tq), 0)
    mask_1 = jnp.logical_and(nrow * CMP_STRIDE + (CMP_LEN - 1) <= qpos, nrow < n_cmp)
    mask_c = jnp.concatenate([mask_1.astype(F32)] * GRP, axis=1) > 0.5
    psum = []
    for g in range(N_KV_HEADS):
        sc = jnp.where(mask_c, _dot(kcm, qt_sc[g]), NEG_INF)
        e = jnp.where(mask_c, jnp.exp2(sc - jnp.max(sc, axis=0, keepdims=True)), 0.0)
        pc = e / jnp.maximum(jnp.sum(e, axis=0, keepdims=True), 1e-30)
        oct_sc[g] = _dot(vct, pc.astype(BF16))
        psum.append(sum(pc[:, r * tq:(r + 1) * tq] for r in range(GRP)))

    n_sel_pad = -(-n_sel // 8) * 8
    jrow = lax.broadcasted_iota(jnp.int32, (n_sel_pad, tq), 0)
    cur = qpos // SEL_BLK
    forced = jnp.logical_or(jrow == 0, jnp.logical_or(jrow == cur, jrow == cur - 1))
    bonus = jnp.where(forced, FORCE_BONUS, 0.0)
    k_eff = min(TOPK, n_sel)
    ovt = ovt_ref[...]
    for g in range(N_KV_HEADS):
        hi = psum[g].astype(BF16)
        lo = (psum[g] - hi.astype(F32)).astype(BF16)
        imp = _dot(ovt, hi) + _dot(ovt, lo)
        score = jnp.where(jrow <= cur, imp + bonus, NEG_INF)
        rank = jnp.zeros((n_sel_pad, tq), F32)
        for j in range(n_sel):
            row = score[j:j + 1, :]
            tie = jnp.where(jrow > j, 1.0, 0.0)
            rank = rank + jnp.where(row > score, 1.0, jnp.where(row == score, tie, 0.0))
        sel = jnp.where(rank < k_eff, 1.0, 0.0)
        sel_sc[g] = jnp.concatenate([sel, jnp.zeros((LANES - n_sel_pad, tq), F32)], axis=0).astype(BF16)

    reset_softmax()
    key_row = lax.broadcasted_iota(jnp.int32, (kc_len, LANES), 0)
    blk_col = lax.broadcasted_iota(jnp.int32, (kc_len, LANES), 1)
    for ci in range(t // kc_len):
        @pl.when(ci * kc_len < off + tq)
        def _(ci=ci):
            base = ci * kc_len
            expand = jnp.where(blk_col == (base + key_row) // SEL_BLK, 1.0, 0.0).astype(BF16)
            causal = base + lax.broadcasted_iota(jnp.int32, (kc_len, tq), 0) <= qpos
            k = ks_ref[0, base:base + kc_len, :].astype(BF16)
            vt = vst_ref[0, :, base:base + kc_len].astype(BF16)
            for g in range(N_KV_HEADS):
                mask = jnp.logical_and(_dot(expand, sel_sc[g]) > 0.5, causal)
                bias_sc[:kc_len, :] = jnp.where(mask, 0.0, -jnp.inf)
                s_sc[:kc_len, :] = _dot(k, qt_sc[g])
                softmax_update(g, kc_len)
                acc_sc[g] = acc_sc[g] + _dot(vt, p_sc[:kc_len, :])
    o_sel = [acc_sc[g] / jnp.maximum(l_sc[g], 1e-30) for g in range(N_KV_HEADS)]

    wl = WINDOW + tq
    ws = pl.multiple_of(jnp.maximum(off - WINDOW, 0), LANES)
    kw = kw_ref[0, pl.ds(ws, wl), :].astype(BF16)
    vwt = vwt_ref[0, :, pl.ds(ws, wl)].astype(BF16)
    dist = qpos - (ws + lax.broadcasted_iota(jnp.int32, (wl, tq), 0))
    bias_sc[:wl, :] = jnp.where(jnp.logical_and(dist >= 0, dist < WINDOW), 0.0, -jnp.inf)
    reset_softmax()
    gtt = gt_ref[0].T
    outs = []
    for g in range(N_KV_HEADS):
        s_sc[:wl, :] = _dot(kw, qt_sc[g])
        softmax_update(g, wl)
        o_w = _dot(vwt, p_sc[:wl, :]) / jnp.maximum(l_sc[g], 1e-30)
        o_s = o_sel[g]
        o_c = oct_sc[g]
        for r in range(GRP):
            h = g * GRP + r
            cs = slice(r * tq, (r + 1) * tq)
            o_h = (gtt[3 * h:3 * h + 1, :] * o_c[:, cs] + gtt[3 * h + 1:3 * h + 2, :] * o_s[:, cs]
                   + gtt[3 * h + 2:3 * h + 3, :] * o_w[:, cs])
            outs.append(o_h.T)
    o_ref[0] = jnp.concatenate(
        [_pair_tile(outs[2 * p], outs[2 * p + 1], 2 * p, lane) for p in range(N_HEADS // 2)], axis=-1)


def _attn_qlanes(q, gt, kcmp, vcmp_t, ks, vs_t, kw, vw_t, tq=128, kc_len=512):
    b, t, _ = q.shape
    n_chunk = kcmp.shape[1]
    n_cmp = n_chunk - N_SUB + 1
    n_sel = -(-t // SEL_BLK)
    assert t % kc_len == 0 and t >= WINDOW + tq and n_sel <= LANES and n_chunk % LANES == 0
    assert tq % LANES == 0
    n_sel_pad = -(-n_sel // 8) * 8
    ov_t = _cmp_to_sel(n_chunk, n_cmp, n_sel, n_sel_pad).T
    qblk = lambda i, j: (i, j, 0)
    full = lambda i, j: (i, 0, 0)
    tok = pl.BlockSpec((1, t, KV_W), full)
    tr = pl.BlockSpec((1, KV_W, t), full)
    cols = GRP * tq
    n_keys = max(kc_len, WINDOW + tq)
    body = functools.partial(_attn_qlanes_body, t=t, tq=tq, kc_len=kc_len, n_cmp=n_cmp, n_sel=n_sel)
    return pl.pallas_call(
        body, out_shape=jax.ShapeDtypeStruct((b, t, D_ATT), F32), grid=(b, t // tq),
        in_specs=[pl.BlockSpec((1, tq, D_ATT), qblk), pl.BlockSpec((1, tq, LANES), qblk),
                  pl.BlockSpec((1, n_chunk, KV_W), full), pl.BlockSpec((1, KV_W, n_chunk), full),
                  tok, tr, tok, tr, pl.BlockSpec(ov_t.shape, lambda i, j: (0, 0))],
        out_specs=pl.BlockSpec((1, tq, D_ATT), qblk),
        scratch_shapes=[pltpu.VMEM((N_KV_HEADS, KV_W, cols), BF16), pltpu.VMEM((N_KV_HEADS, LANES, tq), BF16),
                        pltpu.VMEM((N_KV_HEADS, KV_W, cols), F32), pltpu.VMEM((N_KV_HEADS, 1, cols), F32),
                        pltpu.VMEM((N_KV_HEADS, 1, cols), F32), pltpu.VMEM((N_KV_HEADS, KV_W, cols), F32),
                        pltpu.VMEM((n_keys, cols), F32), pltpu.VMEM((n_keys, cols), BF16),
                        pltpu.VMEM((n_keys, tq), F32)],
        compiler_params=_params(2), name="attn_prompt")(q, gt, kcmp, vcmp_t, ks, vs_t, kw, vw_t, ov_t)


def _sample_heads(q_row):
    lane = lax.broadcasted_iota(jnp.int32, (1, LANES), 1)
    return jnp.concatenate([_head_to_group_lanes(q_row, h, lane) for h in range(N_HEADS)], axis=0)


def _attn_s1_body(q_ref, kc_ref, vc_ref, ov_ref, oc_ref, idx_ref, *, n_cmp, n_sel, q_pos):
    qh = _sample_heads(q_ref[0])
    n_chunk = kc_ref.shape[1]
    ncol = lax.broadcasted_iota(jnp.int32, (N_HEADS, n_chunk), 1)
    mask = jnp.logical_and(ncol * CMP_STRIDE + (CMP_LEN - 1) <= q_pos, ncol < n_cmp)
    pc = _masked_softmax(_dot_t(qh.astype(BF16), kc_ref[0].astype(BF16)) * SCALE, mask)
    oc_ref[0] = _dot(pc.astype(BF16), vc_ref[0].astype(BF16))
    rows = [jnp.sum(pc[g * GRP:(g + 1) * GRP, :], axis=0, keepdims=True) for g in range(N_KV_HEADS)]
    psum = jnp.concatenate(rows + [jnp.zeros((N_HEADS - N_KV_HEADS, n_chunk), F32)], axis=0)
    imp = _split_dot(psum, ov_ref[...])
    width = imp.shape[1]
    jl = lax.broadcasted_iota(jnp.int32, (N_HEADS, width), 1)
    cur = q_pos // SEL_BLK
    forced = jnp.logical_or(jl == 0, jnp.logical_or(jl == cur, jl == cur - 1))
    score = jnp.where(jl <= cur, imp + jnp.where(forced, FORCE_BONUS, 0.0), NEG_INF)
    ii = lax.broadcasted_iota(jnp.int32, (width, width), 0)
    jj = lax.broadcasted_iota(jnp.int32, (width, width), 1)
    slot = lax.broadcasted_iota(jnp.int32, (width, LANES), 1).astype(F32)
    blk = lax.broadcasted_iota(jnp.int32, (width, LANES), 0).astype(F32)
    tie = jnp.where(jj < ii, 1.0, 0.0)
    out_rows = []
    for g in range(N_KV_HEADS):
        row = jnp.broadcast_to(score[g:g + 1, :], (width, width))
        col = jnp.sum(jnp.where(ii == jj, row, 0.0), axis=1, keepdims=True)
        beats = jnp.where(row > col, 1.0, jnp.where(row == col, tie, 0.0))
        rank = jnp.sum(beats, axis=1, keepdims=True)
        out_rows.append(jnp.sum(jnp.where(rank == slot, blk, 0.0), axis=0, keepdims=True))
    out_rows.append(jnp.zeros((N_HEADS - N_KV_HEADS, LANES), F32))
    idx_ref[0] = jnp.concatenate(out_rows, axis=0).astype(jnp.int32)


def _attn_s1(q, kcmp, vcmp, q_pos, n_sel):
    b = q.shape[0]
    n_chunk = kcmp.shape[1]
    n_cmp = n_chunk - N_SUB + 1
    width = -(-n_sel // LANES) * LANES
    ov = _cmp_to_sel(n_chunk, n_cmp, n_sel, width)
    blk3 = lambda i: (i, 0, 0)
    cm = pl.BlockSpec((1, n_chunk, KV_W), blk3)
    out = pl.BlockSpec((1, N_HEADS, LANES), blk3)
    return pl.pallas_call(
        functools.partial(_attn_s1_body, n_cmp=n_cmp, n_sel=n_sel, q_pos=q_pos),
        out_shape=[jax.ShapeDtypeStruct((b, N_HEADS, LANES), F32),
                   jax.ShapeDtypeStruct((b, N_HEADS, LANES), jnp.int32)],
        grid=(b,),
        in_specs=[pl.BlockSpec((1, 1, D_ATT), blk3), cm, cm, pl.BlockSpec(ov.shape, lambda i: (0, 0))],
        out_specs=[out, out], compiler_params=_params(1), name="attn_sample_select")(
            q.reshape(b, 1, D_ATT), kcmp, vcmp, ov)


def _extra_key_attention(qh, k_t, v_t, mask, k_new, v_new):
    qf = qh.astype(F32)
    s = jnp.where(mask, _dot(qh, k_t.astype(BF16)) * SCALE, NEG_INF)
    s_new = jnp.sum(qf * k_new, axis=-1, keepdims=True) * SCALE
    m = jnp.maximum(jnp.max(s, axis=-1, keepdims=True), s_new)
    e = jnp.where(mask, jnp.exp(s - m), 0.0)
    e_new = jnp.exp(s_new - m)
    denom = jnp.maximum(jnp.sum(e, axis=-1, keepdims=True) + e_new, 1e-30)
    return (_dot_t(e.astype(BF16), v_t.astype(BF16)) + e_new * v_new) / denom


def _attn_s2_body(idx_ref, pt_ref, q_ref, gt_ref, oc_ref, ksn_ref, vsn_ref, kwn_ref, vwn_ref,
                  kws_ref, vws_ref, kcache, vcache, o_ref, kbuf, vbuf, sem, *, n_pages, n_cache_blk, q_pos):
    b = pl.program_id(0)
    page_len = kcache.shape[2]
    per_page = page_len // SEL_BLK

    def copies(g, k, blk):
        blk = jnp.minimum(blk, n_cache_blk - 1)
        page = pt_ref[b * n_pages + blk // per_page]
        dst = pl.ds(k * page_len, page_len)
        return (pltpu.make_async_copy(kcache.at[page], kbuf.at[g, :, dst], sem.at[0]),
                pltpu.make_async_copy(vcache.at[page], vbuf.at[g, :, dst], sem.at[1]))

    blks = [[idx_ref[(b * N_KV_HEADS + g) * TOPK + k] for k in range(TOPK)] for g in range(N_KV_HEADS)]
    for g in range(N_KV_HEADS):
        for k in range(TOPK):
            for cp in copies(g, k, blks[g][k]):
                cp.start()
    for g in range(N_KV_HEADS):
        for k in range(TOPK):
            for cp in copies(g, k, blks[g][k]):
                cp.wait()

    qh = _sample_heads(q_ref[0]).astype(BF16)
    n_keys = TOPK * page_len
    row_grp = lax.broadcasted_iota(jnp.int32, (N_HEADS, n_keys), 0) // GRP
    key = lax.broadcasted_iota(jnp.int32, (N_HEADS, n_keys), 1)
    slot = key // page_len
    blk_in_page = (key % page_len) // SEL_BLK
    keep = jnp.zeros((N_HEADS, n_keys), F32)
    for g in range(N_KV_HEADS):
        for k in range(TOPK):
            blk = blks[g][k]
            in_cache = jnp.where(blk < n_cache_blk, 1.0, 0.0)
            hit = jnp.where(blk_in_page == blk % per_page, in_cache, 0.0)
            keep = jnp.where(jnp.logical_and(row_grp == g, slot == k), hit, keep)
    mask_s = keep > 0.5
    o_sel = [_extra_key_attention(qh, kbuf[g], vbuf[g], mask_s, ksn_ref[0], vsn_ref[0])
             for g in range(N_KV_HEADS)]
    out_grp = lax.broadcasted_iota(jnp.int32, (N_HEADS, LANES), 0) // GRP
    o_s = jnp.where(out_grp == 0, o_sel[0], o_sel[1])

    n_win = kws_ref.shape[2]
    kpos = (q_pos - n_win) + lax.broadcasted_iota(jnp.int32, (N_HEADS, n_win), 1)
    dist = q_pos - kpos
    mask_w = jnp.logical_and(dist >= 0, dist < WINDOW)
    o_w = _extra_key_attention(qh, kws_ref[0], vws_ref[0], mask_w, kwn_ref[0], vwn_ref[0])

    lane8 = lax.broadcasted_iota(jnp.int32, (N_HEADS, LANES), 1)
    head8 = lax.broadcasted_iota(jnp.int32, (N_HEADS, LANES), 0)
    gt = jnp.broadcast_to(gt_ref[0], (N_HEADS, LANES))
    gate = [jnp.sum(jnp.where(lane8 == 3 * head8 + j, gt, 0.0), axis=-1, keepdims=True) for j in range(3)]
    o = gate[0] * oc_ref[0] + gate[1] * o_s + gate[2] * o_w
    lane = lax.broadcasted_iota(jnp.int32, (1, LANES), 1)
    o_ref[0] = jnp.concatenate(
        [_pair_tile(o[2 * p:2 * p + 1, :], o[2 * p + 1:2 * p + 2, :], 2 * p, lane)
         for p in range(N_HEADS // 2)], axis=-1)


def _attn_s2(idx, page_table, q, gt, oc, ks_new, vs_new, kw_new, vw_new, kw_state_t, vw_state_t,
             cache_k_t, cache_v_t, q_pos):
    b, n_pages = page_table.shape
    page_len = cache_k_t.shape[2]
    n_win = kw_state_t.shape[2]
    assert n_win <= WINDOW
    blk3 = lambda i, *_: (i, 0, 0)
    row = lambda w: pl.BlockSpec((1, 1, w), blk3)
    grid_spec = pltpu.PrefetchScalarGridSpec(
        num_scalar_prefetch=2, grid=(b,),
        in_specs=[row(D_ATT), row(LANES), pl.BlockSpec((1, N_HEADS, LANES), blk3),
                  row(KV_W), row(KV_W), row(KV_W), row(KV_W),
                  pl.BlockSpec((1, KV_W, n_win), blk3), pl.BlockSpec((1, KV_W, n_win), blk3),
                  pl.BlockSpec(memory_space=pl.ANY), pl.BlockSpec(memory_space=pl.ANY)],
        out_specs=row(D_ATT),
        scratch_shapes=[pltpu.VMEM((N_KV_HEADS, KV_W, TOPK * page_len), F32),
                        pltpu.VMEM((N_KV_HEADS, KV_W, TOPK * page_len), F32),
                        pltpu.SemaphoreType.DMA((2,))])
    r3 = lambda a: a.reshape(b, 1, a.shape[-1])
    body = functools.partial(_attn_s2_body, n_pages=n_pages, n_cache_blk=n_pages * (page_len // SEL_BLK),
                             q_pos=q_pos)
    return pl.pallas_call(
        body, out_shape=jax.ShapeDtypeStruct((b, 1, D_ATT), F32), grid_spec=grid_spec,
        compiler_params=_params(1), name="attn_sample")(
            idx.reshape(-1), page_table.reshape(-1), r3(q), r3(gt), oc, r3(ks_new), r3(vs_new),
            r3(kw_new), r3(vw_new), kw_state_t, vw_state_t, cache_k_t, cache_v_t).reshape(b, D_ATT)


def _prompt_layer(x, p, tm=512, tm_proj=1024):
    b, t, d = x.shape
    x2 = _ffn(x.reshape(b * t, d), p['ffn1_norm'], p['ffn1_w_in'], p['ffn1_w_out'], None, tm)
    u, q, gt, kc, vc, ks, kw, kc_t, vc_t, ks_t, vs_t, kw_t, vw_t = _proj(
        x2, p, _rope_tables(np.arange(t)), tm_proj, seq_len=t)
    r3 = lambda a: a.reshape(b, t, a.shape[-1])
    u3 = r3(u)
    cn = _conv_prompt(u3, p)
    kcmp, vcmp_t = _cmp_prompt(r3(kc), r3(vc), p)
    o = _attn_qlanes(r3(q), r3(gt), kcmp, vcmp_t, r3(ks), vs_t, r3(kw), vw_t)
    y = _ffn(x2, p['ffn2_norm'], p['ffn2_w_in'], p['ffn2_w_out'], p['final_norm'], tm,
             mix=(cn.reshape(b * t, C_CONV), o.reshape(b * t, D_ATT), p['out_norm_attn'], p['w_out']))
    r4 = lambda a: a.reshape(b, N_KV_HEADS, HEAD_DIM, a.shape[-1]).transpose(0, 3, 1, 2)
    keep = min(WINDOW, t)
    state = (r4(kc_t), r4(vc_t), r4(ks_t), r4(vs_t), r4(kw_t[:, :, t - keep:]), r4(vw_t[:, :, t - keep:]),
             u3[:, t - (CONV_W - 1):])
    return y.reshape(b, t, d), state


def _sample_layer(x, p, cache_k_cmp, cache_v_cmp, cache_k_sel, cache_v_sel, kw_state, vw_state,
                  conv_state, page_table):
    b, t, d = x.shape
    assert t == 1
    past_len = page_table.shape[1] * cache_k_cmp.shape[1]
    x2 = _ffn(x.reshape(b, d), p['ffn1_norm'], p['ffn1_w_in'], p['ffn1_w_out'], None, b)
    tabs = _rope_tables(np.full((b,), past_len))
    u, q, gt, kc, vc, ks, vs, kw, vw = _proj(x2, p, tabs, b)
    cn = _conv_sample(conv_state, u, p)
    keys_on_lanes = lambda a: a.transpose(0, 2, 3, 1).reshape(a.shape[0], KV_W, a.shape[1])
    kcmp = _cmp_sample(page_table, keys_on_lanes(cache_k_cmp), p, True)
    vcmp = _cmp_sample(page_table, keys_on_lanes(cache_v_cmp), p, False)
    n_sel = -(-(past_len + 1) // SEL_BLK)
    oc, idx = _attn_s1(q, kcmp, vcmp, past_len, n_sel)
    n_win = kw_state.shape[1]
    o = _attn_s2(idx[:, :N_KV_HEADS, :TOPK], page_table, q, gt, oc, ks, vs, kw, vw,
                 keys_on_lanes(kw_state), keys_on_lanes(vw_state),
                 keys_on_lanes(cache_k_sel), keys_on_lanes(cache_v_sel), past_len)
    y = _ffn(x2, p['ffn2_norm'], p['ffn2_w_in'], p['ffn2_w_out'], p['final_norm'], b,
             mix=(cn, o, p['out_norm_attn'], p['w_out']))
    r4 = lambda a: a.reshape(b, 1, N_KV_HEADS, HEAD_DIM)
    keep = min(WINDOW, n_win + 1)
    new_kw = jnp.concatenate([kw_state, r4(kw)], axis=1)[:, n_win + 1 - keep:]
    new_vw = jnp.concatenate([vw_state, r4(vw)], axis=1)[:, n_win + 1 - keep:]
    new_conv = jnp.concatenate([conv_state, u[:, None, :]], axis=1)[:, 1:]
    return y.reshape(b, 1, d), (r4(kc), r4(vc), r4(ks), r4(vs), new_kw, new_vw, new_conv)


_PARAM_NAMES = ('ffn1_norm', 'ffn1_w_in', 'ffn1_w_out', 'mix_norm', 'w_in', 'conv_w', 'conv_b', 'conv_ln_g',
                'conv_ln_b', 'q_norm', 'k_cmp_norm', 'k_sel_norm', 'k_win_norm', 'cmp_k_pos', 'cmp_k_w1',
                'cmp_k_w2', 'cmp_v_pos', 'cmp_v_w1', 'cmp_v_w2', 'out_norm_conv', 'out_norm_attn', 'w_out',
                'ffn2_norm', 'ffn2_w_in', 'ffn2_w_out', 'final_norm')


def kernel(x_prompt, x_sample, cache_k_cmp, cache_v_cmp, cache_k_sel, cache_v_sel, state_k_win, state_v_win,
           state_conv, page_table, ffn1_norm, ffn1_w_in, ffn1_w_out, mix_norm, w_in, conv_w, conv_b, conv_ln_g,
           conv_ln_b, q_norm, k_cmp_norm, k_sel_norm, k_win_norm, cmp_k_pos, cmp_k_w1, cmp_k_w2, cmp_v_pos,
           cmp_v_w1, cmp_v_w2, out_norm_conv, out_norm_attn, w_out, ffn2_norm, ffn2_w_in, ffn2_w_out, final_norm):
    stacked = dict(zip(_PARAM_NAMES, (
        ffn1_norm, ffn1_w_in, ffn1_w_out, mix_norm, w_in, conv_w, conv_b, conv_ln_g, conv_ln_b, q_norm,
        k_cmp_norm, k_sel_norm, k_win_norm, cmp_k_pos, cmp_k_w1, cmp_k_w2, cmp_v_pos, cmp_v_w1, cmp_v_w2,
        out_norm_conv, out_norm_attn, w_out, ffn2_norm, ffn2_w_in, ffn2_w_out, final_norm)))
    depth = ffn1_norm.shape[0]
    yp, ys = x_prompt, x_sample
    new_p, new_s = [], []
    for l in range(depth):
        p = {k: v[l] for k, v in stacked.items()}
        yp, st_p = _prompt_layer(yp, p)
        new_p.append(st_p)
        ys, st_s = _sample_layer(ys, p, cache_k_cmp[l], cache_v_cmp[l], cache_k_sel[l], cache_v_sel[l],
                                 state_k_win[l], state_v_win[l], state_conv[l], page_table)
        new_s.append(st_s)
    outs_p = [jnp.stack(tup) for tup in zip(*new_p)]
    outs_s = [jnp.stack(tup) for tup in zip(*new_s)]
    return (yp, ys, *outs_p, *outs_s)
```

```python
import functools

import numpy as np
import jax
import jax.numpy as jnp
from jax import lax
from jax.experimental import pallas as pl
from jax.experimental.pallas import tpu as pltpu

F32 = jnp.float32
BF16 = jnp.bfloat16

HEAD_DIM = 64
N_HEADS = 8
N_KV_HEADS = 2
GRP = N_HEADS // N_KV_HEADS
KV_W = N_KV_HEADS * HEAD_DIM
C_CONV = 512
D_ATT = 512
CONV_W = 31
CMP_LEN = 32
CMP_STRIDE = 16
N_SUB = CMP_LEN // CMP_STRIDE
CMP_HID = 2 * HEAD_DIM
SEL_BLK = 64
TOPK = 16
WINDOW = 512
ROT_DIM = HEAD_DIM // 4
ROT_HALF = ROT_DIM // 2
ROPE_THETA = 500000.0
NEG_INF = -1e30
MASK_PENALTY = 2.0 ** 100
FORCE_BONUS = 1e4
EPS = 1e-6
SCALE = HEAD_DIM ** -0.5
LOG2E = 1.4426950408889634
LANES = 128
CHUNK_W = CMP_STRIDE * KV_W
VMEM_LIMIT = 56 * 1024 * 1024


def _params(n_grid_dims):
    return pltpu.CompilerParams(
        dimension_semantics=("arbitrary",) * n_grid_dims, vmem_limit_bytes=VMEM_LIMIT)


def _dot(a, b):
    return jnp.dot(a, b, preferred_element_type=F32)


def _dot_t(a, b):
    return lax.dot_general(a, b, (((1,), (1,)), ((), ())), preferred_element_type=F32)


def _split_dot(x, m_bf16):
    hi = x.astype(BF16)
    lo = (x - hi.astype(F32)).astype(BF16)
    return _dot(hi, m_bf16) + _dot(lo, m_bf16)


def _sigmoid(x):
    return 1.0 / (1.0 + jnp.exp(-x))


def _rms(x, g):
    return x * lax.rsqrt(jnp.mean(x * x, axis=-1, keepdims=True) + EPS) * g


def _group_rms(x, g_tiled, ones_blockdiag):
    ss = _split_dot(x * x, ones_blockdiag)
    return x * lax.rsqrt(ss * (1.0 / HEAD_DIM) + EPS) * g_tiled


def _rope(x, c, sa, sb):
    n = x.shape[-1]
    return x * c + pltpu.roll(x, n - ROT_HALF, 1) * sa + pltpu.roll(x, ROT_HALF, 1) * sb


def _tile_lanes(t, reps):
    return jnp.concatenate([t] * reps, axis=-1) if reps > 1 else t


def _masked_softmax(s, mask):
    s = jnp.where(mask, s, NEG_INF)
    m = jnp.max(s, axis=-1, keepdims=True)
    e = jnp.where(mask, jnp.exp(s - m), 0.0)
    return e / jnp.maximum(jnp.sum(e, axis=-1, keepdims=True), 1e-30)


def _head_to_group_lanes(q, h, lane):
    g, p = h // GRP, h // 2
    qp = q[:, LANES * p:LANES * (p + 1)]
    src = qp if (h % 2) == g else pltpu.roll(qp, HEAD_DIM, 1)
    keep = (lane < HEAD_DIM) if g == 0 else (lane >= HEAD_DIM)
    return jnp.where(keep, src, 0.0)


def _pair_tile(o_even, o_odd, h_even, lane):
    g = h_even // GRP
    a = o_even if g == 0 else pltpu.roll(o_even, HEAD_DIM, 1)
    b = pltpu.roll(o_odd, HEAD_DIM, 1) if g == 0 else o_odd
    return jnp.where(lane < HEAD_DIM, a, b)


def _ffn_body(*refs, n_chunk, tf, ff, final, mixed):
    refs = list(refs)
    x_ref = refs.pop(0)
    x = x_ref[...]
    if mixed:
        c_ref, a_ref, ag_ref, wc_ref, wa_ref = refs[:5]
        refs = refs[5:]
        x = (x + _dot(c_ref[...].astype(BF16), wc_ref[...])
             + _dot(_rms(a_ref[...], ag_ref[...]).astype(BF16), wa_ref[...]))
    g_ref, wi_ref, wo_ref = refs[:3]
    fg_ref = refs[3] if final else None
    o_ref = refs[-1]
    xn = _rms(x, g_ref[...]).astype(BF16)
    acc = jnp.zeros(x.shape, F32)
    for c in range(n_chunk):
        a = _dot(xn, wi_ref[:, c * tf:(c + 1) * tf])
        b = _dot(xn, wi_ref[:, ff + c * tf:ff + (c + 1) * tf])
        h = (a * _sigmoid(a)) * b
        acc = acc + _dot(h.astype(BF16), wo_ref[c * tf:(c + 1) * tf, :])
    y = x + 0.5 * acc
    if final:
        y = _rms(y, fg_ref[...])
    o_ref[...] = y


def _ffn(x2, norm_g, w_in, w_out, final_g, tm, mix=None):
    rows, d = x2.shape
    ff = w_out.shape[0]
    tf = 256
    assert rows % tm == 0 and ff % tf == 0 and w_in.shape == (d, 2 * ff)
    const = lambda i: (0, 0)
    row = lambda i: (i, 0)
    single = pl.Buffered(1)
    in_specs = [pl.BlockSpec((tm, d), row)]
    args = [x2]
    if mix is not None:
        cn, att, att_g, w_mix = mix
        w_mix = w_mix.astype(BF16)
        in_specs += [pl.BlockSpec((tm, C_CONV), row), pl.BlockSpec((tm, D_ATT), row),
                     pl.BlockSpec((1, D_ATT), const), pl.BlockSpec((C_CONV, d), const),
                     pl.BlockSpec((D_ATT, d), const)]
        args += [cn, att, att_g.reshape(1, D_ATT), w_mix[:C_CONV], w_mix[C_CONV:]]
    in_specs += [
        pl.BlockSpec((1, d), const),
        pl.BlockSpec((d, 2 * ff), const, pipeline_mode=single),
        pl.BlockSpec((ff, d), const, pipeline_mode=single),
    ]
    args += [norm_g.reshape(1, d), w_in.astype(BF16), w_out.astype(BF16)]
    if final_g is not None:
        in_specs.append(pl.BlockSpec((1, d), const))
        args.append(final_g.reshape(1, d))
    body = functools.partial(_ffn_body, n_chunk=ff // tf, tf=tf, ff=ff, final=final_g is not None,
                             mixed=mix is not None)
    return pl.pallas_call(
        body, out_shape=jax.ShapeDtypeStruct((rows, d), F32), grid=(rows // tm,),
        in_specs=in_specs, out_specs=pl.BlockSpec((tm, d), lambda i: (i, 0)),
        compiler_params=_params(1), name="ffn")(*args)


_SEG_GLU = 2 * C_CONV
_SEG_Q = _SEG_GLU + D_ATT
N_IN = _SEG_Q + 6 * KV_W + 3 * N_HEADS
N_IN_PAD = _SEG_Q + 7 * KV_W


def _proj_body(x_ref, g_ref, w_ref, qg_ref, ksg_ref, kwg_ref, bd4_ref, bd1_ref, c_ref, sa_ref, sb_ref,
               u_ref, q_ref, gt_ref, kc_ref, vc_ref, *kv_refs, transposed):
    xn = _rms(x_ref[...], g_ref[...]).astype(BF16)

    def seg(lo, width):
        return _dot(xn, w_ref[:, lo:lo + width])

    c, sa, sb = c_ref[...], sa_ref[...], sb_ref[...]
    reps = D_ATT // LANES
    glu = seg(0, _SEG_GLU)
    u_ref[...] = glu[:, :C_CONV] * _sigmoid(glu[:, C_CONV:])
    q = _group_rms(seg(_SEG_GLU, D_ATT), qg_ref[...], bd4_ref[...])
    q_ref[...] = _rope(q, _tile_lanes(c, reps), _tile_lanes(sa, reps), _tile_lanes(sb, reps))
    base = _SEG_Q
    kc = seg(base, KV_W)
    vc = seg(base + KV_W, KV_W)
    ks = _rope(_group_rms(seg(base + 2 * KV_W, KV_W), ksg_ref[...], bd1_ref[...]), c, sa, sb)
    vs = seg(base + 3 * KV_W, KV_W)
    kw = _rope(_group_rms(seg(base + 4 * KV_W, KV_W), kwg_ref[...], bd1_ref[...]), c, sa, sb)
    vw = seg(base + 5 * KV_W, KV_W)
    gt_ref[...] = _sigmoid(seg(base + 6 * KV_W, KV_W))
    kc_ref[...] = kc
    vc_ref[...] = vc
    if transposed:
        kv_refs[0][...] = ks
        kv_refs[1][...] = kw
        for ref, val in zip(kv_refs[2:], (kc, vc, ks, vs, kw, vw)):
            ref[0] = val.T
    else:
        for ref, val in zip(kv_refs, (ks, vs, kw, vw)):
            ref[...] = val


def _blockdiag_ones(width):
    idx = np.arange(width) // HEAD_DIM
    return jnp.asarray(idx[:, None] == idx[None, :], dtype=BF16)


def _rope_tables(pos):
    pos = np.asarray(pos, dtype=np.float64)
    inv = ROPE_THETA ** (-(np.arange(ROT_HALF, dtype=np.float64) * 2.0 / ROT_DIM))
    ang = pos[:, None] * inv[None, :]
    cos, sin = np.cos(ang).astype(np.float32), np.sin(ang).astype(np.float32)
    n = pos.shape[0]
    ones = np.ones((n, HEAD_DIM - ROT_DIM), np.float32)
    zeros = np.zeros((n, HEAD_DIM - ROT_DIM), np.float32)
    z8 = np.zeros((n, ROT_HALF), np.float32)
    c = np.concatenate([cos, cos, ones], axis=1)
    sa = np.concatenate([-sin, z8, zeros], axis=1)
    sb = np.concatenate([z8, sin, zeros], axis=1)
    return tuple(jnp.asarray(np.tile(t, (1, N_KV_HEADS))) for t in (c, sa, sb))


def _proj(x2, p, tabs, tm, seq_len=None):
    rows, d = x2.shape
    n_tab = tabs[0].shape[0] // tm
    w = jnp.pad(p['w_in'], ((0, 0), (0, N_IN_PAD - N_IN))).astype(BF16)
    const = lambda i: (0, 0)
    row = lambda i: (i, 0)
    tab = lambda i: (i % n_tab, 0)
    in_specs = [
        pl.BlockSpec((tm, d), row), pl.BlockSpec((1, d), const),
        pl.BlockSpec((d, N_IN_PAD), const, pipeline_mode=pl.Buffered(1)),
        pl.BlockSpec((1, D_ATT), const), pl.BlockSpec((1, KV_W), const), pl.BlockSpec((1, KV_W), const),
        pl.BlockSpec((D_ATT, D_ATT), const), pl.BlockSpec((KV_W, KV_W), const),
        pl.BlockSpec((tm, LANES), tab), pl.BlockSpec((tm, LANES), tab), pl.BlockSpec((tm, LANES), tab),
    ]
    widths = [C_CONV, D_ATT, LANES, KV_W, KV_W] + [KV_W] * (4 if seq_len is None else 2)
    out_shape = [jax.ShapeDtypeStruct((rows, wd), F32) for wd in widths]
    out_specs = [pl.BlockSpec((tm, wd), row) for wd in widths]
    if seq_len is not None:
        nt = seq_len // tm
        assert seq_len % tm == 0 and rows % seq_len == 0
        out_shape += [jax.ShapeDtypeStruct((rows // seq_len, KV_W, seq_len), F32)] * 6
        out_specs += [pl.BlockSpec((1, KV_W, tm), lambda i: (i // nt, 0, i % nt))] * 6
    args = [x2, p['mix_norm'].reshape(1, d), w,
            jnp.tile(p['q_norm'], N_HEADS).reshape(1, D_ATT),
            jnp.tile(p['k_sel_norm'], N_KV_HEADS).reshape(1, KV_W),
            jnp.tile(p['k_win_norm'], N_KV_HEADS).reshape(1, KV_W),
            _blockdiag_ones(D_ATT), _blockdiag_ones(KV_W), *tabs]
    return pl.pallas_call(
        functools.partial(_proj_body, transposed=seq_len is not None), out_shape=out_shape,
        grid=(rows // tm,), in_specs=in_specs, out_specs=out_specs,
        compiler_params=_params(1), name="proj")(*args)


def _conv_post(acc, lg, lb, og):
    mu = jnp.mean(acc, axis=-1, keepdims=True)
    xc = acc - mu
    var = jnp.mean(xc * xc, axis=-1, keepdims=True)
    y = xc * lax.rsqrt(var + EPS) * lg + lb
    y = y * _sigmoid(y)
    return _rms(y, og)


def _conv_body(u_ref, cw_ref, cb_ref, lg_ref, lb_ref, og_ref, o_ref, *, n_tiles, tt):
    halo = 32
    lead = halo - (CONV_W - 1)

    def tile(win, t0):
        acc = jnp.zeros((tt, C_CONV), F32) + cb_ref[...]
        for r in range(8):
            sh = win if r == 0 else pltpu.roll(win, tt + halo - r, 0)
            for a in range(halo // 8 + 1):
                w = 8 * a + r - lead
                if 0 <= w < CONV_W:
                    acc = acc + sh[8 * a:8 * a + tt, :] * cw_ref[w:w + 1, :]
        o_ref[0, pl.ds(t0, tt), :] = _conv_post(acc, lg_ref[...], lb_ref[...], og_ref[...])

    tile(jnp.concatenate([jnp.zeros((halo, C_CONV), F32), u_ref[0, 0:tt, :]], axis=0), 0)

    def step(i, carry):
        t0 = pl.multiple_of(i * tt, tt)
        tile(u_ref[0, pl.ds(pl.multiple_of(t0 - halo, halo), tt + halo), :], t0)
        return carry

    lax.fori_loop(1, n_tiles, step, 0)


def _conv_prompt(u, p, tt=256):
    b, t, c = u.shape
    cw = jnp.pad(p['conv_w'], ((0, 1), (0, 0)))
    const = lambda i: (0, 0)
    vec = lambda a: a.reshape(1, c)
    blk = pl.BlockSpec((1, t, c), lambda i: (i, 0, 0))
    return pl.pallas_call(
        functools.partial(_conv_body, n_tiles=t // tt, tt=tt),
        out_shape=jax.ShapeDtypeStruct((b, t, c), F32), grid=(b,),
        in_specs=[blk, pl.BlockSpec((32, c), const)] + [pl.BlockSpec((1, c), const)] * 4,
        out_specs=blk, compiler_params=_params(1), name="conv_prompt")(
            u, cw, vec(p['conv_b']), vec(p['conv_ln_g']), vec(p['conv_ln_b']), vec(p['out_norm_conv']))


def _conv_sample_body(st_ref, u_ref, cw_ref, cb_ref, lg_ref, lb_ref, og_ref, o_ref):
    acc = u_ref[...] * cw_ref[CONV_W - 1:CONV_W, :] + cb_ref[...]
    for w in range(CONV_W - 1):
        acc = acc + st_ref[w] * cw_ref[w:w + 1, :]
    o_ref[...] = _conv_post(acc, lg_ref[...], lb_ref[...], og_ref[...])


def _conv_sample(state, u, p):
    b = u.shape[0]
    c = C_CONV
    cw = jnp.pad(p['conv_w'], ((0, 1), (0, 0)))
    vec = lambda a: a.reshape(1, c)
    return pl.pallas_call(
        _conv_sample_body, out_shape=jax.ShapeDtypeStruct((b, c), F32),
        compiler_params=pltpu.CompilerParams(vmem_limit_bytes=VMEM_LIMIT), name="conv_sample")(
            state.transpose(1, 0, 2), u, cw, vec(p['conv_b']), vec(p['conv_ln_g']),
            vec(p['conv_ln_b']), vec(p['out_norm_conv']))


def _compress_weights(pe, w1, w2):
    assert N_SUB == 2
    w1r = w1.reshape(N_SUB, CMP_STRIDE, HEAD_DIM, CMP_HID).transpose(1, 2, 0, 3)
    eye = jnp.eye(N_KV_HEADS, dtype=w1.dtype)
    wf = jnp.einsum('sdmh,gk->sgdmkh', w1r, eye).reshape(CHUNK_W, N_SUB * N_KV_HEADS * CMP_HID)
    per = jnp.broadcast_to(pe.reshape(N_SUB, CMP_STRIDE, 1, HEAD_DIM),
                           (N_SUB, CMP_STRIDE, N_KV_HEADS, HEAD_DIM)).reshape(N_SUB, CHUNK_W)
    w2f = jnp.einsum('hd,gk->ghkd', w2, eye).reshape(N_KV_HEADS * CMP_HID, KV_W)
    return per, wf.astype(BF16), w2f.astype(BF16)


def _position_pair_weights(per, wf):
    half = N_KV_HEADS * CMP_HID
    pairs = CMP_STRIDE // 2
    pe_p = per.reshape(N_SUB * pairs, 2 * KV_W)
    w_p = wf.reshape(pairs, 2 * KV_W, N_SUB, half).transpose(2, 0, 1, 3).reshape(N_SUB * pairs, 2 * KV_W, half)
    return pe_p, w_p


def _compress_tail(a0, a1, w2_ref):
    n = a0.shape[0]
    h = a0 + pltpu.roll(a1, n - 1, 0)
    t = h * _sigmoid(h)
    tok = _dot(t.astype(BF16), w2_ref[...])
    return jnp.where(lax.broadcasted_iota(jnp.int32, tok.shape, 0) < n - 1, tok, 0.0)


def _cmp_key_post(tok, kng, bd1, c, sa, sb):
    return _rope(_group_rms(tok, kng[...], bd1[...]), c[...], sa[...], sb[...])


def _first_layer(rows_ref, row0, n_chunk, pe_ref, w1_ref):
    half = N_KV_HEADS * CMP_HID
    pairs = CMP_STRIDE // 2
    a0 = jnp.zeros((n_chunk, half), F32)
    a1 = jnp.zeros((n_chunk, half), F32)
    for sp in range(pairs):
        xs = jnp.concatenate(
            [rows_ref[pl.ds(row0 + 2 * sp + e, n_chunk, stride=CMP_STRIDE), :] for e in range(2)], axis=1)
        a0 = a0 + _dot((xs + pe_ref[sp:sp + 1, :]).astype(BF16), w1_ref[sp])
        a1 = a1 + _dot((xs + pe_ref[pairs + sp:pairs + sp + 1, :]).astype(BF16), w1_ref[pairs + sp])
    return a0, a1


def _cmp_prompt_body(xk_ref, xv_ref, pek, w1k, w2k, kng, bd1, c, sa, sb, pev, w1v, w2v, ok_ref, ov_ref):
    n_chunk = ok_ref.shape[1]
    tok_k = _compress_tail(*_first_layer(xk_ref.at[0], 0, n_chunk, pek, w1k), w2k)
    tok_v = _compress_tail(*_first_layer(xv_ref.at[0], 0, n_chunk, pev, w1v), w2v)
    ok_ref[0] = _cmp_key_post(tok_k, kng, bd1, c, sa, sb)
    ov_ref[0] = tok_v.T


def _cmp_weights(p, name):
    per, wf, w2f = _compress_weights(p[f'cmp_{name}_pos'], p[f'cmp_{name}_w1'], p[f'cmp_{name}_w2'])
    return [*_position_pair_weights(per, wf), w2f]


def _cmp_prompt(kc, vc, p):
    b, t, _ = kc.shape
    n_chunk = t // CMP_STRIDE
    args = [*_cmp_weights(p, 'k'), *_cmp_key_consts(p, n_chunk), *_cmp_weights(p, 'v')]
    blk = pl.BlockSpec((1, t, KV_W), lambda i: (i, 0, 0))
    b3 = lambda i: (i, 0, 0)
    const = lambda a: pl.BlockSpec(a.shape, lambda i: (0,) * a.ndim)
    return pl.pallas_call(
        _cmp_prompt_body,
        out_shape=[jax.ShapeDtypeStruct((b, n_chunk, KV_W), F32), jax.ShapeDtypeStruct((b, KV_W, n_chunk), F32)],
        grid=(b,), in_specs=[blk, blk] + [const(a) for a in args],
        out_specs=[pl.BlockSpec((1, n_chunk, KV_W), b3), pl.BlockSpec((1, KV_W, n_chunk), b3)],
        compiler_params=_params(1), name="cmp_prompt")(kc, vc, *args)


def _cmp_key_consts(p, n_chunk):
    cmp_end = np.arange(n_chunk) * CMP_STRIDE + CMP_LEN - 1
    kng = jnp.tile(p['k_cmp_norm'], N_KV_HEADS).reshape(1, KV_W)
    return [kng, _blockdiag_ones(KV_W), *_rope_tables(cmp_end)]


def _cmp_sample_body(pt_ref, cache_hbm, pe_ref, w1_ref, w2_ref, *rest, n_pages, n_batch, is_key):
    key_consts, (o_ref, buf, tok, a_sc, sem) = rest[:-5], rest[-5:]
    b = pl.program_id(0)
    slot = b % 2
    page_len = cache_hbm.shape[2]
    n_pos = n_pages * page_len
    unroll = 8 if n_pages % 8 == 0 else 1

    def page_copy(page, i, sl):
        dst = pl.ds(pl.multiple_of(i * page_len, page_len), page_len)
        return pltpu.make_async_copy(cache_hbm.at[page], buf.at[sl, :, dst], sem.at[sl])

    def gather(seq, sl):
        def issue(i, carry):
            page_copy(pt_ref[seq * n_pages + i], i, sl).start()
            return carry
        lax.fori_loop(0, n_pages, issue, 0, unroll=unroll)

    @pl.when(b == 0)
    def _():
        gather(0, 0)

    @pl.when(b + 1 < n_batch)
    def _():
        gather(b + 1, 1 - slot)

    def wait(i, carry):
        page_copy(0, i, slot).wait()
        return carry
    lax.fori_loop(0, n_pages, wait, 0, unroll=unroll)

    n_blk = 4 if n_pos % (4 * 8 * page_len) == 0 else 1
    pos_blk = n_pos // n_blk
    chunk_blk = pos_blk // CMP_STRIDE
    xw = min(pos_blk, 8 * page_len)
    for j in range(n_blk):
        p0 = j * pos_blk
        for i in range(pos_blk // xw):
            o = p0 + i * xw
            tok[o:o + xw, :] = buf[slot, :, o:o + xw].T
        a0, a1 = _first_layer(tok, p0, chunk_blk, pe_ref, w1_ref)
        a_sc[0, j * chunk_blk:(j + 1) * chunk_blk, :] = a0
        a_sc[1, j * chunk_blk:(j + 1) * chunk_blk, :] = a1
    out = _compress_tail(a_sc[0], a_sc[1], w2_ref)
    o_ref[0] = _cmp_key_post(out, *key_consts) if is_key else out


def _cmp_sample(page_table, cache_t, p, is_key):
    b, n_pages = page_table.shape
    page_len = cache_t.shape[2]
    n_pos = n_pages * page_len
    n_chunk = n_pos // CMP_STRIDE
    name = 'k' if is_key else 'v'
    args = _cmp_weights(p, name) + (_cmp_key_consts(p, n_chunk) if is_key else [])
    const = lambda a: pl.BlockSpec(a.shape, lambda i, pt: (0,) * a.ndim)
    grid_spec = pltpu.PrefetchScalarGridSpec(
        num_scalar_prefetch=1, grid=(b,),
        in_specs=[pl.BlockSpec(memory_space=pl.ANY)] + [const(a) for a in args],
        out_specs=pl.BlockSpec((1, n_chunk, KV_W), lambda i, pt: (i, 0, 0)),
        scratch_shapes=[pltpu.VMEM((2, KV_W, n_pos), F32), pltpu.VMEM((n_pos, KV_W), F32),
                        pltpu.VMEM((N_SUB, n_chunk, N_KV_HEADS * CMP_HID), F32),
                        pltpu.SemaphoreType.DMA((2,))])
    return pl.pallas_call(
        functools.partial(_cmp_sample_body, n_pages=n_pages, n_batch=b, is_key=is_key),
        out_shape=jax.ShapeDtypeStruct((b, n_chunk, KV_W), F32), grid_spec=grid_spec,
        compiler_params=_params(1), name=f"cmp_sample_{name}")(page_table.reshape(-1), cache_t, *args)


def _cmp_to_sel(n_chunk, n_cmp, n_sel, width):
    cs = np.arange(n_chunk)[:, None] * CMP_STRIDE
    ss = np.arange(width)[None, :] * SEL_BLK
    ov = np.clip(np.minimum(cs + CMP_LEN, ss + SEL_BLK) - np.maximum(cs, ss), 0, None) / CMP_LEN
    ov = ov * (np.arange(n_chunk)[:, None] < n_cmp) * (np.arange(width)[None, :] < n_sel)
    return jnp.asarray(ov, dtype=BF16)


def _attn_qlanes_body(q_ref, gt_ref, kc_ref, vct_ref, ks_ref, vst_ref, kw_ref, vwt_ref, ovt_ref, eblk_ref,
                      tri_ref, o_ref, qa_sc, oct_sc, m_sc, l_sc, acc_sc, s_sc,
                      *, t, tq, kc_len, n_cmp, n_sel):
    off = pl.program_id(1) * tq
    cols = GRP * tq
    lane = lax.broadcasted_iota(jnp.int32, (tq, LANES), 1)
    qpos = off + lax.broadcasted_iota(jnp.int32, (1, tq), 1)
    q = q_ref[0] * (SCALE * LOG2E)
    for h in range(N_HEADS):
        g, r = divmod(h, GRP)
        qa_sc[g, :KV_W, r * tq:(r + 1) * tq] = _head_to_group_lanes(q, h, lane).T.astype(BF16)

    def attend(jobs):
        stats = []
        for st, k, vt, rhs, bias in jobs:
            n = k.shape[0]
            s = _dot(k, rhs)
            if bias is not None:
                s = s + bias
            s_sc[st, :n, :] = s
            m_old = m_sc[st]
            m_new = jnp.maximum(m_old, jnp.max(s, axis=0, keepdims=True))
            m_sc[st] = m_new
            stats.append((m_new, jnp.exp2(m_old - m_new)))
        probs = []
        for (st, k, vt, rhs, bias), (m_new, alpha) in zip(jobs, stats):
            p = jnp.exp2(s_sc[st, :k.shape[0], :] - m_new)
            l_sc[st] = alpha * l_sc[st] + jnp.sum(p, axis=0, keepdims=True)
            probs.append(p.astype(BF16))
        for (st, k, vt, rhs, bias), (m_new, alpha), p in zip(jobs, stats, probs):
            acc_sc[st] = alpha * acc_sc[st] + _dot(vt, p)

    sel_state = lambda g: g
    win_state = lambda g: N_KV_HEADS + g
    m_sc[...] = jnp.full(m_sc.shape, NEG_INF, F32)
    l_sc[...] = jnp.zeros(l_sc.shape, F32)
    acc_sc[...] = jnp.zeros(acc_sc.shape, F32)

    n_chunk = kc_ref.shape[1]
    kcm = kc_ref[0].astype(BF16)
    vct = vct_ref[0].astype(BF16)
    nrow = lax.broadcasted_iota(jnp.int32, (n_chunk, tq), 0)
    mask_1 = jnp.logical_and(nrow * CMP_STRIDE + (CMP_LEN - 1) <= qpos, nrow < n_cmp)
    mask_c = jnp.concatenate([mask_1.astype(F32)] * GRP, axis=1) > 0.5
    psum = []
    for g in range(N_KV_HEADS):
        sc = jnp.where(mask_c, _dot(kcm, qa_sc[g, :KV_W, :]), NEG_INF)
        e = jnp.where(mask_c, jnp.exp2(sc - jnp.max(sc, axis=0, keepdims=True)), 0.0)
        pc = e / jnp.maximum(jnp.sum(e, axis=0, keepdims=True), 1e-30)
        oct_sc[g] = _dot(vct, pc.astype(BF16))
        psum.append(sum(pc[:, r * tq:(r + 1) * tq] for r in range(GRP)))

    n_sel_pad = -(-n_sel // 8) * 8
    jrow = lax.broadcasted_iota(jnp.int32, (n_sel_pad, tq), 0)
    cur = qpos // SEL_BLK
    forced = jnp.logical_or(jrow == 0, jnp.logical_or(jrow == cur, jrow == cur - 1))
    bonus = jnp.where(forced, FORCE_BONUS, 0.0)
    k_eff = min(TOPK, n_sel)
    ovt = ovt_ref[...]
    for g in range(N_KV_HEADS):
        hi = psum[g].astype(BF16)
        lo = (psum[g] - hi.astype(F32)).astype(BF16)
        imp = _dot(ovt, hi) + _dot(ovt, lo)
        score = jnp.where(jrow <= cur, imp + bonus, NEG_INF)
        rank = jnp.zeros((n_sel_pad, tq), F32)
        for j in range(n_sel):
            row = score[j:j + 1, :]
            tie = jnp.where(jrow > j, 1.0, 0.0)
            rank = rank + jnp.where(row > score, 1.0, jnp.where(row == score, tie, 0.0))
        keep = jnp.logical_and(rank < k_eff, jrow < off // SEL_BLK)
        pen = jnp.where(keep, 0.0, -MASK_PENALTY).astype(BF16)
        qa_sc[g, KV_W:KV_W + n_sel_pad, :] = jnp.concatenate([pen] * GRP, axis=1)
        qa_sc[g, KV_W + n_sel_pad:, :] = jnp.zeros((LANES - n_sel_pad, cols), BF16)

    tq_off = pl.multiple_of(off, tq)
    tri_own = tri_ref[0]
    tri_front = tri_ref[1]

    k_own = ks_ref[0, pl.ds(tq_off, tq), :].astype(BF16)
    vt_own = vst_ref[0, :, pl.ds(tq_off, tq)].astype(BF16)
    kw_own = kw_ref[0, pl.ds(tq_off, tq), :].astype(BF16)
    vwt_own = vwt_ref[0, :, pl.ds(tq_off, tq)].astype(BF16)
    groups = range(N_KV_HEADS)
    attend([(sel_state(g), k_own, vt_own, qa_sc[g, :KV_W, :], tri_own) for g in groups]
           + [(win_state(g), kw_own, vwt_own, qa_sc[g, :KV_W, :], tri_own) for g in groups])

    for ci in range(t // kc_len):
        @pl.when(ci * kc_len < off)
        def _(ci=ci):
            base = ci * kc_len
            k_aug = jnp.concatenate(
                [ks_ref[0, base:base + kc_len, :].astype(BF16), eblk_ref[base:base + kc_len, :]], axis=1)
            vt = vst_ref[0, :, base:base + kc_len].astype(BF16)
            attend([(sel_state(g), k_aug, vt, qa_sc[g], None) for g in groups])

    def window_keys(start, n, bias):
        start = pl.multiple_of(start, tq)
        kw = kw_ref[0, pl.ds(start, n), :].astype(BF16)
        vwt = vwt_ref[0, :, pl.ds(start, n)].astype(BF16)
        attend([(win_state(g), kw, vwt, qa_sc[g, :KV_W, :], bias) for g in groups])

    @pl.when(off >= WINDOW)
    def _():
        window_keys(off - WINDOW, tq, tri_front)
        window_keys(off - WINDOW + tq, WINDOW - tq, None)

    for i in range(1, WINDOW // tq):
        @pl.when(jnp.logical_and(off < WINDOW, off - i * tq >= 0))
        def _(i=i):
            window_keys(off - i * tq, tq, None)

    gtt = gt_ref[0].T
    outs = []
    for g in range(N_KV_HEADS):
        o_w = acc_sc[win_state(g)] / jnp.maximum(l_sc[win_state(g)], 1e-30)
        o_s = acc_sc[sel_state(g)] / jnp.maximum(l_sc[sel_state(g)], 1e-30)
        o_c = oct_sc[g]
        for r in range(GRP):
            h = g * GRP + r
            cs = slice(r * tq, (r + 1) * tq)
            o_h = (gtt[3 * h:3 * h + 1, :] * o_c[:, cs] + gtt[3 * h + 1:3 * h + 2, :] * o_s[:, cs]
                   + gtt[3 * h + 2:3 * h + 3, :] * o_w[:, cs])
            outs.append(o_h.T)
    o_ref[0] = jnp.concatenate(
        [_pair_tile(outs[2 * p], outs[2 * p + 1], 2 * p, lane) for p in range(N_HEADS // 2)], axis=-1)


def _attn_qlanes(q, gt, kcmp, vcmp_t, ks, vs_t, kw, vw_t, tq=128, kc_len=512):
    b, t, _ = q.shape
    n_chunk = kcmp.shape[1]
    n_cmp = n_chunk - N_SUB + 1
    n_sel = -(-t // SEL_BLK)
    assert t % kc_len == 0 and t >= WINDOW + tq and n_sel <= LANES and n_chunk % LANES == 0
    assert tq % LANES == 0 and WINDOW % tq == 0 and tq % SEL_BLK == 0
    n_sel_pad = -(-n_sel // 8) * 8
    ov_t = _cmp_to_sel(n_chunk, n_cmp, n_sel, n_sel_pad).T
    eblk = jnp.asarray(np.arange(t)[:, None] // SEL_BLK == np.arange(LANES)[None, :], dtype=BF16)
    kk, qq = np.arange(tq)[:, None], np.arange(tq)[None, :]
    tri = np.stack([np.where(kk <= qq, 0.0, -np.inf), np.where(kk > qq, 0.0, -np.inf)]).astype(np.float32)
    tri = jnp.asarray(np.tile(tri, (1, 1, GRP)))
    qblk = lambda i, j: (i, j, 0)
    full = lambda i, j: (i, 0, 0)
    const = lambda a: pl.BlockSpec(a.shape, lambda i, j: (0,) * a.ndim)
    tok = pl.BlockSpec((1, t, KV_W), full)
    tr = pl.BlockSpec((1, KV_W, t), full)
    cols = GRP * tq
    n_states = 2 * N_KV_HEADS
    assert kc_len >= WINDOW - tq
    body = functools.partial(_attn_qlanes_body, t=t, tq=tq, kc_len=kc_len, n_cmp=n_cmp, n_sel=n_sel)
    return pl.pallas_call(
        body, out_shape=jax.ShapeDtypeStruct((b, t, D_ATT), F32), grid=(b, t // tq),
        in_specs=[pl.BlockSpec((1, tq, D_ATT), qblk), pl.BlockSpec((1, tq, LANES), qblk),
                  pl.BlockSpec((1, n_chunk, KV_W), full), pl.BlockSpec((1, KV_W, n_chunk), full),
                  tok, tr, tok, tr, const(ov_t), const(eblk), const(tri)],
        out_specs=pl.BlockSpec((1, tq, D_ATT), qblk),
        scratch_shapes=[pltpu.VMEM((N_KV_HEADS, KV_W + LANES, cols), BF16),
                        pltpu.VMEM((N_KV_HEADS, KV_W, cols), F32), pltpu.VMEM((n_states, 1, cols), F32),
                        pltpu.VMEM((n_states, 1, cols), F32), pltpu.VMEM((n_states, KV_W, cols), F32),
                        pltpu.VMEM((n_states, kc_len, cols), F32)],
        compiler_params=_params(2), name="attn_prompt")(
            q, gt, kcmp, vcmp_t, ks, vs_t, kw, vw_t, ov_t, eblk, tri)


def _sample_heads(q_row):
    lane = lax.broadcasted_iota(jnp.int32, (1, LANES), 1)
    return jnp.concatenate([_head_to_group_lanes(q_row, h, lane) for h in range(N_HEADS)], axis=0)


def _attn_s1_body(q_ref, kc_ref, vc_ref, ov_ref, oc_ref, idx_ref, *, n_cmp, n_sel, q_pos):
    qh = _sample_heads(q_ref[0])
    n_chunk = kc_ref.shape[1]
    ncol = lax.broadcasted_iota(jnp.int32, (N_HEADS, n_chunk), 1)
    mask = jnp.logical_and(ncol * CMP_STRIDE + (CMP_LEN - 1) <= q_pos, ncol < n_cmp)
    pc = _masked_softmax(_dot_t(qh.astype(BF16), kc_ref[0].astype(BF16)) * SCALE, mask)
    oc_ref[0] = _dot(pc.astype(BF16), vc_ref[0].astype(BF16))
    rows = [jnp.sum(pc[g * GRP:(g + 1) * GRP, :], axis=0, keepdims=True) for g in range(N_KV_HEADS)]
    psum = jnp.concatenate(rows + [jnp.zeros((N_HEADS - N_KV_HEADS, n_chunk), F32)], axis=0)
    imp = _split_dot(psum, ov_ref[...])
    width = imp.shape[1]
    jl = lax.broadcasted_iota(jnp.int32, (N_HEADS, width), 1)
    cur = q_pos // SEL_BLK
    forced = jnp.logical_or(jl == 0, jnp.logical_or(jl == cur, jl == cur - 1))
    score = jnp.where(jl <= cur, imp + jnp.where(forced, FORCE_BONUS, 0.0), NEG_INF)
    ii = lax.broadcasted_iota(jnp.int32, (width, width), 0)
    jj = lax.broadcasted_iota(jnp.int32, (width, width), 1)
    slot = lax.broadcasted_iota(jnp.int32, (width, LANES), 1).astype(F32)
    blk = lax.broadcasted_iota(jnp.int32, (width, LANES), 0).astype(F32)
    tie = jnp.where(jj < ii, 1.0, 0.0)
    out_rows = []
    for g in range(N_KV_HEADS):
        row = jnp.broadcast_to(score[g:g + 1, :], (width, width))
        col = jnp.sum(jnp.where(ii == jj, row, 0.0), axis=1, keepdims=True)
        beats = jnp.where(row > col, 1.0, jnp.where(row == col, tie, 0.0))
        rank = jnp.sum(beats, axis=1, keepdims=True)
        out_rows.append(jnp.sum(jnp.where(rank == slot, blk, 0.0), axis=0, keepdims=True))
    out_rows.append(jnp.zeros((N_HEADS - N_KV_HEADS, LANES), F32))
    idx_ref[0] = jnp.concatenate(out_rows, axis=0).astype(jnp.int32)


def _attn_s1(q, kcmp, vcmp, q_pos, n_sel):
    b = q.shape[0]
    n_chunk = kcmp.shape[1]
    n_cmp = n_chunk - N_SUB + 1
    width = -(-n_sel // LANES) * LANES
    ov = _cmp_to_sel(n_chunk, n_cmp, n_sel, width)
    blk3 = lambda i: (i, 0, 0)
    cm = pl.BlockSpec((1, n_chunk, KV_W), blk3)
    out = pl.BlockSpec((1, N_HEADS, LANES), blk3)
    return pl.pallas_call(
        functools.partial(_attn_s1_body, n_cmp=n_cmp, n_sel=n_sel, q_pos=q_pos),
        out_shape=[jax.ShapeDtypeStruct((b, N_HEADS, LANES), F32),
                   jax.ShapeDtypeStruct((b, N_HEADS, LANES), jnp.int32)],
        grid=(b,),
        in_specs=[pl.BlockSpec((1, 1, D_ATT), blk3), cm, cm, pl.BlockSpec(ov.shape, lambda i: (0, 0))],
        out_specs=[out, out], compiler_params=_params(1), name="attn_sample_select")(
            q.reshape(b, 1, D_ATT), kcmp, vcmp, ov)


def _extra_key_attention(qh, k_t, v_t, mask, k_new, v_new):
    qf = qh.astype(F32)
    s = jnp.where(mask, _dot(qh, k_t.astype(BF16)) * SCALE, NEG_INF)
    s_new = jnp.sum(qf * k_new, axis=-1, keepdims=True) * SCALE
    m = jnp.maximum(jnp.max(s, axis=-1, keepdims=True), s_new)
    e = jnp.where(mask, jnp.exp(s - m), 0.0)
    e_new = jnp.exp(s_new - m)
    denom = jnp.maximum(jnp.sum(e, axis=-1, keepdims=True) + e_new, 1e-30)
    return (_dot_t(e.astype(BF16), v_t.astype(BF16)) + e_new * v_new) / denom


def _attn_s2_body(idx_ref, pt_ref, q_ref, gt_ref, oc_ref, ksn_ref, vsn_ref, kwn_ref, vwn_ref,
                  kws_ref, vws_ref, kcache, vcache, o_ref, kbuf, vbuf, sem,
                  *, n_pages, n_batch, n_cache_blk, q_pos):
    b = pl.program_id(0)
    slot = b % 2
    page_len = kcache.shape[2]
    per_page = page_len // SEL_BLK

    def block_id(seq, g, k):
        return idx_ref[(seq * N_KV_HEADS + g) * TOPK + k]

    def copies(seq, sl, g, k):
        blk = jnp.minimum(block_id(seq, g, k), n_cache_blk - 1)
        page = pt_ref[seq * n_pages + blk // per_page]
        dst = pl.ds(k * page_len, page_len)
        return (pltpu.make_async_copy(kcache.at[page], kbuf.at[sl, g, :, dst], sem.at[sl, 0]),
                pltpu.make_async_copy(vcache.at[page], vbuf.at[sl, g, :, dst], sem.at[sl, 1]))

    def for_all_copies(seq, sl, fn):
        for g in range(N_KV_HEADS):
            for k in range(TOPK):
                for cp in copies(seq, sl, g, k):
                    fn(cp)

    @pl.when(b == 0)
    def _():
        for_all_copies(0, 0, lambda cp: cp.start())

    @pl.when(b + 1 < n_batch)
    def _():
        for_all_copies(b + 1, 1 - slot, lambda cp: cp.start())

    for_all_copies(b, slot, lambda cp: cp.wait())
    blks = [[block_id(b, g, k) for k in range(TOPK)] for g in range(N_KV_HEADS)]

    qh = _sample_heads(q_ref[0]).astype(BF16)
    n_keys = TOPK * page_len
    row_grp = lax.broadcasted_iota(jnp.int32, (N_HEADS, n_keys), 0) // GRP
    key = lax.broadcasted_iota(jnp.int32, (N_HEADS, n_keys), 1)
    key_slot = key // page_len
    blk_in_page = (key % page_len) // SEL_BLK
    keep = jnp.zeros((N_HEADS, n_keys), F32)
    for g in range(N_KV_HEADS):
        for k in range(TOPK):
            blk = blks[g][k]
            in_cache = jnp.where(blk < n_cache_blk, 1.0, 0.0)
            hit = jnp.where(blk_in_page == blk % per_page, in_cache, 0.0)
            keep = jnp.where(jnp.logical_and(row_grp == g, key_slot == k), hit, keep)
    mask_s = keep > 0.5
    o_sel = [_extra_key_attention(qh, kbuf.at[slot, g][...], vbuf.at[slot, g][...], mask_s,
                                  ksn_ref[0], vsn_ref[0])
             for g in range(N_KV_HEADS)]
    out_grp = lax.broadcasted_iota(jnp.int32, (N_HEADS, LANES), 0) // GRP
    o_s = jnp.where(out_grp == 0, o_sel[0], o_sel[1])

    n_win = kws_ref.shape[2]
    kpos = (q_pos - n_win) + lax.broadcasted_iota(jnp.int32, (N_HEADS, n_win), 1)
    dist = q_pos - kpos
    mask_w = jnp.logical_and(dist >= 0, dist < WINDOW)
    o_w = _extra_key_attention(qh, kws_ref[0], vws_ref[0], mask_w, kwn_ref[0], vwn_ref[0])

    lane8 = lax.broadcasted_iota(jnp.int32, (N_HEADS, LANES), 1)
    head8 = lax.broadcasted_iota(jnp.int32, (N_HEADS, LANES), 0)
    gt = jnp.broadcast_to(gt_ref[0], (N_HEADS, LANES))
    gate = [jnp.sum(jnp.where(lane8 == 3 * head8 + j, gt, 0.0), axis=-1, keepdims=True) for j in range(3)]
    o = gate[0] * oc_ref[0] + gate[1] * o_s + gate[2] * o_w
    lane = lax.broadcasted_iota(jnp.int32, (1, LANES), 1)
    o_ref[0] = jnp.concatenate(
        [_pair_tile(o[2 * p:2 * p + 1, :], o[2 * p + 1:2 * p + 2, :], 2 * p, lane)
         for p in range(N_HEADS // 2)], axis=-1)


def _attn_s2(idx, page_table, q, gt, oc, ks_new, vs_new, kw_new, vw_new, kw_state_t, vw_state_t,
             cache_k_t, cache_v_t, q_pos):
    b, n_pages = page_table.shape
    page_len = cache_k_t.shape[2]
    n_win = kw_state_t.shape[2]
    assert n_win <= WINDOW
    blk3 = lambda i, *_: (i, 0, 0)
    row = lambda w: pl.BlockSpec((1, 1, w), blk3)
    grid_spec = pltpu.PrefetchScalarGridSpec(
        num_scalar_prefetch=2, grid=(b,),
        in_specs=[row(D_ATT), row(LANES), pl.BlockSpec((1, N_HEADS, LANES), blk3),
                  row(KV_W), row(KV_W), row(KV_W), row(KV_W),
                  pl.BlockSpec((1, KV_W, n_win), blk3), pl.BlockSpec((1, KV_W, n_win), blk3),
                  pl.BlockSpec(memory_space=pl.ANY), pl.BlockSpec(memory_space=pl.ANY)],
        out_specs=row(D_ATT),
        scratch_shapes=[pltpu.VMEM((2, N_KV_HEADS, KV_W, TOPK * page_len), F32),
                        pltpu.VMEM((2, N_KV_HEADS, KV_W, TOPK * page_len), F32),
                        pltpu.SemaphoreType.DMA((2, 2))])
    r3 = lambda a: a.reshape(b, 1, a.shape[-1])
    body = functools.partial(_attn_s2_body, n_pages=n_pages, n_batch=b,
                             n_cache_blk=n_pages * (page_len // SEL_BLK),
                             q_pos=q_pos)
    return pl.pallas_call(
        body, out_shape=jax.ShapeDtypeStruct((b, 1, D_ATT), F32), grid_spec=grid_spec,
        compiler_params=_params(1), name="attn_sample")(
            idx.reshape(-1), page_table.reshape(-1), r3(q), r3(gt), oc, r3(ks_new), r3(vs_new),
            r3(kw_new), r3(vw_new), kw_state_t, vw_state_t, cache_k_t, cache_v_t).reshape(b, D_ATT)


def _prompt_layer(x, p, tm=512, tm_proj=1024):
    b, t, d = x.shape
    x2 = _ffn(x.reshape(b * t, d), p['ffn1_norm'], p['ffn1_w_in'], p['ffn1_w_out'], None, tm)
    u, q, gt, kc, vc, ks, kw, kc_t, vc_t, ks_t, vs_t, kw_t, vw_t = _proj(
        x2, p, _rope_tables(np.arange(t)), tm_proj, seq_len=t)
    r3 = lambda a: a.reshape(b, t, a.shape[-1])
    u3 = r3(u)
    cn = _conv_prompt(u3, p)
    kcmp, vcmp_t = _cmp_prompt(r3(kc), r3(vc), p)
    o = _attn_qlanes(r3(q), r3(gt), kcmp, vcmp_t, r3(ks), vs_t, r3(kw), vw_t)
    y = _ffn(x2, p['ffn2_norm'], p['ffn2_w_in'], p['ffn2_w_out'], p['final_norm'], tm,
             mix=(cn.reshape(b * t, C_CONV), o.reshape(b * t, D_ATT), p['out_norm_attn'], p['w_out']))
    r4 = lambda a: a.reshape(b, N_KV_HEADS, HEAD_DIM, a.shape[-1]).transpose(0, 3, 1, 2)
    keep = min(WINDOW, t)
    state = (r4(kc_t), r4(vc_t), r4(ks_t), r4(vs_t), r4(kw_t[:, :, t - keep:]), r4(vw_t[:, :, t - keep:]),
             u3[:, t - (CONV_W - 1):])
    return y.reshape(b, t, d), state


def _sample_layer(x, p, cache_k_cmp, cache_v_cmp, cache_k_sel, cache_v_sel, kw_state, vw_state,
                  conv_state, page_table):
    b, t, d = x.shape
    assert t == 1
    past_len = page_table.shape[1] * cache_k_cmp.shape[1]
    x2 = _ffn(x.reshape(b, d), p['ffn1_norm'], p['ffn1_w_in'], p['ffn1_w_out'], None, b)
    tabs = _rope_tables(np.full((b,), past_len))
    u, q, gt, kc, vc, ks, vs, kw, vw = _proj(x2, p, tabs, b)
    cn = _conv_sample(conv_state, u, p)
    keys_on_lanes = lambda a: a.transpose(0, 2, 3, 1).reshape(a.shape[0], KV_W, a.shape[1])
    kcmp = _cmp_sample(page_table, keys_on_lanes(cache_k_cmp), p, True)
    vcmp = _cmp_sample(page_table, keys_on_lanes(cache_v_cmp), p, False)
    n_sel = -(-(past_len + 1) // SEL_BLK)
    oc, idx = _attn_s1(q, kcmp, vcmp, past_len, n_sel)
    n_win = kw_state.shape[1]
    o = _attn_s2(idx[:, :N_KV_HEADS, :TOPK], page_table, q, gt, oc, ks, vs, kw, vw,
                 keys_on_lanes(kw_state), keys_on_lanes(vw_state),
                 keys_on_lanes(cache_k_sel), keys_on_lanes(cache_v_sel), past_len)
    y = _ffn(x2, p['ffn2_norm'], p['ffn2_w_in'], p['ffn2_w_out'], p['final_norm'], b,
             mix=(cn, o, p['out_norm_attn'], p['w_out']))
    r4 = lambda a: a.reshape(b, 1, N_KV_HEADS, HEAD_DIM)
    keep = min(WINDOW, n_win + 1)
    new_kw = jnp.concatenate([kw_state, r4(kw)], axis=1)[:, n_win + 1 - keep:]
    new_vw = jnp.concatenate([vw_state, r4(vw)], axis=1)[:, n_win + 1 - keep:]
    new_conv = jnp.concatenate([conv_state, u[:, None, :]], axis=1)[:, 1:]
    return y.reshape(b, 1, d), (r4(kc), r4(vc), r4(ks), r4(vs), new_kw, new_vw, new_conv)


_PARAM_NAMES = ('ffn1_norm', 'ffn1_w_in', 'ffn1_w_out', 'mix_norm', 'w_in', 'conv_w', 'conv_b', 'conv_ln_g',
                'conv_ln_b', 'q_norm', 'k_cmp_norm', 'k_sel_norm', 'k_win_norm', 'cmp_k_pos', 'cmp_k_w1',
                'cmp_k_w2', 'cmp_v_pos', 'cmp_v_w1', 'cmp_v_w2', 'out_norm_conv', 'out_norm_attn', 'w_out',
                'ffn2_norm', 'ffn2_w_in', 'ffn2_w_out', 'final_norm')


def kernel(x_prompt, x_sample, cache_k_cmp, cache_v_cmp, cache_k_sel, cache_v_sel, state_k_win, state_v_win,
           state_conv, page_table, ffn1_norm, ffn1_w_in, ffn1_w_out, mix_norm, w_in, conv_w, conv_b, conv_ln_g,
           conv_ln_b, q_norm, k_cmp_norm, k_sel_norm, k_win_norm, cmp_k_pos, cmp_k_w1, cmp_k_w2, cmp_v_pos,
           cmp_v_w1, cmp_v_w2, out_norm_conv, out_norm_attn, w_out, ffn2_norm, ffn2_w_in, ffn2_w_out, final_norm):
    stacked = dict(zip(_PARAM_NAMES, (
        ffn1_norm, ffn1_w_in, ffn1_w_out, mix_norm, w_in, conv_w, conv_b, conv_ln_g, conv_ln_b, q_norm,
        k_cmp_norm, k_sel_norm, k_win_norm, cmp_k_pos, cmp_k_w1, cmp_k_w2, cmp_v_pos, cmp_v_w1, cmp_v_w2,
        out_norm_conv, out_norm_attn, w_out, ffn2_norm, ffn2_w_in, ffn2_w_out, final_norm)))
    depth = ffn1_norm.shape[0]
    yp, ys = x_prompt, x_sample
    new_p, new_s = [], []
    for l in range(depth):
        p = {k: v[l] for k, v in stacked.items()}
        yp, st_p = _prompt_layer(yp, p)
        new_p.append(st_p)
        ys, st_s = _sample_layer(ys, p, cache_k_cmp[l], cache_v_cmp[l], cache_k_sel[l], cache_v_sel[l],
                                 state_k_win[l], state_v_win[l], state_conv[l], page_table)
        new_s.append(st_s)
    outs_p = [jnp.stack(tup) for tup in zip(*new_p)]
    outs_s = [jnp.stack(tup) for tup in zip(*new_s)]
    return (yp, ys, *outs_p, *outs_s)
```

```python
import functools

import numpy as np
import jax
import jax.numpy as jnp
from jax import lax
from jax.experimental import pallas as pl
from jax.experimental.pallas import tpu as pltpu

F32 = jnp.float32
BF16 = jnp.bfloat16

HEAD_DIM = 64
N_HEADS = 8
N_KV_HEADS = 2
GRP = N_HEADS // N_KV_HEADS
KV_W = N_KV_HEADS * HEAD_DIM
C_CONV = 512
D_ATT = 512
CONV_W = 31
CMP_LEN = 32
CMP_STRIDE = 16
N_SUB = CMP_LEN // CMP_STRIDE
CMP_HID = 2 * HEAD_DIM
SEL_BLK = 64
TOPK = 16
WINDOW = 512
ROT_DIM = HEAD_DIM // 4
ROT_HALF = ROT_DIM // 2
ROPE_THETA = 500000.0
NEG_INF = -1e30
MASK_PENALTY = 2.0 ** 100
FORCE_BONUS = 1e4
EPS = 1e-6
SCALE = HEAD_DIM ** -0.5
LOG2E = 1.4426950408889634
LANES = 128
CHUNK_W = CMP_STRIDE * KV_W
VMEM_LIMIT = 56 * 1024 * 1024


def _params(n_grid_dims):
    return pltpu.CompilerParams(
        dimension_semantics=("arbitrary",) * n_grid_dims, vmem_limit_bytes=VMEM_LIMIT)


def _dot(a, b):
    return jnp.dot(a, b, preferred_element_type=F32)


def _dot_t(a, b):
    return lax.dot_general(a, b, (((1,), (1,)), ((), ())), preferred_element_type=F32)


def _split_dot(x, m_bf16):
    hi = x.astype(BF16)
    lo = (x - hi.astype(F32)).astype(BF16)
    return _dot(hi, m_bf16) + _dot(lo, m_bf16)


def _sigmoid(x):
    return 1.0 / (1.0 + jnp.exp(-x))


def _rms(x, g):
    return x * lax.rsqrt(jnp.mean(x * x, axis=-1, keepdims=True) + EPS) * g


def _group_rms(x, g_tiled, ones_blockdiag):
    ss = _split_dot(x * x, ones_blockdiag)
    return x * lax.rsqrt(ss * (1.0 / HEAD_DIM) + EPS) * g_tiled


def _rope(x, c, sa, sb):
    n = x.shape[-1]
    return x * c + pltpu.roll(x, n - ROT_HALF, 1) * sa + pltpu.roll(x, ROT_HALF, 1) * sb


def _tile_lanes(t, reps):
    return jnp.concatenate([t] * reps, axis=-1) if reps > 1 else t


def _masked_softmax(s, mask):
    s = jnp.where(mask, s, NEG_INF)
    m = jnp.max(s, axis=-1, keepdims=True)
    e = jnp.where(mask, jnp.exp(s - m), 0.0)
    return e / jnp.maximum(jnp.sum(e, axis=-1, keepdims=True), 1e-30)


def _head_to_group_lanes(q, h, lane):
    g, p = h // GRP, h // 2
    qp = q[:, LANES * p:LANES * (p + 1)]
    src = qp if (h % 2) == g else pltpu.roll(qp, HEAD_DIM, 1)
    keep = (lane < HEAD_DIM) if g == 0 else (lane >= HEAD_DIM)
    return jnp.where(keep, src, 0.0)


def _pair_tile(o_even, o_odd, h_even, lane):
    g = h_even // GRP
    a = o_even if g == 0 else pltpu.roll(o_even, HEAD_DIM, 1)
    b = pltpu.roll(o_odd, HEAD_DIM, 1) if g == 0 else o_odd
    return jnp.where(lane < HEAD_DIM, a, b)


def _ffn_body(*refs, n_chunk, tf, ff, final, mixed):
    refs = list(refs)
    x_ref = refs.pop(0)
    x = x_ref[...]
    if mixed:
        c_ref, a_ref, ag_ref, wc_ref, wa_ref = refs[:5]
        refs = refs[5:]
        x = (x + _dot(c_ref[...].astype(BF16), wc_ref[...])
             + _dot(_rms(a_ref[...], ag_ref[...]).astype(BF16), wa_ref[...]))
    g_ref, wi_ref, wo_ref = refs[:3]
    fg_ref = refs[3] if final else None
    o_ref = refs[-1]
    xn = _rms(x, g_ref[...]).astype(BF16)
    acc = jnp.zeros(x.shape, F32)
    for c in range(n_chunk):
        a = _dot(xn, wi_ref[:, c * tf:(c + 1) * tf])
        b = _dot(xn, wi_ref[:, ff + c * tf:ff + (c + 1) * tf])
        h = (a * _sigmoid(a)) * b
        acc = acc + _dot(h.astype(BF16), wo_ref[c * tf:(c + 1) * tf, :])
    y = x + 0.5 * acc
    if final:
        y = _rms(y, fg_ref[...])
    o_ref[...] = y


def _ffn(x2, norm_g, w_in, w_out, final_g, tm, mix=None):
    rows, d = x2.shape
    ff = w_out.shape[0]
    tf = 256
    assert rows % tm == 0 and ff % tf == 0 and w_in.shape == (d, 2 * ff)
    const = lambda i: (0, 0)
    row = lambda i: (i, 0)
    single = pl.Buffered(1)
    in_specs = [pl.BlockSpec((tm, d), row)]
    args = [x2]
    if mix is not None:
        cn, att, att_g, w_mix = mix
        w_mix = w_mix.astype(BF16)
        in_specs += [pl.BlockSpec((tm, C_CONV), row), pl.BlockSpec((tm, D_ATT), row),
                     pl.BlockSpec((1, D_ATT), const), pl.BlockSpec((C_CONV, d), const),
                     pl.BlockSpec((D_ATT, d), const)]
        args += [cn, att, att_g.reshape(1, D_ATT), w_mix[:C_CONV], w_mix[C_CONV:]]
    in_specs += [
        pl.BlockSpec((1, d), const),
        pl.BlockSpec((d, 2 * ff), const, pipeline_mode=single),
        pl.BlockSpec((ff, d), const, pipeline_mode=single),
    ]
    args += [norm_g.reshape(1, d), w_in.astype(BF16), w_out.astype(BF16)]
    if final_g is not None:
        in_specs.append(pl.BlockSpec((1, d), const))
        args.append(final_g.reshape(1, d))
    body = functools.partial(_ffn_body, n_chunk=ff // tf, tf=tf, ff=ff, final=final_g is not None,
                             mixed=mix is not None)
    return pl.pallas_call(
        body, out_shape=jax.ShapeDtypeStruct((rows, d), F32), grid=(rows // tm,),
        in_specs=in_specs, out_specs=pl.BlockSpec((tm, d), lambda i: (i, 0)),
        compiler_params=_params(1), name="ffn")(*args)


_SEG_GLU = 2 * C_CONV
_SEG_Q = _SEG_GLU + D_ATT
N_IN = _SEG_Q + 6 * KV_W + 3 * N_HEADS
N_IN_PAD = _SEG_Q + 7 * KV_W


def _proj_body(x_ref, g_ref, w_ref, qg_ref, ksg_ref, kwg_ref, bd4_ref, bd1_ref, c_ref, sa_ref, sb_ref,
               u_ref, q_ref, gt_ref, kc_ref, vc_ref, *kv_refs, transposed):
    xn = _rms(x_ref[...], g_ref[...]).astype(BF16)

    def seg(lo, width):
        return _dot(xn, w_ref[:, lo:lo + width])

    c, sa, sb = c_ref[...], sa_ref[...], sb_ref[...]
    reps = D_ATT // LANES
    glu = seg(0, _SEG_GLU)
    u_ref[...] = glu[:, :C_CONV] * _sigmoid(glu[:, C_CONV:])
    q = _group_rms(seg(_SEG_GLU, D_ATT), qg_ref[...], bd4_ref[...])
    q_ref[...] = _rope(q, _tile_lanes(c, reps), _tile_lanes(sa, reps), _tile_lanes(sb, reps))
    base = _SEG_Q
    kc = seg(base, KV_W)
    vc = seg(base + KV_W, KV_W)
    ks = _rope(_group_rms(seg(base + 2 * KV_W, KV_W), ksg_ref[...], bd1_ref[...]), c, sa, sb)
    vs = seg(base + 3 * KV_W, KV_W)
    kw = _rope(_group_rms(seg(base + 4 * KV_W, KV_W), kwg_ref[...], bd1_ref[...]), c, sa, sb)
    vw = seg(base + 5 * KV_W, KV_W)
    gt_ref[...] = _sigmoid(seg(base + 6 * KV_W, KV_W))
    kc_ref[...] = kc
    vc_ref[...] = vc
    if transposed:
        kv_refs[0][...] = ks
        kv_refs[1][...] = kw
        for ref, val in zip(kv_refs[2:], (kc, vc, ks, vs, kw, vw)):
            ref[0] = val.T
    else:
        for ref, val in zip(kv_refs, (ks, vs, kw, vw)):
            ref[...] = val


def _blockdiag_ones(width):
    idx = np.arange(width) // HEAD_DIM
    return jnp.asarray(idx[:, None] == idx[None, :], dtype=BF16)


def _rope_tables(pos):
    pos = np.asarray(pos, dtype=np.float64)
    inv = ROPE_THETA ** (-(np.arange(ROT_HALF, dtype=np.float64) * 2.0 / ROT_DIM))
    ang = pos[:, None] * inv[None, :]
    cos, sin = np.cos(ang).astype(np.float32), np.sin(ang).astype(np.float32)
    n = pos.shape[0]
    ones = np.ones((n, HEAD_DIM - ROT_DIM), np.float32)
    zeros = np.zeros((n, HEAD_DIM - ROT_DIM), np.float32)
    z8 = np.zeros((n, ROT_HALF), np.float32)
    c = np.concatenate([cos, cos, ones], axis=1)
    sa = np.concatenate([-sin, z8, zeros], axis=1)
    sb = np.concatenate([z8, sin, zeros], axis=1)
    return tuple(jnp.asarray(np.tile(t, (1, N_KV_HEADS))) for t in (c, sa, sb))


def _proj(x2, p, tabs, tm, seq_len=None):
    rows, d = x2.shape
    n_tab = tabs[0].shape[0] // tm
    w = jnp.pad(p['w_in'], ((0, 0), (0, N_IN_PAD - N_IN))).astype(BF16)
    const = lambda i: (0, 0)
    row = lambda i: (i, 0)
    tab = lambda i: (i % n_tab, 0)
    in_specs = [
        pl.BlockSpec((tm, d), row), pl.BlockSpec((1, d), const),
        pl.BlockSpec((d, N_IN_PAD), const, pipeline_mode=pl.Buffered(1)),
        pl.BlockSpec((1, D_ATT), const), pl.BlockSpec((1, KV_W), const), pl.BlockSpec((1, KV_W), const),
        pl.BlockSpec((D_ATT, D_ATT), const), pl.BlockSpec((KV_W, KV_W), const),
        pl.BlockSpec((tm, LANES), tab), pl.BlockSpec((tm, LANES), tab), pl.BlockSpec((tm, LANES), tab),
    ]
    widths = [C_CONV, D_ATT, LANES, KV_W, KV_W] + [KV_W] * (4 if seq_len is None else 2)
    out_shape = [jax.ShapeDtypeStruct((rows, wd), F32) for wd in widths]
    out_specs = [pl.BlockSpec((tm, wd), row) for wd in widths]
    if seq_len is not None:
        nt = seq_len // tm
        assert seq_len % tm == 0 and rows % seq_len == 0
        out_shape += [jax.ShapeDtypeStruct((rows // seq_len, KV_W, seq_len), F32)] * 6
        out_specs += [pl.BlockSpec((1, KV_W, tm), lambda i: (i // nt, 0, i % nt))] * 6
    args = [x2, p['mix_norm'].reshape(1, d), w,
            jnp.tile(p['q_norm'], N_HEADS).reshape(1, D_ATT),
            jnp.tile(p['k_sel_norm'], N_KV_HEADS).reshape(1, KV_W),
            jnp.tile(p['k_win_norm'], N_KV_HEADS).reshape(1, KV_W),
            _blockdiag_ones(D_ATT), _blockdiag_ones(KV_W), *tabs]
    return pl.pallas_call(
        functools.partial(_proj_body, transposed=seq_len is not None), out_shape=out_shape,
        grid=(rows // tm,), in_specs=in_specs, out_specs=out_specs,
        compiler_params=_params(1), name="proj")(*args)


def _conv_post(acc, lg, lb, og):
    mu = jnp.mean(acc, axis=-1, keepdims=True)
    xc = acc - mu
    var = jnp.mean(xc * xc, axis=-1, keepdims=True)
    y = xc * lax.rsqrt(var + EPS) * lg + lb
    y = y * _sigmoid(y)
    return _rms(y, og)


def _conv_body(u_ref, cw_ref, cb_ref, lg_ref, lb_ref, og_ref, o_ref, *, n_tiles, tt):
    halo = 32
    lead = halo - (CONV_W - 1)

    def tile(win, t0):
        acc = jnp.zeros((tt, C_CONV), F32) + cb_ref[...]
        for r in range(8):
            sh = win if r == 0 else pltpu.roll(win, tt + halo - r, 0)
            for a in range(halo // 8 + 1):
                w = 8 * a + r - lead
                if 0 <= w < CONV_W:
                    acc = acc + sh[8 * a:8 * a + tt, :] * cw_ref[w:w + 1, :]
        o_ref[0, pl.ds(t0, tt), :] = _conv_post(acc, lg_ref[...], lb_ref[...], og_ref[...])

    tile(jnp.concatenate([jnp.zeros((halo, C_CONV), F32), u_ref[0, 0:tt, :]], axis=0), 0)

    def step(i, carry):
        t0 = pl.multiple_of(i * tt, tt)
        tile(u_ref[0, pl.ds(pl.multiple_of(t0 - halo, halo), tt + halo), :], t0)
        return carry

    lax.fori_loop(1, n_tiles, step, 0)


def _conv_prompt(u, p, tt=256):
    b, t, c = u.shape
    cw = jnp.pad(p['conv_w'], ((0, 1), (0, 0)))
    const = lambda i: (0, 0)
    vec = lambda a: a.reshape(1, c)
    blk = pl.BlockSpec((1, t, c), lambda i: (i, 0, 0))
    return pl.pallas_call(
        functools.partial(_conv_body, n_tiles=t // tt, tt=tt),
        out_shape=jax.ShapeDtypeStruct((b, t, c), F32), grid=(b,),
        in_specs=[blk, pl.BlockSpec((32, c), const)] + [pl.BlockSpec((1, c), const)] * 4,
        out_specs=blk, compiler_params=_params(1), name="conv_prompt")(
            u, cw, vec(p['conv_b']), vec(p['conv_ln_g']), vec(p['conv_ln_b']), vec(p['out_norm_conv']))


def _conv_sample_body(st_ref, u_ref, cw_ref, cb_ref, lg_ref, lb_ref, og_ref, o_ref):
    acc = u_ref[...] * cw_ref[CONV_W - 1:CONV_W, :] + cb_ref[...]
    for w in range(CONV_W - 1):
        acc = acc + st_ref[w] * cw_ref[w:w + 1, :]
    o_ref[...] = _conv_post(acc, lg_ref[...], lb_ref[...], og_ref[...])


def _conv_sample(state, u, p):
    b = u.shape[0]
    c = C_CONV
    cw = jnp.pad(p['conv_w'], ((0, 1), (0, 0)))
    vec = lambda a: a.reshape(1, c)
    return pl.pallas_call(
        _conv_sample_body, out_shape=jax.ShapeDtypeStruct((b, c), F32),
        compiler_params=pltpu.CompilerParams(vmem_limit_bytes=VMEM_LIMIT), name="conv_sample")(
            state.transpose(1, 0, 2), u, cw, vec(p['conv_b']), vec(p['conv_ln_g']),
            vec(p['conv_ln_b']), vec(p['out_norm_conv']))


def _compress_weights(pe, w1, w2):
    assert N_SUB == 2
    w1r = w1.reshape(N_SUB, CMP_STRIDE, HEAD_DIM, CMP_HID).transpose(1, 2, 0, 3)
    eye = jnp.eye(N_KV_HEADS, dtype=w1.dtype)
    wf = jnp.einsum('sdmh,gk->sgdmkh', w1r, eye).reshape(CHUNK_W, N_SUB * N_KV_HEADS * CMP_HID)
    per = jnp.broadcast_to(pe.reshape(N_SUB, CMP_STRIDE, 1, HEAD_DIM),
                           (N_SUB, CMP_STRIDE, N_KV_HEADS, HEAD_DIM)).reshape(N_SUB, CHUNK_W)
    w2f = jnp.einsum('hd,gk->ghkd', w2, eye).reshape(N_KV_HEADS * CMP_HID, KV_W)
    return per, wf.astype(BF16), w2f.astype(BF16)


def _position_pair_weights(per, wf):
    half = N_KV_HEADS * CMP_HID
    pairs = CMP_STRIDE // 2
    pe_p = per.reshape(N_SUB * pairs, 2 * KV_W)
    w_p = wf.reshape(pairs, 2 * KV_W, N_SUB, half).transpose(2, 0, 1, 3).reshape(N_SUB * pairs, 2 * KV_W, half)
    return pe_p, w_p


def _compress_tail(a0, a1, w2_ref):
    n = a0.shape[0]
    h = a0 + pltpu.roll(a1, n - 1, 0)
    t = h * _sigmoid(h)
    tok = _dot(t.astype(BF16), w2_ref[...])
    return jnp.where(lax.broadcasted_iota(jnp.int32, tok.shape, 0) < n - 1, tok, 0.0)


def _cmp_key_post(tok, kng, bd1, c, sa, sb):
    return _rope(_group_rms(tok, kng[...], bd1[...]), c[...], sa[...], sb[...])


def _first_layer(rows_ref, row0, n_chunk, pe_ref, w1_ref):
    half = N_KV_HEADS * CMP_HID
    pairs = CMP_STRIDE // 2
    a0 = jnp.zeros((n_chunk, half), F32)
    a1 = jnp.zeros((n_chunk, half), F32)
    for sp in range(pairs):
        xs = jnp.concatenate(
            [rows_ref[pl.ds(row0 + 2 * sp + e, n_chunk, stride=CMP_STRIDE), :] for e in range(2)], axis=1)
        a0 = a0 + _dot((xs + pe_ref[sp:sp + 1, :]).astype(BF16), w1_ref[sp])
        a1 = a1 + _dot((xs + pe_ref[pairs + sp:pairs + sp + 1, :]).astype(BF16), w1_ref[pairs + sp])
    return a0, a1


def _cmp_prompt_body(xk_ref, xv_ref, pek, w1k, w2k, kng, bd1, c, sa, sb, pev, w1v, w2v, ok_ref, ov_ref):
    n_chunk = ok_ref.shape[1]
    tok_k = _compress_tail(*_first_layer(xk_ref.at[0], 0, n_chunk, pek, w1k), w2k)
    tok_v = _compress_tail(*_first_layer(xv_ref.at[0], 0, n_chunk, pev, w1v), w2v)
    ok_ref[0] = _cmp_key_post(tok_k, kng, bd1, c, sa, sb)
    ov_ref[0] = tok_v.T


def _cmp_weights(p, name):
    per, wf, w2f = _compress_weights(p[f'cmp_{name}_pos'], p[f'cmp_{name}_w1'], p[f'cmp_{name}_w2'])
    return [*_position_pair_weights(per, wf), w2f]


def _cmp_prompt(kc, vc, p):
    b, t, _ = kc.shape
    n_chunk = t // CMP_STRIDE
    args = [*_cmp_weights(p, 'k'), *_cmp_key_consts(p, n_chunk), *_cmp_weights(p, 'v')]
    blk = pl.BlockSpec((1, t, KV_W), lambda i: (i, 0, 0))
    b3 = lambda i: (i, 0, 0)
    const = lambda a: pl.BlockSpec(a.shape, lambda i: (0,) * a.ndim)
    return pl.pallas_call(
        _cmp_prompt_body,
        out_shape=[jax.ShapeDtypeStruct((b, n_chunk, KV_W), F32), jax.ShapeDtypeStruct((b, KV_W, n_chunk), F32)],
        grid=(b,), in_specs=[blk, blk] + [const(a) for a in args],
        out_specs=[pl.BlockSpec((1, n_chunk, KV_W), b3), pl.BlockSpec((1, KV_W, n_chunk), b3)],
        compiler_params=_params(1), name="cmp_prompt")(kc, vc, *args)


def _cmp_key_consts(p, n_chunk):
    cmp_end = np.arange(n_chunk) * CMP_STRIDE + CMP_LEN - 1
    kng = jnp.tile(p['k_cmp_norm'], N_KV_HEADS).reshape(1, KV_W)
    return [kng, _blockdiag_ones(KV_W), *_rope_tables(cmp_end)]


def _cmp_sample_body(pt_ref, cache_hbm, pe_ref, w1_ref, w2_ref, *rest, n_pages, n_batch, is_key):
    key_consts, (o_ref, buf, tok, a_sc, sem) = rest[:-5], rest[-5:]
    b = pl.program_id(0)
    slot = b % 2
    page_len = cache_hbm.shape[2]
    n_pos = n_pages * page_len
    unroll = 8 if n_pages % 8 == 0 else 1

    def page_copy(page, i, sl):
        dst = pl.ds(pl.multiple_of(i * page_len, page_len), page_len)
        return pltpu.make_async_copy(cache_hbm.at[page], buf.at[sl, :, dst], sem.at[sl])

    def gather(seq, sl):
        def issue(i, carry):
            page_copy(pt_ref[seq * n_pages + i], i, sl).start()
            return carry
        lax.fori_loop(0, n_pages, issue, 0, unroll=unroll)

    @pl.when(b == 0)
    def _():
        gather(0, 0)

    @pl.when(b + 1 < n_batch)
    def _():
        gather(b + 1, 1 - slot)

    def wait(i, carry):
        page_copy(0, i, slot).wait()
        return carry
    lax.fori_loop(0, n_pages, wait, 0, unroll=unroll)

    n_blk = 4 if n_pos % (4 * 8 * page_len) == 0 else 1
    pos_blk = n_pos // n_blk
    chunk_blk = pos_blk // CMP_STRIDE
    xw = min(pos_blk, 8 * page_len)
    for j in range(n_blk):
        p0 = j * pos_blk
        for i in range(pos_blk // xw):
            o = p0 + i * xw
            tok[o:o + xw, :] = buf[slot, :, o:o + xw].T
        a0, a1 = _first_layer(tok, p0, chunk_blk, pe_ref, w1_ref)
        a_sc[0, j * chunk_blk:(j + 1) * chunk_blk, :] = a0
        a_sc[1, j * chunk_blk:(j + 1) * chunk_blk, :] = a1
    out = _compress_tail(a_sc[0], a_sc[1], w2_ref)
    o_ref[0] = _cmp_key_post(out, *key_consts) if is_key else out


def _cmp_sample(page_table, cache_t, p, is_key):
    b, n_pages = page_table.shape
    page_len = cache_t.shape[2]
    n_pos = n_pages * page_len
    n_chunk = n_pos // CMP_STRIDE
    name = 'k' if is_key else 'v'
    args = _cmp_weights(p, name) + (_cmp_key_consts(p, n_chunk) if is_key else [])
    const = lambda a: pl.BlockSpec(a.shape, lambda i, pt: (0,) * a.ndim)
    grid_spec = pltpu.PrefetchScalarGridSpec(
        num_scalar_prefetch=1, grid=(b,),
        in_specs=[pl.BlockSpec(memory_space=pl.ANY)] + [const(a) for a in args],
        out_specs=pl.BlockSpec((1, n_chunk, KV_W), lambda i, pt: (i, 0, 0)),
        scratch_shapes=[pltpu.VMEM((2, KV_W, n_pos), F32), pltpu.VMEM((n_pos, KV_W), F32),
                        pltpu.VMEM((N_SUB, n_chunk, N_KV_HEADS * CMP_HID), F32),
                        pltpu.SemaphoreType.DMA((2,))])
    return pl.pallas_call(
        functools.partial(_cmp_sample_body, n_pages=n_pages, n_batch=b, is_key=is_key),
        out_shape=jax.ShapeDtypeStruct((b, n_chunk, KV_W), F32), grid_spec=grid_spec,
        compiler_params=_params(1), name=f"cmp_sample_{name}")(page_table.reshape(-1), cache_t, *args)


def _cmp_to_sel(n_chunk, n_cmp, n_sel, width):
    cs = np.arange(n_chunk)[:, None] * CMP_STRIDE
    ss = np.arange(width)[None, :] * SEL_BLK
    ov = np.clip(np.minimum(cs + CMP_LEN, ss + SEL_BLK) - np.maximum(cs, ss), 0, None) / CMP_LEN
    ov = ov * (np.arange(n_chunk)[:, None] < n_cmp) * (np.arange(width)[None, :] < n_sel)
    return jnp.asarray(ov, dtype=BF16)


def _attn_qlanes_body(q_ref, gt_ref, kc_ref, vct_ref, ks_ref, vst_ref, kw_ref, vwt_ref, ovt_ref, eblk_ref,
                      tri_ref, o_ref, qa_sc, oct_sc, m_sc, l_sc, acc_sc, s_sc,
                      *, t, tq, kc_len, n_cmp, n_sel):
    off = pl.program_id(1) * tq
    cols = GRP * tq
    lane = lax.broadcasted_iota(jnp.int32, (tq, LANES), 1)
    qpos = off + lax.broadcasted_iota(jnp.int32, (1, tq), 1)
    q = q_ref[0] * (SCALE * LOG2E)
    for h in range(N_HEADS):
        g, r = divmod(h, GRP)
        qa_sc[g, :KV_W, r * tq:(r + 1) * tq] = _head_to_group_lanes(q, h, lane).T.astype(BF16)

    def attend(jobs):
        stats = []
        for st, k, vt, rhs, bias in jobs:
            n = k.shape[0]
            s = _dot(k, rhs)
            if bias is not None:
                s = s + bias
            s_sc[st, :n, :] = s
            m_old = m_sc[st]
            m_new = jnp.maximum(m_old, jnp.max(s, axis=0, keepdims=True))
            m_sc[st] = m_new
            stats.append((m_new, jnp.exp2(m_old - m_new)))
        probs = []
        for (st, k, vt, rhs, bias), (m_new, alpha) in zip(jobs, stats):
            p = jnp.exp2(s_sc[st, :k.shape[0], :] - m_new)
            l_sc[st] = alpha * l_sc[st] + jnp.sum(p, axis=0, keepdims=True)
            probs.append(p.astype(BF16))
        for (st, k, vt, rhs, bias), (m_new, alpha), p in zip(jobs, stats, probs):
            acc_sc[st] = alpha * acc_sc[st] + _dot(vt, p)

    sel_state = lambda g: g
    win_state = lambda g: N_KV_HEADS + g
    m_sc[...] = jnp.full(m_sc.shape, NEG_INF, F32)
    l_sc[...] = jnp.zeros(l_sc.shape, F32)
    acc_sc[...] = jnp.zeros(acc_sc.shape, F32)

    n_chunk = kc_ref.shape[1]
    kcm = kc_ref[0].astype(BF16)
    vct = vct_ref[0].astype(BF16)
    nrow = lax.broadcasted_iota(jnp.int32, (n_chunk, tq), 0)
    mask_1 = jnp.logical_and(nrow * CMP_STRIDE + (CMP_LEN - 1) <= qpos, nrow < n_cmp)
    mask_c = jnp.concatenate([mask_1.astype(F32)] * GRP, axis=1) > 0.5
    psum = []
    for g in range(N_KV_HEADS):
        sc = jnp.where(mask_c, _dot(kcm, qa_sc[g, :KV_W, :]), NEG_INF)
        e = jnp.where(mask_c, jnp.exp2(sc - jnp.max(sc, axis=0, keepdims=True)), 0.0)
        pc = e / jnp.maximum(jnp.sum(e, axis=0, keepdims=True), 1e-30)
        oct_sc[g] = _dot(vct, pc.astype(BF16))
        psum.append(sum(pc[:, r * tq:(r + 1) * tq] for r in range(GRP)))

    n_sel_pad = -(-n_sel // 8) * 8
    jrow = lax.broadcasted_iota(jnp.int32, (n_sel_pad, tq), 0)
    cur = qpos // SEL_BLK
    forced = jnp.logical_or(jrow == 0, jnp.logical_or(jrow == cur, jrow == cur - 1))
    bonus = jnp.where(forced, FORCE_BONUS, 0.0)
    k_eff = min(TOPK, n_sel)
    ovt = ovt_ref[...]
    for g in range(N_KV_HEADS):
        hi = psum[g].astype(BF16)
        lo = (psum[g] - hi.astype(F32)).astype(BF16)
        imp = _dot(ovt, hi) + _dot(ovt, lo)
        score = jnp.where(jrow <= cur, imp + bonus, NEG_INF)
        rank = jnp.zeros((n_sel_pad, tq), F32)
        for j in range(n_sel):
            row = score[j:j + 1, :]
            tie = jnp.where(jrow > j, 1.0, 0.0)
            rank = rank + jnp.where(row > score, 1.0, jnp.where(row == score, tie, 0.0))
        chosen = jnp.logical_and(rank < k_eff, jrow <= cur)
        for part, keep in enumerate((jnp.logical_and(chosen, jrow < off // SEL_BLK), chosen)):
            r0 = KV_W + part * LANES
            pen = jnp.where(keep, 0.0, -MASK_PENALTY).astype(BF16)
            qa_sc[g, r0:r0 + n_sel_pad, :] = jnp.concatenate([pen] * GRP, axis=1)
            qa_sc[g, r0 + n_sel_pad:r0 + LANES, :] = jnp.zeros((LANES - n_sel_pad, cols), BF16)

    tq_off = pl.multiple_of(off, tq)
    tri_own = tri_ref[0]
    tri_front = tri_ref[1]

    k_own = jnp.concatenate(
        [ks_ref[0, pl.ds(tq_off, tq), :].astype(BF16), eblk_ref[pl.ds(tq_off, tq), :]], axis=1)
    vt_own = vst_ref[0, :, pl.ds(tq_off, tq)].astype(BF16)
    kw_own = kw_ref[0, pl.ds(tq_off, tq), :].astype(BF16)
    vwt_own = vwt_ref[0, :, pl.ds(tq_off, tq)].astype(BF16)
    groups = range(N_KV_HEADS)
    q_own = lambda g: jnp.concatenate([qa_sc[g, :KV_W, :], qa_sc[g, KV_W + LANES:, :]], axis=0)
    attend([(sel_state(g), k_own, vt_own, q_own(g), tri_own) for g in groups]
           + [(win_state(g), kw_own, vwt_own, qa_sc[g, :KV_W, :], tri_own) for g in groups])

    for ci in range(t // kc_len):
        @pl.when(ci * kc_len < off)
        def _(ci=ci):
            base = ci * kc_len
            k_aug = jnp.concatenate(
                [ks_ref[0, base:base + kc_len, :].astype(BF16), eblk_ref[base:base + kc_len, :]], axis=1)
            vt = vst_ref[0, :, base:base + kc_len].astype(BF16)
            attend([(sel_state(g), k_aug, vt, qa_sc[g, :KV_W + LANES, :], None) for g in groups])

    def window_keys(start, n, bias):
        start = pl.multiple_of(start, tq)
        kw = kw_ref[0, pl.ds(start, n), :].astype(BF16)
        vwt = vwt_ref[0, :, pl.ds(start, n)].astype(BF16)
        attend([(win_state(g), kw, vwt, qa_sc[g, :KV_W, :], bias) for g in groups])

    @pl.when(off >= WINDOW)
    def _():
        window_keys(off - WINDOW, tq, tri_front)
        window_keys(off - WINDOW + tq, WINDOW - tq, None)

    for i in range(1, WINDOW // tq):
        @pl.when(jnp.logical_and(off < WINDOW, off - i * tq >= 0))
        def _(i=i):
            window_keys(off - i * tq, tq, None)

    gtt = gt_ref[0].T
    outs = []
    for g in range(N_KV_HEADS):
        o_w = acc_sc[win_state(g)] / jnp.maximum(l_sc[win_state(g)], 1e-30)
        o_s = acc_sc[sel_state(g)] / jnp.maximum(l_sc[sel_state(g)], 1e-30)
        o_c = oct_sc[g]
        for r in range(GRP):
            h = g * GRP + r
            cs = slice(r * tq, (r + 1) * tq)
            o_h = (gtt[3 * h:3 * h + 1, :] * o_c[:, cs] + gtt[3 * h + 1:3 * h + 2, :] * o_s[:, cs]
                   + gtt[3 * h + 2:3 * h + 3, :] * o_w[:, cs])
            outs.append(o_h.T)
    o_ref[0] = jnp.concatenate(
        [_pair_tile(outs[2 * p], outs[2 * p + 1], 2 * p, lane) for p in range(N_HEADS // 2)], axis=-1)


def _attn_qlanes(q, gt, kcmp, vcmp_t, ks, vs_t, kw, vw_t, tq=256, kc_len=512):
    b, t, _ = q.shape
    n_chunk = kcmp.shape[1]
    n_cmp = n_chunk - N_SUB + 1
    n_sel = -(-t // SEL_BLK)
    assert t % kc_len == 0 and t >= WINDOW + tq and n_sel <= LANES and n_chunk % LANES == 0
    assert tq % LANES == 0 and WINDOW % tq == 0 and tq % SEL_BLK == 0
    n_sel_pad = -(-n_sel // 8) * 8
    ov_t = _cmp_to_sel(n_chunk, n_cmp, n_sel, n_sel_pad).T
    eblk = jnp.asarray(np.arange(t)[:, None] // SEL_BLK == np.arange(LANES)[None, :], dtype=BF16)
    kk, qq = np.arange(tq)[:, None], np.arange(tq)[None, :]
    tri = np.stack([np.where(kk <= qq, 0.0, -np.inf), np.where(kk > qq, 0.0, -np.inf)]).astype(np.float32)
    tri = jnp.asarray(np.tile(tri, (1, 1, GRP)))
    qblk = lambda i, j: (i, j, 0)
    full = lambda i, j: (i, 0, 0)
    const = lambda a: pl.BlockSpec(a.shape, lambda i, j: (0,) * a.ndim)
    tok = pl.BlockSpec((1, t, KV_W), full)
    tr = pl.BlockSpec((1, KV_W, t), full)
    cols = GRP * tq
    n_states = 2 * N_KV_HEADS
    assert kc_len >= WINDOW - tq
    body = functools.partial(_attn_qlanes_body, t=t, tq=tq, kc_len=kc_len, n_cmp=n_cmp, n_sel=n_sel)
    return pl.pallas_call(
        body, out_shape=jax.ShapeDtypeStruct((b, t, D_ATT), F32), grid=(b, t // tq),
        in_specs=[pl.BlockSpec((1, tq, D_ATT), qblk), pl.BlockSpec((1, tq, LANES), qblk),
                  pl.BlockSpec((1, n_chunk, KV_W), full), pl.BlockSpec((1, KV_W, n_chunk), full),
                  tok, tr, tok, tr, const(ov_t), const(eblk), const(tri)],
        out_specs=pl.BlockSpec((1, tq, D_ATT), qblk),
        scratch_shapes=[pltpu.VMEM((N_KV_HEADS, KV_W + 2 * LANES, cols), BF16),
                        pltpu.VMEM((N_KV_HEADS, KV_W, cols), F32), pltpu.VMEM((n_states, 1, cols), F32),
                        pltpu.VMEM((n_states, 1, cols), F32), pltpu.VMEM((n_states, KV_W, cols), F32),
                        pltpu.VMEM((n_states, kc_len, cols), F32)],
        compiler_params=_params(2), name="attn_prompt")(
            q, gt, kcmp, vcmp_t, ks, vs_t, kw, vw_t, ov_t, eblk, tri)


def _sample_heads(q_row):
    lane = lax.broadcasted_iota(jnp.int32, (1, LANES), 1)
    return jnp.concatenate([_head_to_group_lanes(q_row, h, lane) for h in range(N_HEADS)], axis=0)


def _attn_s1_body(q_ref, kc_ref, vc_ref, ov_ref, oc_ref, idx_ref, *, n_cmp, n_sel, q_pos):
    qh = _sample_heads(q_ref[0])
    n_chunk = kc_ref.shape[1]
    ncol = lax.broadcasted_iota(jnp.int32, (N_HEADS, n_chunk), 1)
    mask = jnp.logical_and(ncol * CMP_STRIDE + (CMP_LEN - 1) <= q_pos, ncol < n_cmp)
    pc = _masked_softmax(_dot_t(qh.astype(BF16), kc_ref[0].astype(BF16)) * SCALE, mask)
    oc_ref[0] = _dot(pc.astype(BF16), vc_ref[0].astype(BF16))
    rows = [jnp.sum(pc[g * GRP:(g + 1) * GRP, :], axis=0, keepdims=True) for g in range(N_KV_HEADS)]
    psum = jnp.concatenate(rows + [jnp.zeros((N_HEADS - N_KV_HEADS, n_chunk), F32)], axis=0)
    imp = _split_dot(psum, ov_ref[...])
    width = imp.shape[1]
    jl = lax.broadcasted_iota(jnp.int32, (N_HEADS, width), 1)
    cur = q_pos // SEL_BLK
    forced = jnp.logical_or(jl == 0, jnp.logical_or(jl == cur, jl == cur - 1))
    score = jnp.where(jl <= cur, imp + jnp.where(forced, FORCE_BONUS, 0.0), NEG_INF)
    ii = lax.broadcasted_iota(jnp.int32, (width, width), 0)
    jj = lax.broadcasted_iota(jnp.int32, (width, width), 1)
    slot = lax.broadcasted_iota(jnp.int32, (width, LANES), 1).astype(F32)
    blk = lax.broadcasted_iota(jnp.int32, (width, LANES), 0).astype(F32)
    tie = jnp.where(jj < ii, 1.0, 0.0)
    out_rows = []
    for g in range(N_KV_HEADS):
        row = jnp.broadcast_to(score[g:g + 1, :], (width, width))
        col = jnp.sum(jnp.where(ii == jj, row, 0.0), axis=1, keepdims=True)
        beats = jnp.where(row > col, 1.0, jnp.where(row == col, tie, 0.0))
        rank = jnp.sum(beats, axis=1, keepdims=True)
        out_rows.append(jnp.sum(jnp.where(rank == slot, blk, 0.0), axis=0, keepdims=True))
    out_rows.append(jnp.zeros((N_HEADS - N_KV_HEADS, LANES), F32))
    idx_ref[0] = jnp.concatenate(out_rows, axis=0).astype(jnp.int32)


def _attn_s1(q, kcmp, vcmp, q_pos, n_sel):
    b = q.shape[0]
    n_chunk = kcmp.shape[1]
    n_cmp = n_chunk - N_SUB + 1
    width = -(-n_sel // LANES) * LANES
    ov = _cmp_to_sel(n_chunk, n_cmp, n_sel, width)
    blk3 = lambda i: (i, 0, 0)
    cm = pl.BlockSpec((1, n_chunk, KV_W), blk3)
    out = pl.BlockSpec((1, N_HEADS, LANES), blk3)
    return pl.pallas_call(
        functools.partial(_attn_s1_body, n_cmp=n_cmp, n_sel=n_sel, q_pos=q_pos),
        out_shape=[jax.ShapeDtypeStruct((b, N_HEADS, LANES), F32),
                   jax.ShapeDtypeStruct((b, N_HEADS, LANES), jnp.int32)],
        grid=(b,),
        in_specs=[pl.BlockSpec((1, 1, D_ATT), blk3), cm, cm, pl.BlockSpec(ov.shape, lambda i: (0, 0))],
        out_specs=[out, out], compiler_params=_params(1), name="attn_sample_select")(
            q.reshape(b, 1, D_ATT), kcmp, vcmp, ov)


def _extra_key_attention(qh, k_t, v_t, mask, k_new, v_new):
    qf = qh.astype(F32)
    s = jnp.where(mask, _dot(qh, k_t.astype(BF16)) * SCALE, NEG_INF)
    s_new = jnp.sum(qf * k_new, axis=-1, keepdims=True) * SCALE
    m = jnp.maximum(jnp.max(s, axis=-1, keepdims=True), s_new)
    e = jnp.where(mask, jnp.exp(s - m), 0.0)
    e_new = jnp.exp(s_new - m)
    denom = jnp.maximum(jnp.sum(e, axis=-1, keepdims=True) + e_new, 1e-30)
    return (_dot_t(e.astype(BF16), v_t.astype(BF16)) + e_new * v_new) / denom


def _attn_s2_body(idx_ref, pt_ref, q_ref, gt_ref, oc_ref, ksn_ref, vsn_ref, kwn_ref, vwn_ref,
                  kws_ref, vws_ref, kcache, vcache, o_ref, kbuf, vbuf, sem,
                  *, n_pages, n_batch, n_cache_blk, q_pos):
    b = pl.program_id(0)
    slot = b % 2
    page_len = kcache.shape[2]
    per_page = page_len // SEL_BLK

    def block_id(seq, g, k):
        return idx_ref[(seq * N_KV_HEADS + g) * TOPK + k]

    def copies(seq, sl, g, k):
        blk = jnp.minimum(block_id(seq, g, k), n_cache_blk - 1)
        page = pt_ref[seq * n_pages + blk // per_page]
        dst = pl.ds(k * page_len, page_len)
        return (pltpu.make_async_copy(kcache.at[page], kbuf.at[sl, g, :, dst], sem.at[sl, 0]),
                pltpu.make_async_copy(vcache.at[page], vbuf.at[sl, g, :, dst], sem.at[sl, 1]))

    def for_all_copies(seq, sl, fn):
        for g in range(N_KV_HEADS):
            for k in range(TOPK):
                for cp in copies(seq, sl, g, k):
                    fn(cp)

    @pl.when(b == 0)
    def _():
        for_all_copies(0, 0, lambda cp: cp.start())

    @pl.when(b + 1 < n_batch)
    def _():
        for_all_copies(b + 1, 1 - slot, lambda cp: cp.start())

    for_all_copies(b, slot, lambda cp: cp.wait())
    blks = [[block_id(b, g, k) for k in range(TOPK)] for g in range(N_KV_HEADS)]

    qh = _sample_heads(q_ref[0]).astype(BF16)
    n_keys = TOPK * page_len
    row_grp = lax.broadcasted_iota(jnp.int32, (N_HEADS, n_keys), 0) // GRP
    key = lax.broadcasted_iota(jnp.int32, (N_HEADS, n_keys), 1)
    key_slot = key // page_len
    blk_in_page = (key % page_len) // SEL_BLK
    keep = jnp.zeros((N_HEADS, n_keys), F32)
    for g in range(N_KV_HEADS):
        for k in range(TOPK):
            blk = blks[g][k]
            in_cache = jnp.where(blk < n_cache_blk, 1.0, 0.0)
            hit = jnp.where(blk_in_page == blk % per_page, in_cache, 0.0)
            keep = jnp.where(jnp.logical_and(row_grp == g, key_slot == k), hit, keep)
    mask_s = keep > 0.5
    o_sel = [_extra_key_attention(qh, kbuf.at[slot, g][...], vbuf.at[slot, g][...], mask_s,
                                  ksn_ref[0], vsn_ref[0])
             for g in range(N_KV_HEADS)]
    out_grp = lax.broadcasted_iota(jnp.int32, (N_HEADS, LANES), 0) // GRP
    o_s = jnp.where(out_grp == 0, o_sel[0], o_sel[1])

    n_win = kws_ref.shape[2]
    kpos = (q_pos - n_win) + lax.broadcasted_iota(jnp.int32, (N_HEADS, n_win), 1)
    dist = q_pos - kpos
    mask_w = jnp.logical_and(dist >= 0, dist < WINDOW)
    o_w = _extra_key_attention(qh, kws_ref[0], vws_ref[0], mask_w, kwn_ref[0], vwn_ref[0])

    lane8 = lax.broadcasted_iota(jnp.int32, (N_HEADS, LANES), 1)
    head8 = lax.broadcasted_iota(jnp.int32, (N_HEADS, LANES), 0)
    gt = jnp.broadcast_to(gt_ref[0], (N_HEADS, LANES))
    gate = [jnp.sum(jnp.where(lane8 == 3 * head8 + j, gt, 0.0), axis=-1, keepdims=True) for j in range(3)]
    o = gate[0] * oc_ref[0] + gate[1] * o_s + gate[2] * o_w
    lane = lax.broadcasted_iota(jnp.int32, (1, LANES), 1)
    o_ref[0] = jnp.concatenate(
        [_pair_tile(o[2 * p:2 * p + 1, :], o[2 * p + 1:2 * p + 2, :], 2 * p, lane)
         for p in range(N_HEADS // 2)], axis=-1)


def _attn_s2(idx, page_table, q, gt, oc, ks_new, vs_new, kw_new, vw_new, kw_state_t, vw_state_t,
             cache_k_t, cache_v_t, q_pos):
    b, n_pages = page_table.shape
    page_len = cache_k_t.shape[2]
    n_win = kw_state_t.shape[2]
    assert n_win <= WINDOW
    blk3 = lambda i, *_: (i, 0, 0)
    row = lambda w: pl.BlockSpec((1, 1, w), blk3)
    grid_spec = pltpu.PrefetchScalarGridSpec(
        num_scalar_prefetch=2, grid=(b,),
        in_specs=[row(D_ATT), row(LANES), pl.BlockSpec((1, N_HEADS, LANES), blk3),
                  row(KV_W), row(KV_W), row(KV_W), row(KV_W),
                  pl.BlockSpec((1, KV_W, n_win), blk3), pl.BlockSpec((1, KV_W, n_win), blk3),
                  pl.BlockSpec(memory_space=pl.ANY), pl.BlockSpec(memory_space=pl.ANY)],
        out_specs=row(D_ATT),
        scratch_shapes=[pltpu.VMEM((2, N_KV_HEADS, KV_W, TOPK * page_len), F32),
                        pltpu.VMEM((2, N_KV_HEADS, KV_W, TOPK * page_len), F32),
                        pltpu.SemaphoreType.DMA((2, 2))])
    r3 = lambda a: a.reshape(b, 1, a.shape[-1])
    body = functools.partial(_attn_s2_body, n_pages=n_pages, n_batch=b,
                             n_cache_blk=n_pages * (page_len // SEL_BLK),
                             q_pos=q_pos)
    return pl.pallas_call(
        body, out_shape=jax.ShapeDtypeStruct((b, 1, D_ATT), F32), grid_spec=grid_spec,
        compiler_params=_params(1), name="attn_sample")(
            idx.reshape(-1), page_table.reshape(-1), r3(q), r3(gt), oc, r3(ks_new), r3(vs_new),
            r3(kw_new), r3(vw_new), kw_state_t, vw_state_t, cache_k_t, cache_v_t).reshape(b, D_ATT)


def _prompt_layer(x, p, tm=512, tm_proj=1024):
    b, t, d = x.shape
    x2 = _ffn(x.reshape(b * t, d), p['ffn1_norm'], p['ffn1_w_in'], p['ffn1_w_out'], None, tm)
    u, q, gt, kc, vc, ks, kw, kc_t, vc_t, ks_t, vs_t, kw_t, vw_t = _proj(
        x2, p, _rope_tables(np.arange(t)), tm_proj, seq_len=t)
    r3 = lambda a: a.reshape(b, t, a.shape[-1])
    u3 = r3(u)
    cn = _conv_prompt(u3, p)
    kcmp, vcmp_t = _cmp_prompt(r3(kc), r3(vc), p)
    o = _attn_qlanes(r3(q), r3(gt), kcmp, vcmp_t, r3(ks), vs_t, r3(kw), vw_t)
    y = _ffn(x2, p['ffn2_norm'], p['ffn2_w_in'], p['ffn2_w_out'], p['final_norm'], tm,
             mix=(cn.reshape(b * t, C_CONV), o.reshape(b * t, D_ATT), p['out_norm_attn'], p['w_out']))
    r4 = lambda a: a.reshape(b, N_KV_HEADS, HEAD_DIM, a.shape[-1]).transpose(0, 3, 1, 2)
    keep = min(WINDOW, t)
    state = (r4(kc_t), r4(vc_t), r4(ks_t), r4(vs_t), r4(kw_t[:, :, t - keep:]), r4(vw_t[:, :, t - keep:]),
             u3[:, t - (CONV_W - 1):])
    return y.reshape(b, t, d), state


def _sample_layer(x, p, cache_k_cmp, cache_v_cmp, cache_k_sel, cache_v_sel, kw_state, vw_state,
                  conv_state, page_table):
    b, t, d = x.shape
    assert t == 1
    past_len = page_table.shape[1] * cache_k_cmp.shape[1]
    x2 = _ffn(x.reshape(b, d), p['ffn1_norm'], p['ffn1_w_in'], p['ffn1_w_out'], None, b)
    tabs = _rope_tables(np.full((b,), past_len))
    u, q, gt, kc, vc, ks, vs, kw, vw = _proj(x2, p, tabs, b)
    cn = _conv_sample(conv_state, u, p)
    keys_on_lanes = lambda a: a.transpose(0, 2, 3, 1).reshape(a.shape[0], KV_W, a.shape[1])
    kcmp = _cmp_sample(page_table, keys_on_lanes(cache_k_cmp), p, True)
    vcmp = _cmp_sample(page_table, keys_on_lanes(cache_v_cmp), p, False)
    n_sel = -(-(past_len + 1) // SEL_BLK)
    oc, idx = _attn_s1(q, kcmp, vcmp, past_len, n_sel)
    n_win = kw_state.shape[1]
    o = _attn_s2(idx[:, :N_KV_HEADS, :TOPK], page_table, q, gt, oc, ks, vs, kw, vw,
                 keys_on_lanes(kw_state), keys_on_lanes(vw_state),
                 keys_on_lanes(cache_k_sel), keys_on_lanes(cache_v_sel), past_len)
    y = _ffn(x2, p['ffn2_norm'], p['ffn2_w_in'], p['ffn2_w_out'], p['final_norm'], b,
             mix=(cn, o, p['out_norm_attn'], p['w_out']))
    r4 = lambda a: a.reshape(b, 1, N_KV_HEADS, HEAD_DIM)
    keep = min(WINDOW, n_win + 1)
    new_kw = jnp.concatenate([kw_state, r4(kw)], axis=1)[:, n_win + 1 - keep:]
    new_vw = jnp.concatenate([vw_state, r4(vw)], axis=1)[:, n_win + 1 - keep:]
    new_conv = jnp.concatenate([conv_state, u[:, None, :]], axis=1)[:, 1:]
    return y.reshape(b, 1, d), (r4(kc), r4(vc), r4(ks), r4(vs), new_kw, new_vw, new_conv)


_PARAM_NAMES = ('ffn1_norm', 'ffn1_w_in', 'ffn1_w_out', 'mix_norm', 'w_in', 'conv_w', 'conv_b', 'conv_ln_g',
                'conv_ln_b', 'q_norm', 'k_cmp_norm', 'k_sel_norm', 'k_win_norm', 'cmp_k_pos', 'cmp_k_w1',
                'cmp_k_w2', 'cmp_v_pos', 'cmp_v_w1', 'cmp_v_w2', 'out_norm_conv', 'out_norm_attn', 'w_out',
                'ffn2_norm', 'ffn2_w_in', 'ffn2_w_out', 'final_norm')


def kernel(x_prompt, x_sample, cache_k_cmp, cache_v_cmp, cache_k_sel, cache_v_sel, state_k_win, state_v_win,
           state_conv, page_table, ffn1_norm, ffn1_w_in, ffn1_w_out, mix_norm, w_in, conv_w, conv_b, conv_ln_g,
           conv_ln_b, q_norm, k_cmp_norm, k_sel_norm, k_win_norm, cmp_k_pos, cmp_k_w1, cmp_k_w2, cmp_v_pos,
           cmp_v_w1, cmp_v_w2, out_norm_conv, out_norm_attn, w_out, ffn2_norm, ffn2_w_in, ffn2_w_out, final_norm):
    stacked = dict(zip(_PARAM_NAMES, (
        ffn1_norm, ffn1_w_in, ffn1_w_out, mix_norm, w_in, conv_w, conv_b, conv_ln_g, conv_ln_b, q_norm,
        k_cmp_norm, k_sel_norm, k_win_norm, cmp_k_pos, cmp_k_w1, cmp_k_w2, cmp_v_pos, cmp_v_w1, cmp_v_w2,
        out_norm_conv, out_norm_attn, w_out, ffn2_norm, ffn2_w_in, ffn2_w_out, final_norm)))
    depth = ffn1_norm.shape[0]
    yp, ys = x_prompt, x_sample
    new_p, new_s = [], []
    for l in range(depth):
        p = {k: v[l] for k, v in stacked.items()}
        yp, st_p = _prompt_layer(yp, p)
        new_p.append(st_p)
        ys, st_s = _sample_layer(ys, p, cache_k_cmp[l], cache_v_cmp[l], cache_k_sel[l], cache_v_sel[l],
                                 state_k_win[l], state_v_win[l], state_conv[l], page_table)
        new_s.append(st_s)
    outs_p = [jnp.stack(tup) for tup in zip(*new_p)]
    outs_s = [jnp.stack(tup) for tup in zip(*new_s)]
    return (yp, ys, *outs_p, *outs_s)
```

```python
import functools

import numpy as np
import jax
import jax.numpy as jnp
from jax import lax
from jax.experimental import pallas as pl
from jax.experimental.pallas import tpu as pltpu

F32 = jnp.float32
BF16 = jnp.bfloat16

HEAD_DIM = 64
N_HEADS = 8
N_KV_HEADS = 2
GRP = N_HEADS // N_KV_HEADS
KV_W = N_KV_HEADS * HEAD_DIM
C_CONV = 512
D_ATT = 512
CONV_W = 31
CMP_LEN = 32
CMP_STRIDE = 16
N_SUB = CMP_LEN // CMP_STRIDE
CMP_HID = 2 * HEAD_DIM
SEL_BLK = 64
TOPK = 16
WINDOW = 512
ROT_DIM = HEAD_DIM // 4
ROT_HALF = ROT_DIM // 2
ROPE_THETA = 500000.0
NEG_INF = -1e30
MASK_PENALTY = 2.0 ** 100
FORCE_BONUS = 1e4
EPS = 1e-6
SCALE = HEAD_DIM ** -0.5
LOG2E = 1.4426950408889634
LANES = 128
CHUNK_W = CMP_STRIDE * KV_W
VMEM_LIMIT = 56 * 1024 * 1024


def _params(n_grid_dims):
    return pltpu.CompilerParams(
        dimension_semantics=("arbitrary",) * n_grid_dims, vmem_limit_bytes=VMEM_LIMIT)


def _dot(a, b):
    return jnp.dot(a, b, preferred_element_type=F32)


def _dot_t(a, b):
    return lax.dot_general(a, b, (((1,), (1,)), ((), ())), preferred_element_type=F32)


def _split_dot(x, m_bf16):
    hi = x.astype(BF16)
    lo = (x - hi.astype(F32)).astype(BF16)
    return _dot(hi, m_bf16) + _dot(lo, m_bf16)


def _sigmoid(x):
    return 1.0 / (1.0 + jnp.exp(-x))


def _rms(x, g):
    return x * lax.rsqrt(jnp.mean(x * x, axis=-1, keepdims=True) + EPS) * g


def _group_rms(x, g_tiled, ones_blockdiag):
    ss = _split_dot(x * x, ones_blockdiag)
    return x * lax.rsqrt(ss * (1.0 / HEAD_DIM) + EPS) * g_tiled


def _rope(x, c, sa, sb):
    n = x.shape[-1]
    return x * c + pltpu.roll(x, n - ROT_HALF, 1) * sa + pltpu.roll(x, ROT_HALF, 1) * sb


def _tile_lanes(t, reps):
    return jnp.concatenate([t] * reps, axis=-1) if reps > 1 else t


def _masked_softmax(s, mask):
    s = jnp.where(mask, s, NEG_INF)
    m = jnp.max(s, axis=-1, keepdims=True)
    e = jnp.where(mask, jnp.exp(s - m), 0.0)
    return e / jnp.maximum(jnp.sum(e, axis=-1, keepdims=True), 1e-30)


def _head_to_group_lanes(q, h, lane):
    g, p = h // GRP, h // 2
    qp = q[:, LANES * p:LANES * (p + 1)]
    src = qp if (h % 2) == g else pltpu.roll(qp, HEAD_DIM, 1)
    keep = (lane < HEAD_DIM) if g == 0 else (lane >= HEAD_DIM)
    return jnp.where(keep, src, 0.0)


def _pair_tile(o_even, o_odd, h_even, lane):
    g = h_even // GRP
    a = o_even if g == 0 else pltpu.roll(o_even, HEAD_DIM, 1)
    b = pltpu.roll(o_odd, HEAD_DIM, 1) if g == 0 else o_odd
    return jnp.where(lane < HEAD_DIM, a, b)


def _ffn_body(*refs, n_chunk, tf, ff, final, mixed):
    refs = list(refs)
    x_ref = refs.pop(0)
    x = x_ref[...]
    if mixed:
        c_ref, a_ref, ag_ref, wc_ref, wa_ref = refs[:5]
        refs = refs[5:]
        x = (x + _dot(c_ref[...].astype(BF16), wc_ref[...])
             + _dot(_rms(a_ref[...], ag_ref[...]).astype(BF16), wa_ref[...]))
    g_ref, wi_ref, wo_ref = refs[:3]
    fg_ref = refs[3] if final else None
    o_ref = refs[-1]
    xn = _rms(x, g_ref[...]).astype(BF16)
    acc = jnp.zeros(x.shape, F32)
    for c in range(n_chunk):
        a = _dot(xn, wi_ref[:, c * tf:(c + 1) * tf])
        b = _dot(xn, wi_ref[:, ff + c * tf:ff + (c + 1) * tf])
        h = (a * _sigmoid(a)) * b
        acc = acc + _dot(h.astype(BF16), wo_ref[c * tf:(c + 1) * tf, :])
    y = x + 0.5 * acc
    if final:
        y = _rms(y, fg_ref[...])
    o_ref[...] = y


def _ffn(x2, norm_g, w_in, w_out, final_g, tm, mix=None):
    rows, d = x2.shape
    ff = w_out.shape[0]
    tf = 256
    assert rows % tm == 0 and ff % tf == 0 and w_in.shape == (d, 2 * ff)
    const = lambda i: (0, 0)
    row = lambda i: (i, 0)
    single = pl.Buffered(1)
    in_specs = [pl.BlockSpec((tm, d), row)]
    args = [x2]
    if mix is not None:
        cn, att, att_g, w_mix = mix
        w_mix = w_mix.astype(BF16)
        in_specs += [pl.BlockSpec((tm, C_CONV), row), pl.BlockSpec((tm, D_ATT), row),
                     pl.BlockSpec((1, D_ATT), const), pl.BlockSpec((C_CONV, d), const),
                     pl.BlockSpec((D_ATT, d), const)]
        args += [cn, att, att_g.reshape(1, D_ATT), w_mix[:C_CONV], w_mix[C_CONV:]]
    in_specs += [
        pl.BlockSpec((1, d), const),
        pl.BlockSpec((d, 2 * ff), const, pipeline_mode=single),
        pl.BlockSpec((ff, d), const, pipeline_mode=single),
    ]
    args += [norm_g.reshape(1, d), w_in.astype(BF16), w_out.astype(BF16)]
    if final_g is not None:
        in_specs.append(pl.BlockSpec((1, d), const))
        args.append(final_g.reshape(1, d))
    body = functools.partial(_ffn_body, n_chunk=ff // tf, tf=tf, ff=ff, final=final_g is not None,
                             mixed=mix is not None)
    return pl.pallas_call(
        body, out_shape=jax.ShapeDtypeStruct((rows, d), F32), grid=(rows // tm,),
        in_specs=in_specs, out_specs=pl.BlockSpec((tm, d), lambda i: (i, 0)),
        compiler_params=_params(1), name="ffn")(*args)


_SEG_GLU = 2 * C_CONV
_SEG_Q = _SEG_GLU + D_ATT
N_IN = _SEG_Q + 6 * KV_W + 3 * N_HEADS
N_IN_PAD = _SEG_Q + 7 * KV_W


def _proj_body(x_ref, g_ref, w_ref, qg_ref, ksg_ref, kwg_ref, bd4_ref, bd1_ref, c_ref, sa_ref, sb_ref,
               u_ref, q_ref, gt_ref, kc_ref, vc_ref, *kv_refs, transposed):
    xn = _rms(x_ref[...], g_ref[...]).astype(BF16)

    def seg(lo, width):
        return _dot(xn, w_ref[:, lo:lo + width])

    c, sa, sb = c_ref[...], sa_ref[...], sb_ref[...]
    reps = D_ATT // LANES
    glu = seg(0, _SEG_GLU)
    u_ref[...] = glu[:, :C_CONV] * _sigmoid(glu[:, C_CONV:])
    q = _group_rms(seg(_SEG_GLU, D_ATT), qg_ref[...], bd4_ref[...])
    q_ref[...] = _rope(q, _tile_lanes(c, reps), _tile_lanes(sa, reps), _tile_lanes(sb, reps))
    base = _SEG_Q
    kc = seg(base, KV_W)
    vc = seg(base + KV_W, KV_W)
    ks = _rope(_group_rms(seg(base + 2 * KV_W, KV_W), ksg_ref[...], bd1_ref[...]), c, sa, sb)
    vs = seg(base + 3 * KV_W, KV_W)
    kw = _rope(_group_rms(seg(base + 4 * KV_W, KV_W), kwg_ref[...], bd1_ref[...]), c, sa, sb)
    vw = seg(base + 5 * KV_W, KV_W)
    gt_ref[...] = _sigmoid(seg(base + 6 * KV_W, KV_W))
    kc_ref[...] = kc
    vc_ref[...] = vc
    if transposed:
        kv_refs[0][...] = ks
        kv_refs[1][...] = kw
        for ref, val in zip(kv_refs[2:], (kc, vc, ks, vs, kw, vw)):
            ref[0] = val.T
    else:
        for ref, val in zip(kv_refs, (ks, vs, kw, vw)):
            ref[...] = val


def _blockdiag_ones(width):
    idx = np.arange(width) // HEAD_DIM
    return jnp.asarray(idx[:, None] == idx[None, :], dtype=BF16)


def _rope_tables(pos):
    pos = np.asarray(pos, dtype=np.float64)
    inv = ROPE_THETA ** (-(np.arange(ROT_HALF, dtype=np.float64) * 2.0 / ROT_DIM))
    ang = pos[:, None] * inv[None, :]
    cos, sin = np.cos(ang).astype(np.float32), np.sin(ang).astype(np.float32)
    n = pos.shape[0]
    ones = np.ones((n, HEAD_DIM - ROT_DIM), np.float32)
    zeros = np.zeros((n, HEAD_DIM - ROT_DIM), np.float32)
    z8 = np.zeros((n, ROT_HALF), np.float32)
    c = np.concatenate([cos, cos, ones], axis=1)
    sa = np.concatenate([-sin, z8, zeros], axis=1)
    sb = np.concatenate([z8, sin, zeros], axis=1)
    return tuple(jnp.asarray(np.tile(t, (1, N_KV_HEADS))) for t in (c, sa, sb))


def _proj(x2, p, tabs, tm, seq_len=None):
    rows, d = x2.shape
    n_tab = tabs[0].shape[0] // tm
    w = jnp.pad(p['w_in'], ((0, 0), (0, N_IN_PAD - N_IN))).astype(BF16)
    const = lambda i: (0, 0)
    row = lambda i: (i, 0)
    tab = lambda i: (i % n_tab, 0)
    in_specs = [
        pl.BlockSpec((tm, d), row), pl.BlockSpec((1, d), const),
        pl.BlockSpec((d, N_IN_PAD), const, pipeline_mode=pl.Buffered(1)),
        pl.BlockSpec((1, D_ATT), const), pl.BlockSpec((1, KV_W), const), pl.BlockSpec((1, KV_W), const),
        pl.BlockSpec((D_ATT, D_ATT), const), pl.BlockSpec((KV_W, KV_W), const),
        pl.BlockSpec((tm, LANES), tab), pl.BlockSpec((tm, LANES), tab), pl.BlockSpec((tm, LANES), tab),
    ]
    widths = [C_CONV, D_ATT, LANES, KV_W, KV_W] + [KV_W] * (4 if seq_len is None else 2)
    out_shape = [jax.ShapeDtypeStruct((rows, wd), F32) for wd in widths]
    out_specs = [pl.BlockSpec((tm, wd), row) for wd in widths]
    if seq_len is not None:
        nt = seq_len // tm
        assert seq_len % tm == 0 and rows % seq_len == 0
        out_shape += [jax.ShapeDtypeStruct((rows // seq_len, KV_W, seq_len), F32)] * 6
        out_specs += [pl.BlockSpec((1, KV_W, tm), lambda i: (i // nt, 0, i % nt))] * 6
    args = [x2, p['mix_norm'].reshape(1, d), w,
            jnp.tile(p['q_norm'], N_HEADS).reshape(1, D_ATT),
            jnp.tile(p['k_sel_norm'], N_KV_HEADS).reshape(1, KV_W),
            jnp.tile(p['k_win_norm'], N_KV_HEADS).reshape(1, KV_W),
            _blockdiag_ones(D_ATT), _blockdiag_ones(KV_W), *tabs]
    return pl.pallas_call(
        functools.partial(_proj_body, transposed=seq_len is not None), out_shape=out_shape,
        grid=(rows // tm,), in_specs=in_specs, out_specs=out_specs,
        compiler_params=_params(1), name="proj")(*args)


def _conv_post(acc, lg, lb, og):
    mu = jnp.mean(acc, axis=-1, keepdims=True)
    xc = acc - mu
    var = jnp.mean(xc * xc, axis=-1, keepdims=True)
    y = xc * lax.rsqrt(var + EPS) * lg + lb
    y = y * _sigmoid(y)
    return _rms(y, og)


def _conv_body(u_ref, cw_ref, cb_ref, lg_ref, lb_ref, og_ref, o_ref, *, n_tiles, tt):
    halo = 32
    lead = halo - (CONV_W - 1)

    def tile(win, t0):
        acc = jnp.zeros((tt, C_CONV), F32) + cb_ref[...]
        for r in range(8):
            sh = win if r == 0 else pltpu.roll(win, tt + halo - r, 0)
            for a in range(halo // 8 + 1):
                w = 8 * a + r - lead
                if 0 <= w < CONV_W:
                    acc = acc + sh[8 * a:8 * a + tt, :] * cw_ref[w:w + 1, :]
        o_ref[0, pl.ds(t0, tt), :] = _conv_post(acc, lg_ref[...], lb_ref[...], og_ref[...])

    tile(jnp.concatenate([jnp.zeros((halo, C_CONV), F32), u_ref[0, 0:tt, :]], axis=0), 0)

    def step(i, carry):
        t0 = pl.multiple_of(i * tt, tt)
        tile(u_ref[0, pl.ds(pl.multiple_of(t0 - halo, halo), tt + halo), :], t0)
        return carry

    lax.fori_loop(1, n_tiles, step, 0)


def _conv_prompt(u, p, tt=256):
    b, t, c = u.shape
    cw = jnp.pad(p['conv_w'], ((0, 1), (0, 0)))
    const = lambda i: (0, 0)
    vec = lambda a: a.reshape(1, c)
    blk = pl.BlockSpec((1, t, c), lambda i: (i, 0, 0))
    return pl.pallas_call(
        functools.partial(_conv_body, n_tiles=t // tt, tt=tt),
        out_shape=jax.ShapeDtypeStruct((b, t, c), F32), grid=(b,),
        in_specs=[blk, pl.BlockSpec((32, c), const)] + [pl.BlockSpec((1, c), const)] * 4,
        out_specs=blk, compiler_params=_params(1), name="conv_prompt")(
            u, cw, vec(p['conv_b']), vec(p['conv_ln_g']), vec(p['conv_ln_b']), vec(p['out_norm_conv']))


def _conv_sample_body(st_ref, u_ref, cw_ref, cb_ref, lg_ref, lb_ref, og_ref, o_ref):
    acc = u_ref[...] * cw_ref[CONV_W - 1:CONV_W, :] + cb_ref[...]
    for w in range(CONV_W - 1):
        acc = acc + st_ref[w] * cw_ref[w:w + 1, :]
    o_ref[...] = _conv_post(acc, lg_ref[...], lb_ref[...], og_ref[...])


def _conv_sample(state, u, p):
    b = u.shape[0]
    c = C_CONV
    cw = jnp.pad(p['conv_w'], ((0, 1), (0, 0)))
    vec = lambda a: a.reshape(1, c)
    return pl.pallas_call(
        _conv_sample_body, out_shape=jax.ShapeDtypeStruct((b, c), F32),
        compiler_params=pltpu.CompilerParams(vmem_limit_bytes=VMEM_LIMIT), name="conv_sample")(
            state.transpose(1, 0, 2), u, cw, vec(p['conv_b']), vec(p['conv_ln_g']),
            vec(p['conv_ln_b']), vec(p['out_norm_conv']))


def _compress_weights(pe, w1, w2):
    assert N_SUB == 2
    w1r = w1.reshape(N_SUB, CMP_STRIDE, HEAD_DIM, CMP_HID).transpose(1, 2, 0, 3)
    eye = jnp.eye(N_KV_HEADS, dtype=w1.dtype)
    wf = jnp.einsum('sdmh,gk->sgdmkh', w1r, eye).reshape(CHUNK_W, N_SUB * N_KV_HEADS * CMP_HID)
    per = jnp.broadcast_to(pe.reshape(N_SUB, CMP_STRIDE, 1, HEAD_DIM),
                           (N_SUB, CMP_STRIDE, N_KV_HEADS, HEAD_DIM)).reshape(N_SUB, CHUNK_W)
    w2f = jnp.einsum('hd,gk->ghkd', w2, eye).reshape(N_KV_HEADS * CMP_HID, KV_W)
    return per, wf.astype(BF16), w2f.astype(BF16)


def _position_pair_weights(per, wf):
    half = N_KV_HEADS * CMP_HID
    pairs = CMP_STRIDE // 2
    pe_p = per.reshape(N_SUB * pairs, 2 * KV_W)
    w_p = wf.reshape(pairs, 2 * KV_W, N_SUB, half).transpose(2, 0, 1, 3).reshape(N_SUB * pairs, 2 * KV_W, half)
    return pe_p, w_p


def _compress_tail(a0, a1, w2_ref):
    n = a0.shape[0]
    h = a0 + pltpu.roll(a1, n - 1, 0)
    t = h * _sigmoid(h)
    tok = _dot(t.astype(BF16), w2_ref[...])
    return jnp.where(lax.broadcasted_iota(jnp.int32, tok.shape, 0) < n - 1, tok, 0.0)


def _cmp_key_post(tok, kng, bd1, c, sa, sb):
    return _rope(_group_rms(tok, kng[...], bd1[...]), c[...], sa[...], sb[...])


def _first_layer(rows_ref, row0, n_chunk, pe_ref, w1_ref):
    half = N_KV_HEADS * CMP_HID
    pairs = CMP_STRIDE // 2
    a0 = jnp.zeros((n_chunk, half), F32)
    a1 = jnp.zeros((n_chunk, half), F32)
    for sp in range(pairs):
        xs = jnp.concatenate(
            [rows_ref[pl.ds(row0 + 2 * sp + e, n_chunk, stride=CMP_STRIDE), :] for e in range(2)], axis=1)
        a0 = a0 + _dot((xs + pe_ref[sp:sp + 1, :]).astype(BF16), w1_ref[sp])
        a1 = a1 + _dot((xs + pe_ref[pairs + sp:pairs + sp + 1, :]).astype(BF16), w1_ref[pairs + sp])
    return a0, a1


def _cmp_prompt_body(xk_ref, xv_ref, pek, w1k, w2k, kng, bd1, c, sa, sb, pev, w1v, w2v, ok_ref, ov_ref):
    n_chunk = ok_ref.shape[1]
    tok_k = _compress_tail(*_first_layer(xk_ref.at[0], 0, n_chunk, pek, w1k), w2k)
    tok_v = _compress_tail(*_first_layer(xv_ref.at[0], 0, n_chunk, pev, w1v), w2v)
    ok_ref[0] = _cmp_key_post(tok_k, kng, bd1, c, sa, sb)
    ov_ref[0] = tok_v.T


def _cmp_weights(p, name):
    per, wf, w2f = _compress_weights(p[f'cmp_{name}_pos'], p[f'cmp_{name}_w1'], p[f'cmp_{name}_w2'])
    return [*_position_pair_weights(per, wf), w2f]


def _cmp_prompt(kc, vc, p):
    b, t, _ = kc.shape
    n_chunk = t // CMP_STRIDE
    args = [*_cmp_weights(p, 'k'), *_cmp_key_consts(p, n_chunk), *_cmp_weights(p, 'v')]
    blk = pl.BlockSpec((1, t, KV_W), lambda i: (i, 0, 0))
    b3 = lambda i: (i, 0, 0)
    const = lambda a: pl.BlockSpec(a.shape, lambda i: (0,) * a.ndim)
    return pl.pallas_call(
        _cmp_prompt_body,
        out_shape=[jax.ShapeDtypeStruct((b, n_chunk, KV_W), F32), jax.ShapeDtypeStruct((b, KV_W, n_chunk), F32)],
        grid=(b,), in_specs=[blk, blk] + [const(a) for a in args],
        out_specs=[pl.BlockSpec((1, n_chunk, KV_W), b3), pl.BlockSpec((1, KV_W, n_chunk), b3)],
        compiler_params=_params(1), name="cmp_prompt")(kc, vc, *args)


def _cmp_key_consts(p, n_chunk):
    cmp_end = np.arange(n_chunk) * CMP_STRIDE + CMP_LEN - 1
    kng = jnp.tile(p['k_cmp_norm'], N_KV_HEADS).reshape(1, KV_W)
    return [kng, _blockdiag_ones(KV_W), *_rope_tables(cmp_end)]


def _cmp_sample_body(pt_ref, cache_hbm, pe_ref, w1_ref, w2_ref, *rest, n_pages, n_batch, is_key):
    key_consts, (o_ref, buf, tok, a_sc, sem) = rest[:-5], rest[-5:]
    b = pl.program_id(0)
    slot = b % 2
    page_len = cache_hbm.shape[2]
    n_pos = n_pages * page_len
    unroll = 8 if n_pages % 8 == 0 else 1

    def page_copy(page, i, sl):
        dst = pl.ds(pl.multiple_of(i * page_len, page_len), page_len)
        return pltpu.make_async_copy(cache_hbm.at[page], buf.at[sl, :, dst], sem.at[sl])

    def gather(seq, sl):
        def issue(i, carry):
            page_copy(pt_ref[seq * n_pages + i], i, sl).start()
            return carry
        lax.fori_loop(0, n_pages, issue, 0, unroll=unroll)

    @pl.when(b == 0)
    def _():
        gather(0, 0)

    @pl.when(b + 1 < n_batch)
    def _():
        gather(b + 1, 1 - slot)

    def wait(i, carry):
        page_copy(0, i, slot).wait()
        return carry
    lax.fori_loop(0, n_pages, wait, 0, unroll=unroll)

    n_blk = 4 if n_pos % (4 * 8 * page_len) == 0 else 1
    pos_blk = n_pos // n_blk
    chunk_blk = pos_blk // CMP_STRIDE
    xw = min(pos_blk, 8 * page_len)
    for j in range(n_blk):
        p0 = j * pos_blk
        for i in range(pos_blk // xw):
            o = p0 + i * xw
            tok[o:o + xw, :] = buf[slot, :, o:o + xw].T
        a0, a1 = _first_layer(tok, p0, chunk_blk, pe_ref, w1_ref)
        a_sc[0, j * chunk_blk:(j + 1) * chunk_blk, :] = a0
        a_sc[1, j * chunk_blk:(j + 1) * chunk_blk, :] = a1
    out = _compress_tail(a_sc[0], a_sc[1], w2_ref)
    o_ref[0] = _cmp_key_post(out, *key_consts) if is_key else out


def _cmp_sample(page_table, cache_t, p, is_key):
    b, n_pages = page_table.shape
    page_len = cache_t.shape[2]
    n_pos = n_pages * page_len
    n_chunk = n_pos // CMP_STRIDE
    name = 'k' if is_key else 'v'
    args = _cmp_weights(p, name) + (_cmp_key_consts(p, n_chunk) if is_key else [])
    const = lambda a: pl.BlockSpec(a.shape, lambda i, pt: (0,) * a.ndim)
    grid_spec = pltpu.PrefetchScalarGridSpec(
        num_scalar_prefetch=1, grid=(b,),
        in_specs=[pl.BlockSpec(memory_space=pl.ANY)] + [const(a) for a in args],
        out_specs=pl.BlockSpec((1, n_chunk, KV_W), lambda i, pt: (i, 0, 0)),
        scratch_shapes=[pltpu.VMEM((2, KV_W, n_pos), F32), pltpu.VMEM((n_pos, KV_W), F32),
                        pltpu.VMEM((N_SUB, n_chunk, N_KV_HEADS * CMP_HID), F32),
                        pltpu.SemaphoreType.DMA((2,))])
    return pl.pallas_call(
        functools.partial(_cmp_sample_body, n_pages=n_pages, n_batch=b, is_key=is_key),
        out_shape=jax.ShapeDtypeStruct((b, n_chunk, KV_W), F32), grid_spec=grid_spec,
        compiler_params=_params(1), name=f"cmp_sample_{name}")(page_table.reshape(-1), cache_t, *args)


def _cmp_to_sel(n_chunk, n_cmp, n_sel, width):
    cs = np.arange(n_chunk)[:, None] * CMP_STRIDE
    ss = np.arange(width)[None, :] * SEL_BLK
    ov = np.clip(np.minimum(cs + CMP_LEN, ss + SEL_BLK) - np.maximum(cs, ss), 0, None) / CMP_LEN
    ov = ov * (np.arange(n_chunk)[:, None] < n_cmp) * (np.arange(width)[None, :] < n_sel)
    return jnp.asarray(ov, dtype=BF16)


def _attn_qlanes_body(q_ref, gt_ref, kc_ref, vct_ref, ks_ref, vst_ref, kw_ref, vwt_ref, ovt_ref, eblk_ref,
                      o_ref, qa_sc, oct_sc, m_sc, l_sc, acc_sc, s_sc,
                      *, t, tq, kc_len, n_cmp, n_sel):
    off = pl.program_id(1) * tq
    cols = GRP * tq
    lane = lax.broadcasted_iota(jnp.int32, (tq, LANES), 1)
    qpos = off + lax.broadcasted_iota(jnp.int32, (1, tq), 1)
    q = q_ref[0] * (SCALE * LOG2E)
    for h in range(N_HEADS):
        g, r = divmod(h, GRP)
        qa_sc[g, :KV_W, r * tq:(r + 1) * tq] = _head_to_group_lanes(q, h, lane).T.astype(BF16)

    def attend(jobs):
        stats = []
        for st, k, vt, rhs, bias in jobs:
            n = k.shape[0]
            s = _dot(k, rhs)
            if bias is not None:
                s = s + bias()
            s_sc[st, :n, :] = s
            m_old = m_sc[st]
            m_new = jnp.maximum(m_old, jnp.max(s, axis=0, keepdims=True))
            m_sc[st] = m_new
            stats.append((m_new, jnp.exp2(m_old - m_new)))
        probs = []
        for (st, k, vt, rhs, bias), (m_new, alpha) in zip(jobs, stats):
            p = jnp.exp2(s_sc[st, :k.shape[0], :] - m_new)
            l_sc[st] = alpha * l_sc[st] + jnp.sum(p, axis=0, keepdims=True)
            probs.append(p.astype(BF16))
        for (st, k, vt, rhs, bias), (m_new, alpha), p in zip(jobs, stats, probs):
            acc_sc[st] = alpha * acc_sc[st] + _dot(vt, p)

    sel_state = lambda g: g
    win_state = lambda g: N_KV_HEADS + g
    m_sc[...] = jnp.full(m_sc.shape, NEG_INF, F32)
    l_sc[...] = jnp.zeros(l_sc.shape, F32)
    acc_sc[...] = jnp.zeros(acc_sc.shape, F32)

    n_chunk = kc_ref.shape[1]
    kcm = kc_ref[0].astype(BF16)
    vct = vct_ref[0].astype(BF16)
    nrow = lax.broadcasted_iota(jnp.int32, (n_chunk, tq), 0)
    mask_1 = jnp.logical_and(nrow * CMP_STRIDE + (CMP_LEN - 1) <= qpos, nrow < n_cmp)
    mask_c = jnp.concatenate([mask_1.astype(F32)] * GRP, axis=1) > 0.5
    psum = []
    for g in range(N_KV_HEADS):
        sc = jnp.where(mask_c, _dot(kcm, qa_sc[g, :KV_W, :]), NEG_INF)
        e = jnp.where(mask_c, jnp.exp2(sc - jnp.max(sc, axis=0, keepdims=True)), 0.0)
        pc = e / jnp.maximum(jnp.sum(e, axis=0, keepdims=True), 1e-30)
        oct_sc[g] = _dot(vct, pc.astype(BF16))
        psum.append(sum(pc[:, r * tq:(r + 1) * tq] for r in range(GRP)))

    n_sel_pad = -(-n_sel // 8) * 8
    jrow = lax.broadcasted_iota(jnp.int32, (n_sel_pad, tq), 0)
    cur = qpos // SEL_BLK
    forced = jnp.logical_or(jrow == 0, jnp.logical_or(jrow == cur, jrow == cur - 1))
    bonus = jnp.where(forced, FORCE_BONUS, 0.0)
    k_eff = min(TOPK, n_sel)
    ovt = ovt_ref[...]
    for g in range(N_KV_HEADS):
        hi = psum[g].astype(BF16)
        lo = (psum[g] - hi.astype(F32)).astype(BF16)
        imp = _dot(ovt, hi) + _dot(ovt, lo)
        score = jnp.where(jrow <= cur, imp + bonus, NEG_INF)
        rank = jnp.zeros((n_sel_pad, tq), F32)
        for j in range(n_sel):
            row = score[j:j + 1, :]
            tie = jnp.where(jrow > j, 1.0, 0.0)
            rank = rank + jnp.where(row > score, 1.0, jnp.where(row == score, tie, 0.0))
        chosen = jnp.logical_and(rank < k_eff, jrow <= cur)
        for part, keep in enumerate((jnp.logical_and(chosen, jrow < off // SEL_BLK), chosen)):
            r0 = KV_W + part * LANES
            pen = jnp.where(keep, 0.0, -MASK_PENALTY).astype(BF16)
            qa_sc[g, r0:r0 + n_sel_pad, :] = jnp.concatenate([pen] * GRP, axis=1)
            qa_sc[g, r0 + n_sel_pad:r0 + LANES, :] = jnp.zeros((LANES - n_sel_pad, cols), BF16)

    tq_off = pl.multiple_of(off, tq)
    def triangle(keep_front):
        kk = lax.broadcasted_iota(jnp.int32, (tq, cols), 0)
        qq = lax.broadcasted_iota(jnp.int32, (tq, cols), 1) & (tq - 1)
        return jnp.where((kk > qq) if keep_front else (kk <= qq), 0.0, -jnp.inf)

    tri_own = functools.partial(triangle, False)
    tri_front = functools.partial(triangle, True)

    k_own = jnp.concatenate(
        [ks_ref[0, pl.ds(tq_off, tq), :].astype(BF16), eblk_ref[pl.ds(tq_off, tq), :]], axis=1)
    vt_own = vst_ref[0, :, pl.ds(tq_off, tq)].astype(BF16)
    kw_own = kw_ref[0, pl.ds(tq_off, tq), :].astype(BF16)
    vwt_own = vwt_ref[0, :, pl.ds(tq_off, tq)].astype(BF16)
    groups = range(N_KV_HEADS)
    q_own = lambda g: jnp.concatenate([qa_sc[g, :KV_W, :], qa_sc[g, KV_W + LANES:, :]], axis=0)
    attend([(sel_state(g), k_own, vt_own, q_own(g), tri_own) for g in groups]
           + [(win_state(g), kw_own, vwt_own, qa_sc[g, :KV_W, :], tri_own) for g in groups])

    for ci in range(t // kc_len):
        @pl.when(ci * kc_len < off)
        def _(ci=ci):
            base = ci * kc_len
            k_aug = jnp.concatenate(
                [ks_ref[0, base:base + kc_len, :].astype(BF16), eblk_ref[base:base + kc_len, :]], axis=1)
            vt = vst_ref[0, :, base:base + kc_len].astype(BF16)
            attend([(sel_state(g), k_aug, vt, qa_sc[g, :KV_W + LANES, :], None) for g in groups])

    def window_keys(start, n, bias):
        start = pl.multiple_of(start, tq)
        kw = kw_ref[0, pl.ds(start, n), :].astype(BF16)
        vwt = vwt_ref[0, :, pl.ds(start, n)].astype(BF16)
        attend([(win_state(g), kw, vwt, qa_sc[g, :KV_W, :], bias) for g in groups])

    @pl.when(off >= WINDOW)
    def _():
        window_keys(off - WINDOW, tq, tri_front)
        if WINDOW > tq:
            window_keys(off - WINDOW + tq, WINDOW - tq, None)

    for i in range(1, WINDOW // tq):
        @pl.when(jnp.logical_and(off < WINDOW, off - i * tq >= 0))
        def _(i=i):
            window_keys(off - i * tq, tq, None)

    gtt = gt_ref[0].T
    outs = []
    for g in range(N_KV_HEADS):
        o_w = acc_sc[win_state(g)] / jnp.maximum(l_sc[win_state(g)], 1e-30)
        o_s = acc_sc[sel_state(g)] / jnp.maximum(l_sc[sel_state(g)], 1e-30)
        o_c = oct_sc[g]
        for r in range(GRP):
            h = g * GRP + r
            cs = slice(r * tq, (r + 1) * tq)
            o_h = (gtt[3 * h:3 * h + 1, :] * o_c[:, cs] + gtt[3 * h + 1:3 * h + 2, :] * o_s[:, cs]
                   + gtt[3 * h + 2:3 * h + 3, :] * o_w[:, cs])
            outs.append(o_h.T)
    o_ref[0] = jnp.concatenate(
        [_pair_tile(outs[2 * p], outs[2 * p + 1], 2 * p, lane) for p in range(N_HEADS // 2)], axis=-1)


def _attn_qlanes(q, gt, kcmp, vcmp_t, ks, vs_t, kw, vw_t, tq=512, kc_len=512):
    b, t, _ = q.shape
    n_chunk = kcmp.shape[1]
    n_cmp = n_chunk - N_SUB + 1
    n_sel = -(-t // SEL_BLK)
    assert t % kc_len == 0 and t >= WINDOW + tq and n_sel <= LANES and n_chunk % LANES == 0
    assert tq % LANES == 0 and WINDOW % tq == 0 and tq % SEL_BLK == 0
    n_sel_pad = -(-n_sel // 8) * 8
    ov_t = _cmp_to_sel(n_chunk, n_cmp, n_sel, n_sel_pad).T
    assert tq & (tq - 1) == 0
    eblk = jnp.asarray(np.arange(t)[:, None] // SEL_BLK == np.arange(LANES)[None, :], dtype=BF16)
    qblk = lambda i, j: (i, j, 0)
    full = lambda i, j: (i, 0, 0)
    const = lambda a: pl.BlockSpec(a.shape, lambda i, j: (0,) * a.ndim)
    tok = pl.BlockSpec((1, t, KV_W), full)
    tr = pl.BlockSpec((1, KV_W, t), full)
    cols = GRP * tq
    n_states = 2 * N_KV_HEADS
    assert kc_len >= WINDOW - tq
    body = functools.partial(_attn_qlanes_body, t=t, tq=tq, kc_len=kc_len, n_cmp=n_cmp, n_sel=n_sel)
    return pl.pallas_call(
        body, out_shape=jax.ShapeDtypeStruct((b, t, D_ATT), F32), grid=(b, t // tq),
        in_specs=[pl.BlockSpec((1, tq, D_ATT), qblk), pl.BlockSpec((1, tq, LANES), qblk),
                  pl.BlockSpec((1, n_chunk, KV_W), full), pl.BlockSpec((1, KV_W, n_chunk), full),
                  tok, tr, tok, tr, const(ov_t), const(eblk)],
        out_specs=pl.BlockSpec((1, tq, D_ATT), qblk),
        scratch_shapes=[pltpu.VMEM((N_KV_HEADS, KV_W + 2 * LANES, cols), BF16),
                        pltpu.VMEM((N_KV_HEADS, KV_W, cols), F32), pltpu.VMEM((n_states, 1, cols), F32),
                        pltpu.VMEM((n_states, 1, cols), F32), pltpu.VMEM((n_states, KV_W, cols), F32),
                        pltpu.VMEM((n_states, kc_len, cols), F32)],
        compiler_params=_params(2), name="attn_prompt")(
            q, gt, kcmp, vcmp_t, ks, vs_t, kw, vw_t, ov_t, eblk)


def _sample_heads(q_row):
    lane = lax.broadcasted_iota(jnp.int32, (1, LANES), 1)
    return jnp.concatenate([_head_to_group_lanes(q_row, h, lane) for h in range(N_HEADS)], axis=0)


def _attn_s1_body(q_ref, kc_ref, vc_ref, ov_ref, oc_ref, idx_ref, *, n_cmp, n_sel, q_pos):
    qh = _sample_heads(q_ref[0])
    n_chunk = kc_ref.shape[1]
    ncol = lax.broadcasted_iota(jnp.int32, (N_HEADS, n_chunk), 1)
    mask = jnp.logical_and(ncol * CMP_STRIDE + (CMP_LEN - 1) <= q_pos, ncol < n_cmp)
    pc = _masked_softmax(_dot_t(qh.astype(BF16), kc_ref[0].astype(BF16)) * SCALE, mask)
    oc_ref[0] = _dot(pc.astype(BF16), vc_ref[0].astype(BF16))
    rows = [jnp.sum(pc[g * GRP:(g + 1) * GRP, :], axis=0, keepdims=True) for g in range(N_KV_HEADS)]
    psum = jnp.concatenate(rows + [jnp.zeros((N_HEADS - N_KV_HEADS, n_chunk), F32)], axis=0)
    imp = _split_dot(psum, ov_ref[...])
    width = imp.shape[1]
    jl = lax.broadcasted_iota(jnp.int32, (N_HEADS, width), 1)
    cur = q_pos // SEL_BLK
    forced = jnp.logical_or(jl == 0, jnp.logical_or(jl == cur, jl == cur - 1))
    score = jnp.where(jl <= cur, imp + jnp.where(forced, FORCE_BONUS, 0.0), NEG_INF)
    n_rows = -(-n_sel // 8) * 8
    ii = lax.broadcasted_iota(jnp.int32, (n_rows, width), 0)
    jj = lax.broadcasted_iota(jnp.int32, (n_rows, width), 1)
    slot = lax.broadcasted_iota(jnp.int32, (n_rows, LANES), 1).astype(F32)
    blk = lax.broadcasted_iota(jnp.int32, (n_rows, LANES), 0).astype(F32)
    tie = jnp.where(jj < ii, 1.0, 0.0)
    out_rows = []
    for g in range(N_KV_HEADS):
        row = jnp.broadcast_to(score[g:g + 1, :], (n_rows, width))
        col = jnp.sum(jnp.where(ii == jj, row, 0.0), axis=1, keepdims=True)
        beats = jnp.where(row > col, 1.0, jnp.where(row == col, tie, 0.0))
        rank = jnp.sum(beats, axis=1, keepdims=True)
        out_rows.append(jnp.sum(jnp.where(rank == slot, blk, 0.0), axis=0, keepdims=True))
    out_rows.append(jnp.zeros((N_HEADS - N_KV_HEADS, LANES), F32))
    idx_ref[0] = jnp.concatenate(out_rows, axis=0).astype(jnp.int32)


def _attn_s1(q, kcmp, vcmp, q_pos, n_sel):
    b = q.shape[0]
    n_chunk = kcmp.shape[1]
    n_cmp = n_chunk - N_SUB + 1
    width = -(-n_sel // LANES) * LANES
    ov = _cmp_to_sel(n_chunk, n_cmp, n_sel, width)
    blk3 = lambda i: (i, 0, 0)
    cm = pl.BlockSpec((1, n_chunk, KV_W), blk3)
    out = pl.BlockSpec((1, N_HEADS, LANES), blk3)
    return pl.pallas_call(
        functools.partial(_attn_s1_body, n_cmp=n_cmp, n_sel=n_sel, q_pos=q_pos),
        out_shape=[jax.ShapeDtypeStruct((b, N_HEADS, LANES), F32),
                   jax.ShapeDtypeStruct((b, N_HEADS, LANES), jnp.int32)],
        grid=(b,),
        in_specs=[pl.BlockSpec((1, 1, D_ATT), blk3), cm, cm, pl.BlockSpec(ov.shape, lambda i: (0, 0))],
        out_specs=[out, out], compiler_params=_params(1), name="attn_sample_select")(
            q.reshape(b, 1, D_ATT), kcmp, vcmp, ov)


def _extra_key_attention(qh, k_t, v_t, mask, k_new, v_new):
    qf = qh.astype(F32)
    s = jnp.where(mask, _dot(qh, k_t.astype(BF16)) * SCALE, NEG_INF)
    s_new = jnp.sum(qf * k_new, axis=-1, keepdims=True) * SCALE
    m = jnp.maximum(jnp.max(s, axis=-1, keepdims=True), s_new)
    e = jnp.where(mask, jnp.exp(s - m), 0.0)
    e_new = jnp.exp(s_new - m)
    denom = jnp.maximum(jnp.sum(e, axis=-1, keepdims=True) + e_new, 1e-30)
    return (_dot_t(e.astype(BF16), v_t.astype(BF16)) + e_new * v_new) / denom


def _attn_s2_body(idx_ref, pt_ref, q_ref, gt_ref, oc_ref, ksn_ref, vsn_ref, kwn_ref, vwn_ref,
                  kws_ref, vws_ref, kcache, vcache, o_ref, kbuf, vbuf, sem,
                  *, n_pages, n_batch, n_cache_blk, q_pos):
    b = pl.program_id(0)
    slot = b % 2
    page_len = kcache.shape[2]
    per_page = page_len // SEL_BLK

    def block_id(seq, g, k):
        return idx_ref[(seq * N_KV_HEADS + g) * TOPK + k]

    def copies(seq, sl, g, k):
        blk = jnp.minimum(block_id(seq, g, k), n_cache_blk - 1)
        page = pt_ref[seq * n_pages + blk // per_page]
        dst = pl.ds(k * page_len, page_len)
        return (pltpu.make_async_copy(kcache.at[page], kbuf.at[sl, g, :, dst], sem.at[sl, 0]),
                pltpu.make_async_copy(vcache.at[page], vbuf.at[sl, g, :, dst], sem.at[sl, 1]))

    def for_all_copies(seq, sl, fn):
        for g in range(N_KV_HEADS):
            for k in range(TOPK):
                for cp in copies(seq, sl, g, k):
                    fn(cp)

    @pl.when(b == 0)
    def _():
        for_all_copies(0, 0, lambda cp: cp.start())

    @pl.when(b + 1 < n_batch)
    def _():
        for_all_copies(b + 1, 1 - slot, lambda cp: cp.start())

    for_all_copies(b, slot, lambda cp: cp.wait())
    blks = [[block_id(b, g, k) for k in range(TOPK)] for g in range(N_KV_HEADS)]

    qh = _sample_heads(q_ref[0]).astype(BF16)
    n_keys = TOPK * page_len
    row_grp = lax.broadcasted_iota(jnp.int32, (N_HEADS, n_keys), 0) // GRP
    key = lax.broadcasted_iota(jnp.int32, (N_HEADS, n_keys), 1)
    key_slot = key // page_len
    blk_in_page = (key % page_len) // SEL_BLK
    keep = jnp.zeros((N_HEADS, n_keys), F32)
    for g in range(N_KV_HEADS):
        for k in range(TOPK):
            blk = blks[g][k]
            in_cache = jnp.where(blk < n_cache_blk, 1.0, 0.0)
            hit = jnp.where(blk_in_page == blk % per_page, in_cache, 0.0)
            keep = jnp.where(jnp.logical_and(row_grp == g, key_slot == k), hit, keep)
    mask_s = keep > 0.5
    o_sel = [_extra_key_attention(qh, kbuf.at[slot, g][...], vbuf.at[slot, g][...], mask_s,
                                  ksn_ref[0], vsn_ref[0])
             for g in range(N_KV_HEADS)]
    out_grp = lax.broadcasted_iota(jnp.int32, (N_HEADS, LANES), 0) // GRP
    o_s = jnp.where(out_grp == 0, o_sel[0], o_sel[1])

    n_win = kws_ref.shape[2]
    kpos = (q_pos - n_win) + lax.broadcasted_iota(jnp.int32, (N_HEADS, n_win), 1)
    dist = q_pos - kpos
    mask_w = jnp.logical_and(dist >= 0, dist < WINDOW)
    o_w = _extra_key_attention(qh, kws_ref[0], vws_ref[0], mask_w, kwn_ref[0], vwn_ref[0])

    lane8 = lax.broadcasted_iota(jnp.int32, (N_HEADS, LANES), 1)
    head8 = lax.broadcasted_iota(jnp.int32, (N_HEADS, LANES), 0)
    gt = jnp.broadcast_to(gt_ref[0], (N_HEADS, LANES))
    gate = [jnp.sum(jnp.where(lane8 == 3 * head8 + j, gt, 0.0), axis=-1, keepdims=True) for j in range(3)]
    o = gate[0] * oc_ref[0] + gate[1] * o_s + gate[2] * o_w
    lane = lax.broadcasted_iota(jnp.int32, (1, LANES), 1)
    o_ref[0] = jnp.concatenate(
        [_pair_tile(o[2 * p:2 * p + 1, :], o[2 * p + 1:2 * p + 2, :], 2 * p, lane)
         for p in range(N_HEADS // 2)], axis=-1)


def _attn_s2(idx, page_table, q, gt, oc, ks_new, vs_new, kw_new, vw_new, kw_state_t, vw_state_t,
             cache_k_t, cache_v_t, q_pos):
    b, n_pages = page_table.shape
    page_len = cache_k_t.shape[2]
    n_win = kw_state_t.shape[2]
    assert n_win <= WINDOW
    blk3 = lambda i, *_: (i, 0, 0)
    row = lambda w: pl.BlockSpec((1, 1, w), blk3)
    grid_spec = pltpu.PrefetchScalarGridSpec(
        num_scalar_prefetch=2, grid=(b,),
        in_specs=[row(D_ATT), row(LANES), pl.BlockSpec((1, N_HEADS, LANES), blk3),
                  row(KV_W), row(KV_W), row(KV_W), row(KV_W),
                  pl.BlockSpec((1, KV_W, n_win), blk3), pl.BlockSpec((1, KV_W, n_win), blk3),
                  pl.BlockSpec(memory_space=pl.ANY), pl.BlockSpec(memory_space=pl.ANY)],
        out_specs=row(D_ATT),
        scratch_shapes=[pltpu.VMEM((2, N_KV_HEADS, KV_W, TOPK * page_len), F32),
                        pltpu.VMEM((2, N_KV_HEADS, KV_W, TOPK * page_len), F32),
                        pltpu.SemaphoreType.DMA((2, 2))])
    r3 = lambda a: a.reshape(b, 1, a.shape[-1])
    body = functools.partial(_attn_s2_body, n_pages=n_pages, n_batch=b,
                             n_cache_blk=n_pages * (page_len // SEL_BLK),
                             q_pos=q_pos)
    return pl.pallas_call(
        body, out_shape=jax.ShapeDtypeStruct((b, 1, D_ATT), F32), grid_spec=grid_spec,
        compiler_params=_params(1), name="attn_sample")(
            idx.reshape(-1), page_table.reshape(-1), r3(q), r3(gt), oc, r3(ks_new), r3(vs_new),
            r3(kw_new), r3(vw_new), kw_state_t, vw_state_t, cache_k_t, cache_v_t).reshape(b, D_ATT)


def _prompt_layer(x, p, tm=512, tm_proj=1024):
    b, t, d = x.shape
    x2 = _ffn(x.reshape(b * t, d), p['ffn1_norm'], p['ffn1_w_in'], p['ffn1_w_out'], None, tm)
    u, q, gt, kc, vc, ks, kw, kc_t, vc_t, ks_t, vs_t, kw_t, vw_t = _proj(
        x2, p, _rope_tables(np.arange(t)), tm_proj, seq_len=t)
    r3 = lambda a: a.reshape(b, t, a.shape[-1])
    u3 = r3(u)
    cn = _conv_prompt(u3, p)
    kcmp, vcmp_t = _cmp_prompt(r3(kc), r3(vc), p)
    o = _attn_qlanes(r3(q), r3(gt), kcmp, vcmp_t, r3(ks), vs_t, r3(kw), vw_t)
    y = _ffn(x2, p['ffn2_norm'], p['ffn2_w_in'], p['ffn2_w_out'], p['final_norm'], tm,
             mix=(cn.reshape(b * t, C_CONV), o.reshape(b * t, D_ATT), p['out_norm_attn'], p['w_out']))
    r4 = lambda a: a.reshape(b, N_KV_HEADS, HEAD_DIM, a.shape[-1]).transpose(0, 3, 1, 2)
    keep = min(WINDOW, t)
    state = (r4(kc_t), r4(vc_t), r4(ks_t), r4(vs_t), r4(kw_t[:, :, t - keep:]), r4(vw_t[:, :, t - keep:]),
             u3[:, t - (CONV_W - 1):])
    return y.reshape(b, t, d), state


def _sample_layer(x, p, cache_k_cmp, cache_v_cmp, cache_k_sel, cache_v_sel, kw_state, vw_state,
                  conv_state, page_table):
    b, t, d = x.shape
    assert t == 1
    past_len = page_table.shape[1] * cache_k_cmp.shape[1]
    x2 = _ffn(x.reshape(b, d), p['ffn1_norm'], p['ffn1_w_in'], p['ffn1_w_out'], None, b)
    tabs = _rope_tables(np.full((b,), past_len))
    u, q, gt, kc, vc, ks, vs, kw, vw = _proj(x2, p, tabs, b)
    cn = _conv_sample(conv_state, u, p)
    keys_on_lanes = lambda a: a.transpose(0, 2, 3, 1).reshape(a.shape[0], KV_W, a.shape[1])
    kcmp = _cmp_sample(page_table, keys_on_lanes(cache_k_cmp), p, True)
    vcmp = _cmp_sample(page_table, keys_on_lanes(cache_v_cmp), p, False)
    n_sel = -(-(past_len + 1) // SEL_BLK)
    oc, idx = _attn_s1(q, kcmp, vcmp, past_len, n_sel)
    n_win = kw_state.shape[1]
    o = _attn_s2(idx[:, :N_KV_HEADS, :TOPK], page_table, q, gt, oc, ks, vs, kw, vw,
                 keys_on_lanes(kw_state), keys_on_lanes(vw_state),
                 keys_on_lanes(cache_k_sel), keys_on_lanes(cache_v_sel), past_len)
    y = _ffn(x2, p['ffn2_norm'], p['ffn2_w_in'], p['ffn2_w_out'], p['final_norm'], b,
             mix=(cn, o, p['out_norm_attn'], p['w_out']))
    r4 = lambda a: a.reshape(b, 1, N_KV_HEADS, HEAD_DIM)
    keep = min(WINDOW, n_win + 1)
    new_kw = jnp.concatenate([kw_state, r4(kw)], axis=1)[:, n_win + 1 - keep:]
    new_vw = jnp.concatenate([vw_state, r4(vw)], axis=1)[:, n_win + 1 - keep:]
    new_conv = jnp.concatenate([conv_state, u[:, None, :]], axis=1)[:, 1:]
    return y.reshape(b, 1, d), (r4(kc), r4(vc), r4(ks), r4(vs), new_kw, new_vw, new_conv)


_PARAM_NAMES = ('ffn1_norm', 'ffn1_w_in', 'ffn1_w_out', 'mix_norm', 'w_in', 'conv_w', 'conv_b', 'conv_ln_g',
                'conv_ln_b', 'q_norm', 'k_cmp_norm', 'k_sel_norm', 'k_win_norm', 'cmp_k_pos', 'cmp_k_w1',
                'cmp_k_w2', 'cmp_v_pos', 'cmp_v_w1', 'cmp_v_w2', 'out_norm_conv', 'out_norm_attn', 'w_out',
                'ffn2_norm', 'ffn2_w_in', 'ffn2_w_out', 'final_norm')


def kernel(x_prompt, x_sample, cache_k_cmp, cache_v_cmp, cache_k_sel, cache_v_sel, state_k_win, state_v_win,
           state_conv, page_table, ffn1_norm, ffn1_w_in, ffn1_w_out, mix_norm, w_in, conv_w, conv_b, conv_ln_g,
           conv_ln_b, q_norm, k_cmp_norm, k_sel_norm, k_win_norm, cmp_k_pos, cmp_k_w1, cmp_k_w2, cmp_v_pos,
           cmp_v_w1, cmp_v_w2, out_norm_conv, out_norm_attn, w_out, ffn2_norm, ffn2_w_in, ffn2_w_out, final_norm):
    stacked = dict(zip(_PARAM_NAMES, (
        ffn1_norm, ffn1_w_in, ffn1_w_out, mix_norm, w_in, conv_w, conv_b, conv_ln_g, conv_ln_b, q_norm,
        k_cmp_norm, k_sel_norm, k_win_norm, cmp_k_pos, cmp_k_w1, cmp_k_w2, cmp_v_pos, cmp_v_w1, cmp_v_w2,
        out_norm_conv, out_norm_attn, w_out, ffn2_norm, ffn2_w_in, ffn2_w_out, final_norm)))
    depth = ffn1_norm.shape[0]
    yp, ys = x_prompt, x_sample
    new_p, new_s = [], []
    for l in range(depth):
        p = {k: v[l] for k, v in stacked.items()}
        yp, st_p = _prompt_layer(yp, p)
        new_p.append(st_p)
        ys, st_s = _sample_layer(ys, p, cache_k_cmp[l], cache_v_cmp[l], cache_k_sel[l], cache_v_sel[l],
                                 state_k_win[l], state_v_win[l], state_conv[l], page_table)
        new_s.append(st_s)
    outs_p = [jnp.stack(tup) for tup in zip(*new_p)]
    outs_s = [jnp.stack(tup) for tup in zip(*new_s)]
    return (yp, ys, *outs_p, *outs_s)
```

```python
import functools

import numpy as np
import jax
import jax.numpy as jnp
from jax import lax
from jax.experimental import pallas as pl
from jax.experimental.pallas import tpu as pltpu

F32 = jnp.float32
BF16 = jnp.bfloat16

HEAD_DIM = 64
N_HEADS = 8
N_KV_HEADS = 2
GRP = N_HEADS // N_KV_HEADS
KV_W = N_KV_HEADS * HEAD_DIM
C_CONV = 512
D_ATT = 512
CONV_W = 31
CMP_LEN = 32
CMP_STRIDE = 16
N_SUB = CMP_LEN // CMP_STRIDE
CMP_HID = 2 * HEAD_DIM
SEL_BLK = 64
TOPK = 16
WINDOW = 512
ROT_DIM = HEAD_DIM // 4
ROT_HALF = ROT_DIM // 2
ROPE_THETA = 500000.0
NEG_INF = -1e30
MASK_PENALTY = 2.0 ** 100
FORCE_BONUS = 1e4
EPS = 1e-6
SCALE = HEAD_DIM ** -0.5
LOG2E = 1.4426950408889634
LANES = 128
CHUNK_W = CMP_STRIDE * KV_W
VMEM_LIMIT = 56 * 1024 * 1024


def _params(n_grid_dims):
    return pltpu.CompilerParams(
        dimension_semantics=("arbitrary",) * n_grid_dims, vmem_limit_bytes=VMEM_LIMIT)


def _dot(a, b):
    return jnp.dot(a, b, preferred_element_type=F32)


def _dot_t(a, b):
    return lax.dot_general(a, b, (((1,), (1,)), ((), ())), preferred_element_type=F32)


def _split_dot(x, m_bf16):
    hi = x.astype(BF16)
    lo = (x - hi.astype(F32)).astype(BF16)
    return _dot(hi, m_bf16) + _dot(lo, m_bf16)


def _sigmoid(x):
    return 1.0 / (1.0 + jnp.exp(-x))


def _rms(x, g):
    return x * lax.rsqrt(jnp.mean(x * x, axis=-1, keepdims=True) + EPS) * g


def _group_rms(x, g_tiled, ones_blockdiag):
    ss = _split_dot(x * x, ones_blockdiag)
    return x * lax.rsqrt(ss * (1.0 / HEAD_DIM) + EPS) * g_tiled


def _rope(x, c, sa, sb):
    n = x.shape[-1]
    return x * c + pltpu.roll(x, n - ROT_HALF, 1) * sa + pltpu.roll(x, ROT_HALF, 1) * sb


def _tile_lanes(t, reps):
    return jnp.concatenate([t] * reps, axis=-1) if reps > 1 else t


def _masked_softmax(s, mask):
    s = jnp.where(mask, s, NEG_INF)
    m = jnp.max(s, axis=-1, keepdims=True)
    e = jnp.where(mask, jnp.exp(s - m), 0.0)
    return e / jnp.maximum(jnp.sum(e, axis=-1, keepdims=True), 1e-30)


def _head_to_group_lanes(q, h, lane):
    g, p = h // GRP, h // 2
    qp = q[:, LANES * p:LANES * (p + 1)]
    src = qp if (h % 2) == g else pltpu.roll(qp, HEAD_DIM, 1)
    keep = (lane < HEAD_DIM) if g == 0 else (lane >= HEAD_DIM)
    return jnp.where(keep, src, 0.0)


def _pair_tile(o_even, o_odd, h_even, lane):
    g = h_even // GRP
    a = o_even if g == 0 else pltpu.roll(o_even, HEAD_DIM, 1)
    b = pltpu.roll(o_odd, HEAD_DIM, 1) if g == 0 else o_odd
    return jnp.where(lane < HEAD_DIM, a, b)


def _ffn_body(*refs, n_chunk, tf, ff, final, mixed):
    refs = list(refs)
    x_ref = refs.pop(0)
    x = x_ref[...]
    if mixed:
        c_ref, a_ref, ag_ref, wc_ref, wa_ref = refs[:5]
        refs = refs[5:]
        x = (x + _dot(c_ref[...].astype(BF16), wc_ref[...])
             + _dot(_rms(a_ref[...], ag_ref[...]).astype(BF16), wa_ref[...]))
    g_ref, wi_ref, wo_ref = refs[:3]
    fg_ref = refs[3] if final else None
    o_ref = refs[-1]
    xn = _rms(x, g_ref[...]).astype(BF16)
    acc = jnp.zeros(x.shape, F32)
    for c in range(n_chunk):
        a = _dot(xn, wi_ref[:, c * tf:(c + 1) * tf])
        b = _dot(xn, wi_ref[:, ff + c * tf:ff + (c + 1) * tf])
        h = (a * _sigmoid(a)) * b
        acc = acc + _dot(h.astype(BF16), wo_ref[c * tf:(c + 1) * tf, :])
    y = x + 0.5 * acc
    if final:
        y = _rms(y, fg_ref[...])
    o_ref[...] = y


def _ffn(x2, norm_g, w_in, w_out, final_g, tm, mix=None):
    rows, d = x2.shape
    ff = w_out.shape[0]
    tf = 256
    assert rows % tm == 0 and ff % tf == 0 and w_in.shape == (d, 2 * ff)
    const = lambda i: (0, 0)
    row = lambda i: (i, 0)
    single = pl.Buffered(1)
    in_specs = [pl.BlockSpec((tm, d), row)]
    args = [x2]
    if mix is not None:
        cn, att, att_g, w_mix = mix
        w_mix = w_mix.astype(BF16)
        in_specs += [pl.BlockSpec((tm, C_CONV), row), pl.BlockSpec((tm, D_ATT), row),
                     pl.BlockSpec((1, D_ATT), const), pl.BlockSpec((C_CONV, d), const),
                     pl.BlockSpec((D_ATT, d), const)]
        args += [cn, att, att_g.reshape(1, D_ATT), w_mix[:C_CONV], w_mix[C_CONV:]]
    in_specs += [
        pl.BlockSpec((1, d), const),
        pl.BlockSpec((d, 2 * ff), const, pipeline_mode=single),
        pl.BlockSpec((ff, d), const, pipeline_mode=single),
    ]
    args += [norm_g.reshape(1, d), w_in.astype(BF16), w_out.astype(BF16)]
    if final_g is not None:
        in_specs.append(pl.BlockSpec((1, d), const))
        args.append(final_g.reshape(1, d))
    body = functools.partial(_ffn_body, n_chunk=ff // tf, tf=tf, ff=ff, final=final_g is not None,
                             mixed=mix is not None)
    return pl.pallas_call(
        body, out_shape=jax.ShapeDtypeStruct((rows, d), F32), grid=(rows // tm,),
        in_specs=in_specs, out_specs=pl.BlockSpec((tm, d), lambda i: (i, 0)),
        compiler_params=_params(1), name="ffn")(*args)


_SEG_GLU = 2 * C_CONV
_SEG_Q = _SEG_GLU + D_ATT
N_IN = _SEG_Q + 6 * KV_W + 3 * N_HEADS
N_IN_PAD = _SEG_Q + 7 * KV_W


def _proj_body(x_ref, g_ref, w_ref, qg_ref, ksg_ref, kwg_ref, bd4_ref, bd1_ref, c_ref, sa_ref, sb_ref,
               u_ref, q_ref, gt_ref, kc_ref, vc_ref, *kv_refs, transposed):
    xn = _rms(x_ref[...], g_ref[...]).astype(BF16)

    def seg(lo, width):
        return _dot(xn, w_ref[:, lo:lo + width])

    c, sa, sb = c_ref[...], sa_ref[...], sb_ref[...]
    reps = D_ATT // LANES
    glu = seg(0, _SEG_GLU)
    u_ref[...] = glu[:, :C_CONV] * _sigmoid(glu[:, C_CONV:])
    q = _group_rms(seg(_SEG_GLU, D_ATT), qg_ref[...], bd4_ref[...])
    q_ref[...] = _rope(q, _tile_lanes(c, reps), _tile_lanes(sa, reps), _tile_lanes(sb, reps))
    base = _SEG_Q
    kc = seg(base, KV_W)
    vc = seg(base + KV_W, KV_W)
    ks = _rope(_group_rms(seg(base + 2 * KV_W, KV_W), ksg_ref[...], bd1_ref[...]), c, sa, sb)
    vs = seg(base + 3 * KV_W, KV_W)
    kw = _rope(_group_rms(seg(base + 4 * KV_W, KV_W), kwg_ref[...], bd1_ref[...]), c, sa, sb)
    vw = seg(base + 5 * KV_W, KV_W)
    gt_ref[...] = _sigmoid(seg(base + 6 * KV_W, KV_W))
    kc_ref[...] = kc
    vc_ref[...] = vc
    if transposed:
        kv_refs[0][...] = ks
        kv_refs[1][...] = kw
        for ref, val in zip(kv_refs[2:], (kc, vc, ks, vs, kw, vw)):
            ref[0] = val.T
    else:
        for ref, val in zip(kv_refs, (ks, vs, kw, vw)):
            ref[...] = val


def _blockdiag_ones(width):
    idx = np.arange(width) // HEAD_DIM
    return jnp.asarray(idx[:, None] == idx[None, :], dtype=BF16)


def _rope_tables(pos):
    pos = np.asarray(pos, dtype=np.float64)
    inv = ROPE_THETA ** (-(np.arange(ROT_HALF, dtype=np.float64) * 2.0 / ROT_DIM))
    ang = pos[:, None] * inv[None, :]
    cos, sin = np.cos(ang).astype(np.float32), np.sin(ang).astype(np.float32)
    n = pos.shape[0]
    ones = np.ones((n, HEAD_DIM - ROT_DIM), np.float32)
    zeros = np.zeros((n, HEAD_DIM - ROT_DIM), np.float32)
    z8 = np.zeros((n, ROT_HALF), np.float32)
    c = np.concatenate([cos, cos, ones], axis=1)
    sa = np.concatenate([-sin, z8, zeros], axis=1)
    sb = np.concatenate([z8, sin, zeros], axis=1)
    return tuple(jnp.asarray(np.tile(t, (1, N_KV_HEADS))) for t in (c, sa, sb))


def _proj(x2, p, tabs, tm, seq_len=None):
    rows, d = x2.shape
    n_tab = tabs[0].shape[0] // tm
    w = jnp.pad(p['w_in'], ((0, 0), (0, N_IN_PAD - N_IN))).astype(BF16)
    const = lambda i: (0, 0)
    row = lambda i: (i, 0)
    tab = lambda i: (i % n_tab, 0)
    in_specs = [
        pl.BlockSpec((tm, d), row), pl.BlockSpec((1, d), const),
        pl.BlockSpec((d, N_IN_PAD), const, pipeline_mode=pl.Buffered(1)),
        pl.BlockSpec((1, D_ATT), const), pl.BlockSpec((1, KV_W), const), pl.BlockSpec((1, KV_W), const),
        pl.BlockSpec((D_ATT, D_ATT), const), pl.BlockSpec((KV_W, KV_W), const),
        pl.BlockSpec((tm, LANES), tab), pl.BlockSpec((tm, LANES), tab), pl.BlockSpec((tm, LANES), tab),
    ]
    widths = [C_CONV, D_ATT, LANES, KV_W, KV_W] + [KV_W] * (4 if seq_len is None else 2)
    out_shape = [jax.ShapeDtypeStruct((rows, wd), F32) for wd in widths]
    out_specs = [pl.BlockSpec((tm, wd), row) for wd in widths]
    if seq_len is not None:
        nt = seq_len // tm
        assert seq_len % tm == 0 and rows % seq_len == 0
        out_shape += [jax.ShapeDtypeStruct((rows // seq_len, KV_W, seq_len), F32)] * 6
        out_specs += [pl.BlockSpec((1, KV_W, tm), lambda i: (i // nt, 0, i % nt))] * 6
    args = [x2, p['mix_norm'].reshape(1, d), w,
            jnp.tile(p['q_norm'], N_HEADS).reshape(1, D_ATT),
            jnp.tile(p['k_sel_norm'], N_KV_HEADS).reshape(1, KV_W),
            jnp.tile(p['k_win_norm'], N_KV_HEADS).reshape(1, KV_W),
            _blockdiag_ones(D_ATT), _blockdiag_ones(KV_W), *tabs]
    return pl.pallas_call(
        functools.partial(_proj_body, transposed=seq_len is not None), out_shape=out_shape,
        grid=(rows // tm,), in_specs=in_specs, out_specs=out_specs,
        compiler_params=_params(1), name="proj")(*args)


def _conv_post(acc, lg, lb, og):
    mu = jnp.mean(acc, axis=-1, keepdims=True)
    xc = acc - mu
    var = jnp.mean(xc * xc, axis=-1, keepdims=True)
    y = xc * lax.rsqrt(var + EPS) * lg + lb
    y = y * _sigmoid(y)
    return _rms(y, og)


def _conv_body(u_ref, cw_ref, cb_ref, lg_ref, lb_ref, og_ref, o_ref, *, n_tiles, tt):
    halo = 32
    lead = halo - (CONV_W - 1)

    def tile(win, t0):
        acc = jnp.zeros((tt, C_CONV), F32) + cb_ref[...]
        for r in range(8):
            sh = win if r == 0 else pltpu.roll(win, tt + halo - r, 0)
            for a in range(halo // 8 + 1):
                w = 8 * a + r - lead
                if 0 <= w < CONV_W:
                    acc = acc + sh[8 * a:8 * a + tt, :] * cw_ref[w:w + 1, :]
        o_ref[0, pl.ds(t0, tt), :] = _conv_post(acc, lg_ref[...], lb_ref[...], og_ref[...])

    tile(jnp.concatenate([jnp.zeros((halo, C_CONV), F32), u_ref[0, 0:tt, :]], axis=0), 0)

    def step(i, carry):
        t0 = pl.multiple_of(i * tt, tt)
        tile(u_ref[0, pl.ds(pl.multiple_of(t0 - halo, halo), tt + halo), :], t0)
        return carry

    lax.fori_loop(1, n_tiles, step, 0)


def _conv_prompt(u, p, tt=256):
    b, t, c = u.shape
    cw = jnp.pad(p['conv_w'], ((0, 1), (0, 0)))
    const = lambda i: (0, 0)
    vec = lambda a: a.reshape(1, c)
    blk = pl.BlockSpec((1, t, c), lambda i: (i, 0, 0))
    return pl.pallas_call(
        functools.partial(_conv_body, n_tiles=t // tt, tt=tt),
        out_shape=jax.ShapeDtypeStruct((b, t, c), F32), grid=(b,),
        in_specs=[blk, pl.BlockSpec((32, c), const)] + [pl.BlockSpec((1, c), const)] * 4,
        out_specs=blk, compiler_params=_params(1), name="conv_prompt")(
            u, cw, vec(p['conv_b']), vec(p['conv_ln_g']), vec(p['conv_ln_b']), vec(p['out_norm_conv']))


def _conv_sample_body(st_ref, u_ref, cw_ref, cb_ref, lg_ref, lb_ref, og_ref, o_ref):
    acc = u_ref[...] * cw_ref[CONV_W - 1:CONV_W, :] + cb_ref[...]
    for w in range(CONV_W - 1):
        acc = acc + st_ref[w] * cw_ref[w:w + 1, :]
    o_ref[...] = _conv_post(acc, lg_ref[...], lb_ref[...], og_ref[...])


def _conv_sample(state, u, p):
    b = u.shape[0]
    c = C_CONV
    cw = jnp.pad(p['conv_w'], ((0, 1), (0, 0)))
    vec = lambda a: a.reshape(1, c)
    return pl.pallas_call(
        _conv_sample_body, out_shape=jax.ShapeDtypeStruct((b, c), F32),
        compiler_params=pltpu.CompilerParams(vmem_limit_bytes=VMEM_LIMIT), name="conv_sample")(
            state.transpose(1, 0, 2), u, cw, vec(p['conv_b']), vec(p['conv_ln_g']),
            vec(p['conv_ln_b']), vec(p['out_norm_conv']))


def _compress_weights(pe, w1, w2):
    assert N_SUB == 2
    w1r = w1.reshape(N_SUB, CMP_STRIDE, HEAD_DIM, CMP_HID).transpose(1, 2, 0, 3)
    eye = jnp.eye(N_KV_HEADS, dtype=w1.dtype)
    wf = jnp.einsum('sdmh,gk->sgdmkh', w1r, eye).reshape(CHUNK_W, N_SUB * N_KV_HEADS * CMP_HID)
    per = jnp.broadcast_to(pe.reshape(N_SUB, CMP_STRIDE, 1, HEAD_DIM),
                           (N_SUB, CMP_STRIDE, N_KV_HEADS, HEAD_DIM)).reshape(N_SUB, CHUNK_W)
    w2f = jnp.einsum('hd,gk->ghkd', w2, eye).reshape(N_KV_HEADS * CMP_HID, KV_W)
    return per, wf.astype(BF16), w2f.astype(BF16)


def _position_pair_weights(per, wf):
    half = N_KV_HEADS * CMP_HID
    pairs = CMP_STRIDE // 2
    pe_p = per.reshape(N_SUB * pairs, 2 * KV_W)
    w_p = wf.reshape(pairs, 2 * KV_W, N_SUB, half).transpose(2, 0, 1, 3).reshape(N_SUB * pairs, 2 * KV_W, half)
    return pe_p, w_p


def _compress_tail(a0, a1, w2_ref):
    n = a0.shape[0]
    h = a0 + pltpu.roll(a1, n - 1, 0)
    t = h * _sigmoid(h)
    tok = _dot(t.astype(BF16), w2_ref[...])
    return jnp.where(lax.broadcasted_iota(jnp.int32, tok.shape, 0) < n - 1, tok, 0.0)


def _cmp_key_post(tok, kng, bd1, c, sa, sb):
    return _rope(_group_rms(tok, kng[...], bd1[...]), c[...], sa[...], sb[...])


def _first_layer(rows_at, n_chunk, pe_ref, w1_ref):
    half = N_KV_HEADS * CMP_HID
    pairs = CMP_STRIDE // 2
    a0 = jnp.zeros((n_chunk, half), F32)
    a1 = jnp.zeros((n_chunk, half), F32)
    for sp in range(pairs):
        xs = jnp.concatenate([rows_at(2 * sp), rows_at(2 * sp + 1)], axis=1)
        a0 = a0 + _dot((xs + pe_ref[sp:sp + 1, :]).astype(BF16), w1_ref[sp])
        a1 = a1 + _dot((xs + pe_ref[pairs + sp:pairs + sp + 1, :]).astype(BF16), w1_ref[pairs + sp])
    return a0, a1


def _cmp_prompt_body(xk_ref, xv_ref, pek, w1k, w2k, kng, bd1, c, sa, sb, pev, w1v, w2v, ok_ref, ov_ref):
    n_chunk = ok_ref.shape[1]
    rows_of = lambda ref: lambda s: ref[0, pl.ds(s, n_chunk, stride=CMP_STRIDE), :]
    tok_k = _compress_tail(*_first_layer(rows_of(xk_ref), n_chunk, pek, w1k), w2k)
    tok_v = _compress_tail(*_first_layer(rows_of(xv_ref), n_chunk, pev, w1v), w2v)
    ok_ref[0] = _cmp_key_post(tok_k, kng, bd1, c, sa, sb)
    ov_ref[0] = tok_v.T


def _cmp_weights(p, name):
    per, wf, w2f = _compress_weights(p[f'cmp_{name}_pos'], p[f'cmp_{name}_w1'], p[f'cmp_{name}_w2'])
    return [*_position_pair_weights(per, wf), w2f]


def _cmp_prompt(kc, vc, p):
    b, t, _ = kc.shape
    n_chunk = t // CMP_STRIDE
    args = [*_cmp_weights(p, 'k'), *_cmp_key_consts(p, n_chunk), *_cmp_weights(p, 'v')]
    blk = pl.BlockSpec((1, t, KV_W), lambda i: (i, 0, 0))
    b3 = lambda i: (i, 0, 0)
    const = lambda a: pl.BlockSpec(a.shape, lambda i: (0,) * a.ndim)
    return pl.pallas_call(
        _cmp_prompt_body,
        out_shape=[jax.ShapeDtypeStruct((b, n_chunk, KV_W), F32), jax.ShapeDtypeStruct((b, KV_W, n_chunk), F32)],
        grid=(b,), in_specs=[blk, blk] + [const(a) for a in args],
        out_specs=[pl.BlockSpec((1, n_chunk, KV_W), b3), pl.BlockSpec((1, KV_W, n_chunk), b3)],
        compiler_params=_params(1), name="cmp_prompt")(kc, vc, *args)


def _cmp_key_consts(p, n_chunk):
    cmp_end = np.arange(n_chunk) * CMP_STRIDE + CMP_LEN - 1
    kng = jnp.tile(p['k_cmp_norm'], N_KV_HEADS).reshape(1, KV_W)
    return [kng, _blockdiag_ones(KV_W), *_rope_tables(cmp_end)]


def _cmp_sample_body(pt_ref, cache_hbm, pe_ref, w1_ref, w2_ref, *rest, n_pages, n_batch, is_key):
    key_consts, (o_ref, buf, tok, a_sc, sem) = rest[:-5], rest[-5:]
    b = pl.program_id(0)
    slot = b % 2
    page_len = cache_hbm.shape[2]
    n_pos = n_pages * page_len
    unroll = 8 if n_pages % 8 == 0 else 1

    def page_copy(page, i, sl):
        dst = pl.ds(pl.multiple_of(i * page_len, page_len), page_len)
        return pltpu.make_async_copy(cache_hbm.at[page], buf.at[sl, :, dst], sem.at[sl])

    def gather(seq, sl):
        def issue(i, carry):
            page_copy(pt_ref[seq * n_pages + i], i, sl).start()
            return carry
        lax.fori_loop(0, n_pages, issue, 0, unroll=unroll)

    @pl.when(b == 0)
    def _():
        gather(0, 0)

    @pl.when(b + 1 < n_batch)
    def _():
        gather(b + 1, 1 - slot)

    def wait(i, carry):
        page_copy(0, i, slot).wait()
        return carry
    lax.fori_loop(0, n_pages, wait, 0, unroll=unroll)

    n_blk = 4 if n_pos % (4 * 8 * page_len) == 0 else 1
    pos_blk = n_pos // n_blk
    chunk_blk = pos_blk // CMP_STRIDE
    xw = min(pos_blk, 8 * page_len)
    half_rows = CMP_STRIDE // 2
    for j in range(n_blk):
        p0 = j * pos_blk
        for i in range(pos_blk // xw):
            o = p0 + i * xw
            rows = buf[slot, :, o:o + xw].T.reshape(xw // CMP_STRIDE, CMP_STRIDE, KV_W)
            for h in range(2):
                tok[h, o // 2:(o + xw) // 2, :] = rows[:, h * half_rows:(h + 1) * half_rows, :].reshape(
                    xw // 2, KV_W)
        rows_at = lambda s: tok[s // half_rows, pl.ds(p0 // 2 + s % half_rows, chunk_blk, stride=half_rows), :]
        a0, a1 = _first_layer(rows_at, chunk_blk, pe_ref, w1_ref)
        a_sc[0, j * chunk_blk:(j + 1) * chunk_blk, :] = a0
        a_sc[1, j * chunk_blk:(j + 1) * chunk_blk, :] = a1
    out = _compress_tail(a_sc[0], a_sc[1], w2_ref)
    o_ref[0] = _cmp_key_post(out, *key_consts) if is_key else out


def _cmp_sample(page_table, cache_t, p, is_key):
    b, n_pages = page_table.shape
    page_len = cache_t.shape[2]
    n_pos = n_pages * page_len
    n_chunk = n_pos // CMP_STRIDE
    name = 'k' if is_key else 'v'
    args = _cmp_weights(p, name) + (_cmp_key_consts(p, n_chunk) if is_key else [])
    const = lambda a: pl.BlockSpec(a.shape, lambda i, pt: (0,) * a.ndim)
    grid_spec = pltpu.PrefetchScalarGridSpec(
        num_scalar_prefetch=1, grid=(b,),
        in_specs=[pl.BlockSpec(memory_space=pl.ANY)] + [const(a) for a in args],
        out_specs=pl.BlockSpec((1, n_chunk, KV_W), lambda i, pt: (i, 0, 0)),
        scratch_shapes=[pltpu.VMEM((2, KV_W, n_pos), F32), pltpu.VMEM((2, n_pos // 2, KV_W), F32),
                        pltpu.VMEM((N_SUB, n_chunk, N_KV_HEADS * CMP_HID), F32),
                        pltpu.SemaphoreType.DMA((2,))])
    return pl.pallas_call(
        functools.partial(_cmp_sample_body, n_pages=n_pages, n_batch=b, is_key=is_key),
        out_shape=jax.ShapeDtypeStruct((b, n_chunk, KV_W), F32), grid_spec=grid_spec,
        compiler_params=_params(1), name=f"cmp_sample_{name}")(page_table.reshape(-1), cache_t, *args)


def _cmp_to_sel(n_chunk, n_cmp, n_sel, width):
    cs = np.arange(n_chunk)[:, None] * CMP_STRIDE
    ss = np.arange(width)[None, :] * SEL_BLK
    ov = np.clip(np.minimum(cs + CMP_LEN, ss + SEL_BLK) - np.maximum(cs, ss), 0, None) / CMP_LEN
    ov = ov * (np.arange(n_chunk)[:, None] < n_cmp) * (np.arange(width)[None, :] < n_sel)
    return jnp.asarray(ov, dtype=BF16)


def _attn_qlanes_body(q_ref, gt_ref, kc_ref, vct_ref, ks_ref, vst_ref, kw_ref, vwt_ref, ovt_ref, eblk_ref,
                      o_ref, qa_sc, oct_sc, m_sc, l_sc, acc_sc, s_sc,
                      *, t, tq, kc_len, n_cmp, n_sel):
    off = pl.program_id(1) * tq
    cols = GRP * tq
    lane = lax.broadcasted_iota(jnp.int32, (tq, LANES), 1)
    qpos = off + lax.broadcasted_iota(jnp.int32, (1, tq), 1)
    q = q_ref[0] * (SCALE * LOG2E)
    for h in range(N_HEADS):
        g, r = divmod(h, GRP)
        qa_sc[g, :KV_W, r * tq:(r + 1) * tq] = _head_to_group_lanes(q, h, lane).T.astype(BF16)

    def attend(jobs):
        stats = []
        for st, k, vt, rhs, bias in jobs:
            n = k.shape[0]
            s = _dot(k, rhs)
            if bias is not None:
                s = s + bias()
            s_sc[st, :n, :] = s
            m_old = m_sc[st]
            m_new = jnp.maximum(m_old, jnp.max(s, axis=0, keepdims=True))
            m_sc[st] = m_new
            stats.append((m_new, jnp.exp2(m_old - m_new)))
        probs = []
        for (st, k, vt, rhs, bias), (m_new, alpha) in zip(jobs, stats):
            p = jnp.exp2(s_sc[st, :k.shape[0], :] - m_new)
            l_sc[st] = alpha * l_sc[st] + jnp.sum(p, axis=0, keepdims=True)
            probs.append(p.astype(BF16))
        for (st, k, vt, rhs, bias), (m_new, alpha), p in zip(jobs, stats, probs):
            acc_sc[st] = alpha * acc_sc[st] + _dot(vt, p)

    sel_state = lambda g: g
    win_state = lambda g: N_KV_HEADS + g
    m_sc[...] = jnp.full(m_sc.shape, NEG_INF, F32)
    l_sc[...] = jnp.zeros(l_sc.shape, F32)
    acc_sc[...] = jnp.zeros(acc_sc.shape, F32)

    n_chunk = kc_ref.shape[1]
    kcm = kc_ref[0].astype(BF16)
    vct = vct_ref[0].astype(BF16)
    nrow = lax.broadcasted_iota(jnp.int32, (n_chunk, tq), 0)
    mask_1 = jnp.logical_and(nrow * CMP_STRIDE + (CMP_LEN - 1) <= qpos, nrow < n_cmp)
    mask_c = jnp.concatenate([mask_1.astype(F32)] * GRP, axis=1) > 0.5
    psum = []
    for g in range(N_KV_HEADS):
        sc = jnp.where(mask_c, _dot(kcm, qa_sc[g, :KV_W, :]), NEG_INF)
        e = jnp.where(mask_c, jnp.exp2(sc - jnp.max(sc, axis=0, keepdims=True)), 0.0)
        pc = e / jnp.maximum(jnp.sum(e, axis=0, keepdims=True), 1e-30)
        oct_sc[g] = _dot(vct, pc.astype(BF16))
        psum.append(sum(pc[:, r * tq:(r + 1) * tq] for r in range(GRP)))

    n_sel_pad = -(-n_sel // 8) * 8
    jrow = lax.broadcasted_iota(jnp.int32, (n_sel_pad, tq), 0)
    cur = qpos // SEL_BLK
    forced = jnp.logical_or(jrow == 0, jnp.logical_or(jrow == cur, jrow == cur - 1))
    bonus = jnp.where(forced, FORCE_BONUS, 0.0)
    k_eff = min(TOPK, n_sel)
    ovt = ovt_ref[...]
    for g in range(N_KV_HEADS):
        hi = psum[g].astype(BF16)
        lo = (psum[g] - hi.astype(F32)).astype(BF16)
        imp = _dot(ovt, hi) + _dot(ovt, lo)
        score = jnp.where(jrow <= cur, imp + bonus, NEG_INF)
        rank = jnp.zeros((n_sel_pad, tq), F32)
        for j in range(n_sel):
            row = score[j:j + 1, :]
            tie = jnp.where(jrow > j, 1.0, 0.0)
            rank = rank + jnp.where(row > score, 1.0, jnp.where(row == score, tie, 0.0))
        chosen = jnp.logical_and(rank < k_eff, jrow <= cur)
        for part, keep in enumerate((jnp.logical_and(chosen, jrow < off // SEL_BLK), chosen)):
            r0 = KV_W + part * LANES
            pen = jnp.where(keep, 0.0, -MASK_PENALTY).astype(BF16)
            qa_sc[g, r0:r0 + n_sel_pad, :] = jnp.concatenate([pen] * GRP, axis=1)
            qa_sc[g, r0 + n_sel_pad:r0 + LANES, :] = jnp.zeros((LANES - n_sel_pad, cols), BF16)

    tq_off = pl.multiple_of(off, tq)
    def triangle(keep_front):
        kk = lax.broadcasted_iota(jnp.int32, (tq, cols), 0)
        qq = lax.broadcasted_iota(jnp.int32, (tq, cols), 1) & (tq - 1)
        return jnp.where((kk > qq) if keep_front else (kk <= qq), 0.0, -jnp.inf)

    tri_own = functools.partial(triangle, False)
    tri_front = functools.partial(triangle, True)

    k_own = jnp.concatenate(
        [ks_ref[0, pl.ds(tq_off, tq), :].astype(BF16), eblk_ref[pl.ds(tq_off, tq), :]], axis=1)
    vt_own = vst_ref[0, :, pl.ds(tq_off, tq)].astype(BF16)
    kw_own = kw_ref[0, pl.ds(tq_off, tq), :].astype(BF16)
    vwt_own = vwt_ref[0, :, pl.ds(tq_off, tq)].astype(BF16)
    groups = range(N_KV_HEADS)
    q_own = lambda g: jnp.concatenate([qa_sc[g, :KV_W, :], qa_sc[g, KV_W + LANES:, :]], axis=0)
    attend([(sel_state(g), k_own, vt_own, q_own(g), tri_own) for g in groups]
           + [(win_state(g), kw_own, vwt_own, qa_sc[g, :KV_W, :], tri_own) for g in groups])

    for ci in range(t // kc_len):
        @pl.when(ci * kc_len < off)
        def _(ci=ci):
            base = ci * kc_len
            k_aug = jnp.concatenate(
                [ks_ref[0, base:base + kc_len, :].astype(BF16), eblk_ref[base:base + kc_len, :]], axis=1)
            vt = vst_ref[0, :, base:base + kc_len].astype(BF16)
            attend([(sel_state(g), k_aug, vt, qa_sc[g, :KV_W + LANES, :], None) for g in groups])

    def window_keys(start, n, bias):
        start = pl.multiple_of(start, tq)
        kw = kw_ref[0, pl.ds(start, n), :].astype(BF16)
        vwt = vwt_ref[0, :, pl.ds(start, n)].astype(BF16)
        attend([(win_state(g), kw, vwt, qa_sc[g, :KV_W, :], bias) for g in groups])

    @pl.when(off >= WINDOW)
    def _():
        window_keys(off - WINDOW, tq, tri_front)
        if WINDOW > tq:
            window_keys(off - WINDOW + tq, WINDOW - tq, None)

    for i in range(1, WINDOW // tq):
        @pl.when(jnp.logical_and(off < WINDOW, off - i * tq >= 0))
        def _(i=i):
            window_keys(off - i * tq, tq, None)

    gtt = gt_ref[0].T
    outs = []
    for g in range(N_KV_HEADS):
        o_w = acc_sc[win_state(g)] / jnp.maximum(l_sc[win_state(g)], 1e-30)
        o_s = acc_sc[sel_state(g)] / jnp.maximum(l_sc[sel_state(g)], 1e-30)
        o_c = oct_sc[g]
        for r in range(GRP):
            h = g * GRP + r
            cs = slice(r * tq, (r + 1) * tq)
            o_h = (gtt[3 * h:3 * h + 1, :] * o_c[:, cs] + gtt[3 * h + 1:3 * h + 2, :] * o_s[:, cs]
                   + gtt[3 * h + 2:3 * h + 3, :] * o_w[:, cs])
            outs.append(o_h.T)
    o_ref[0] = jnp.concatenate(
        [_pair_tile(outs[2 * p], outs[2 * p + 1], 2 * p, lane) for p in range(N_HEADS // 2)], axis=-1)


def _attn_qlanes(q, gt, kcmp, vcmp_t, ks, vs_t, kw, vw_t, tq=512, kc_len=512):
    b, t, _ = q.shape
    n_chunk = kcmp.shape[1]
    n_cmp = n_chunk - N_SUB + 1
    n_sel = -(-t // SEL_BLK)
    assert t % kc_len == 0 and t >= WINDOW + tq and n_sel <= LANES and n_chunk % LANES == 0
    assert tq % LANES == 0 and WINDOW % tq == 0 and tq % SEL_BLK == 0
    n_sel_pad = -(-n_sel // 8) * 8
    ov_t = _cmp_to_sel(n_chunk, n_cmp, n_sel, n_sel_pad).T
    assert tq & (tq - 1) == 0
    eblk = jnp.asarray(np.arange(t)[:, None] // SEL_BLK == np.arange(LANES)[None, :], dtype=BF16)
    qblk = lambda i, j: (i, j, 0)
    full = lambda i, j: (i, 0, 0)
    const = lambda a: pl.BlockSpec(a.shape, lambda i, j: (0,) * a.ndim)
    tok = pl.BlockSpec((1, t, KV_W), full)
    tr = pl.BlockSpec((1, KV_W, t), full)
    cols = GRP * tq
    n_states = 2 * N_KV_HEADS
    assert kc_len >= WINDOW - tq
    body = functools.partial(_attn_qlanes_body, t=t, tq=tq, kc_len=kc_len, n_cmp=n_cmp, n_sel=n_sel)
    return pl.pallas_call(
        body, out_shape=jax.ShapeDtypeStruct((b, t, D_ATT), F32), grid=(b, t // tq),
        in_specs=[pl.BlockSpec((1, tq, D_ATT), qblk), pl.BlockSpec((1, tq, LANES), qblk),
                  pl.BlockSpec((1, n_chunk, KV_W), full), pl.BlockSpec((1, KV_W, n_chunk), full),
                  tok, tr, tok, tr, const(ov_t), const(eblk)],
        out_specs=pl.BlockSpec((1, tq, D_ATT), qblk),
        scratch_shapes=[pltpu.VMEM((N_KV_HEADS, KV_W + 2 * LANES, cols), BF16),
                        pltpu.VMEM((N_KV_HEADS, KV_W, cols), F32), pltpu.VMEM((n_states, 1, cols), F32),
                        pltpu.VMEM((n_states, 1, cols), F32), pltpu.VMEM((n_states, KV_W, cols), F32),
                        pltpu.VMEM((n_states, kc_len, cols), F32)],
        compiler_params=_params(2), name="attn_prompt")(
            q, gt, kcmp, vcmp_t, ks, vs_t, kw, vw_t, ov_t, eblk)


def _sample_heads(q_row):
    lane = lax.broadcasted_iota(jnp.int32, (1, LANES), 1)
    return jnp.concatenate([_head_to_group_lanes(q_row, h, lane) for h in range(N_HEADS)], axis=0)


def _attn_s1_body(q_ref, kc_ref, vc_ref, ov_ref, oc_ref, idx_ref, *, n_cmp, n_sel, q_pos):
    qh = _sample_heads(q_ref[0])
    n_chunk = kc_ref.shape[1]
    ncol = lax.broadcasted_iota(jnp.int32, (N_HEADS, n_chunk), 1)
    mask = jnp.logical_and(ncol * CMP_STRIDE + (CMP_LEN - 1) <= q_pos, ncol < n_cmp)
    pc = _masked_softmax(_dot_t(qh.astype(BF16), kc_ref[0].astype(BF16)) * SCALE, mask)
    oc_ref[0] = _dot(pc.astype(BF16), vc_ref[0].astype(BF16))
    rows = [jnp.sum(pc[g * GRP:(g + 1) * GRP, :], axis=0, keepdims=True) for g in range(N_KV_HEADS)]
    psum = jnp.concatenate(rows + [jnp.zeros((N_HEADS - N_KV_HEADS, n_chunk), F32)], axis=0)
    imp = _split_dot(psum, ov_ref[...])
    width = imp.shape[1]
    jl = lax.broadcasted_iota(jnp.int32, (N_HEADS, width), 1)
    cur = q_pos // SEL_BLK
    forced = jnp.logical_or(jl == 0, jnp.logical_or(jl == cur, jl == cur - 1))
    score = jnp.where(jl <= cur, imp + jnp.where(forced, FORCE_BONUS, 0.0), NEG_INF)
    n_rows = -(-n_sel // 8) * 8
    ii = lax.broadcasted_iota(jnp.int32, (n_rows, width), 0)
    jj = lax.broadcasted_iota(jnp.int32, (n_rows, width), 1)
    slot = lax.broadcasted_iota(jnp.int32, (n_rows, LANES), 1).astype(F32)
    blk = lax.broadcasted_iota(jnp.int32, (n_rows, LANES), 0).astype(F32)
    tie = jnp.where(jj < ii, 1.0, 0.0)
    out_rows = []
    for g in range(N_KV_HEADS):
        row = jnp.broadcast_to(score[g:g + 1, :], (n_rows, width))
        col = jnp.sum(jnp.where(ii == jj, row, 0.0), axis=1, keepdims=True)
        beats = jnp.where(row > col, 1.0, jnp.where(row == col, tie, 0.0))
        rank = jnp.sum(beats, axis=1, keepdims=True)
        out_rows.append(jnp.sum(jnp.where(rank == slot, blk, 0.0), axis=0, keepdims=True))
    out_rows.append(jnp.zeros((N_HEADS - N_KV_HEADS, LANES), F32))
    idx_ref[0] = jnp.concatenate(out_rows, axis=0).astype(jnp.int32)


def _attn_s1(q, kcmp, vcmp, q_pos, n_sel):
    b = q.shape[0]
    n_chunk = kcmp.shape[1]
    n_cmp = n_chunk - N_SUB + 1
    width = -(-n_sel // LANES) * LANES
    ov = _cmp_to_sel(n_chunk, n_cmp, n_sel, width)
    blk3 = lambda i: (i, 0, 0)
    cm = pl.BlockSpec((1, n_chunk, KV_W), blk3)
    out = pl.BlockSpec((1, N_HEADS, LANES), blk3)
    return pl.pallas_call(
        functools.partial(_attn_s1_body, n_cmp=n_cmp, n_sel=n_sel, q_pos=q_pos),
        out_shape=[jax.ShapeDtypeStruct((b, N_HEADS, LANES), F32),
                   jax.ShapeDtypeStruct((b, N_HEADS, LANES), jnp.int32)],
        grid=(b,),
        in_specs=[pl.BlockSpec((1, 1, D_ATT), blk3), cm, cm, pl.BlockSpec(ov.shape, lambda i: (0, 0))],
        out_specs=[out, out], compiler_params=_params(1), name="attn_sample_select")(
            q.reshape(b, 1, D_ATT), kcmp, vcmp, ov)


def _extra_key_attention(qh, k_t, v_t, mask, k_new, v_new):
    qf = qh.astype(F32)
    s = jnp.where(mask, _dot(qh, k_t.astype(BF16)) * SCALE, NEG_INF)
    s_new = jnp.sum(qf * k_new, axis=-1, keepdims=True) * SCALE
    m = jnp.maximum(jnp.max(s, axis=-1, keepdims=True), s_new)
    e = jnp.where(mask, jnp.exp(s - m), 0.0)
    e_new = jnp.exp(s_new - m)
    denom = jnp.maximum(jnp.sum(e, axis=-1, keepdims=True) + e_new, 1e-30)
    return (_dot_t(e.astype(BF16), v_t.astype(BF16)) + e_new * v_new) / denom


def _attn_s2_body(idx_ref, pt_ref, q_ref, gt_ref, oc_ref, ksn_ref, vsn_ref, kwn_ref, vwn_ref,
                  kws_ref, vws_ref, kcache, vcache, o_ref, kbuf, vbuf, sem,
                  *, n_pages, n_batch, n_cache_blk, q_pos):
    b = pl.program_id(0)
    slot = b % 2
    page_len = kcache.shape[2]
    per_page = page_len // SEL_BLK

    def block_id(seq, g, k):
        return idx_ref[(seq * N_KV_HEADS + g) * TOPK + k]

    def copies(seq, sl, g, k):
        blk = jnp.minimum(block_id(seq, g, k), n_cache_blk - 1)
        page = pt_ref[seq * n_pages + blk // per_page]
        dst = pl.ds(k * page_len, page_len)
        return (pltpu.make_async_copy(kcache.at[page], kbuf.at[sl, g, :, dst], sem.at[sl, 0]),
                pltpu.make_async_copy(vcache.at[page], vbuf.at[sl, g, :, dst], sem.at[sl, 1]))

    def for_all_copies(seq, sl, fn):
        for g in range(N_KV_HEADS):
            for k in range(TOPK):
                for cp in copies(seq, sl, g, k):
                    fn(cp)

    @pl.when(b == 0)
    def _():
        for_all_copies(0, 0, lambda cp: cp.start())

    @pl.when(b + 1 < n_batch)
    def _():
        for_all_copies(b + 1, 1 - slot, lambda cp: cp.start())

    for_all_copies(b, slot, lambda cp: cp.wait())
    blks = [[block_id(b, g, k) for k in range(TOPK)] for g in range(N_KV_HEADS)]

    qh = _sample_heads(q_ref[0]).astype(BF16)
    n_keys = TOPK * page_len
    row_grp = lax.broadcasted_iota(jnp.int32, (N_HEADS, n_keys), 0) // GRP
    key = lax.broadcasted_iota(jnp.int32, (N_HEADS, n_keys), 1)
    key_slot = key // page_len
    blk_in_page = (key % page_len) // SEL_BLK
    keep = jnp.zeros((N_HEADS, n_keys), F32)
    for g in range(N_KV_HEADS):
        for k in range(TOPK):
            blk = blks[g][k]
            in_cache = jnp.where(blk < n_cache_blk, 1.0, 0.0)
            hit = jnp.where(blk_in_page == blk % per_page, in_cache, 0.0)
            keep = jnp.where(jnp.logical_and(row_grp == g, key_slot == k), hit, keep)
    mask_s = keep > 0.5
    o_sel = [_extra_key_attention(qh, kbuf.at[slot, g][...], vbuf.at[slot, g][...], mask_s,
                                  ksn_ref[0], vsn_ref[0])
             for g in range(N_KV_HEADS)]
    out_grp = lax.broadcasted_iota(jnp.int32, (N_HEADS, LANES), 0) // GRP
    o_s = jnp.where(out_grp == 0, o_sel[0], o_sel[1])

    n_win = kws_ref.shape[2]
    kpos = (q_pos - n_win) + lax.broadcasted_iota(jnp.int32, (N_HEADS, n_win), 1)
    dist = q_pos - kpos
    mask_w = jnp.logical_and(dist >= 0, dist < WINDOW)
    o_w = _extra_key_attention(qh, kws_ref[0], vws_ref[0], mask_w, kwn_ref[0], vwn_ref[0])

    lane8 = lax.broadcasted_iota(jnp.int32, (N_HEADS, LANES), 1)
    head8 = lax.broadcasted_iota(jnp.int32, (N_HEADS, LANES), 0)
    gt = jnp.broadcast_to(gt_ref[0], (N_HEADS, LANES))
    gate = [jnp.sum(jnp.where(lane8 == 3 * head8 + j, gt, 0.0), axis=-1, keepdims=True) for j in range(3)]
    o = gate[0] * oc_ref[0] + gate[1] * o_s + gate[2] * o_w
    lane = lax.broadcasted_iota(jnp.int32, (1, LANES), 1)
    o_ref[0] = jnp.concatenate(
        [_pair_tile(o[2 * p:2 * p + 1, :], o[2 * p + 1:2 * p + 2, :], 2 * p, lane)
         for p in range(N_HEADS // 2)], axis=-1)


def _attn_s2(idx, page_table, q, gt, oc, ks_new, vs_new, kw_new, vw_new, kw_state_t, vw_state_t,
             cache_k_t, cache_v_t, q_pos):
    b, n_pages = page_table.shape
    page_len = cache_k_t.shape[2]
    n_win = kw_state_t.shape[2]
    assert n_win <= WINDOW
    blk3 = lambda i, *_: (i, 0, 0)
    row = lambda w: pl.BlockSpec((1, 1, w), blk3)
    grid_spec = pltpu.PrefetchScalarGridSpec(
        num_scalar_prefetch=2, grid=(b,),
        in_specs=[row(D_ATT), row(LANES), pl.BlockSpec((1, N_HEADS, LANES), blk3),
                  row(KV_W), row(KV_W), row(KV_W), row(KV_W),
                  pl.BlockSpec((1, KV_W, n_win), blk3), pl.BlockSpec((1, KV_W, n_win), blk3),
                  pl.BlockSpec(memory_space=pl.ANY), pl.BlockSpec(memory_space=pl.ANY)],
        out_specs=row(D_ATT),
        scratch_shapes=[pltpu.VMEM((2, N_KV_HEADS, KV_W, TOPK * page_len), F32),
                        pltpu.VMEM((2, N_KV_HEADS, KV_W, TOPK * page_len), F32),
                        pltpu.SemaphoreType.DMA((2, 2))])
    r3 = lambda a: a.reshape(b, 1, a.shape[-1])
    body = functools.partial(_attn_s2_body, n_pages=n_pages, n_batch=b,
                             n_cache_blk=n_pages * (page_len // SEL_BLK),
                             q_pos=q_pos)
    return pl.pallas_call(
        body, out_shape=jax.ShapeDtypeStruct((b, 1, D_ATT), F32), grid_spec=grid_spec,
        compiler_params=_params(1), name="attn_sample")(
            idx.reshape(-1), page_table.reshape(-1), r3(q), r3(gt), oc, r3(ks_new), r3(vs_new),
            r3(kw_new), r3(vw_new), kw_state_t, vw_state_t, cache_k_t, cache_v_t).reshape(b, D_ATT)


def _prompt_layer(x, p, tm=512, tm_proj=1024):
    b, t, d = x.shape
    x2 = _ffn(x.reshape(b * t, d), p['ffn1_norm'], p['ffn1_w_in'], p['ffn1_w_out'], None, tm)
    u, q, gt, kc, vc, ks, kw, kc_t, vc_t, ks_t, vs_t, kw_t, vw_t = _proj(
        x2, p, _rope_tables(np.arange(t)), tm_proj, seq_len=t)
    r3 = lambda a: a.reshape(b, t, a.shape[-1])
    u3 = r3(u)
    cn = _conv_prompt(u3, p)
    kcmp, vcmp_t = _cmp_prompt(r3(kc), r3(vc), p)
    o = _attn_qlanes(r3(q), r3(gt), kcmp, vcmp_t, r3(ks), vs_t, r3(kw), vw_t)
    y = _ffn(x2, p['ffn2_norm'], p['ffn2_w_in'], p['ffn2_w_out'], p['final_norm'], tm,
             mix=(cn.reshape(b * t, C_CONV), o.reshape(b * t, D_ATT), p['out_norm_attn'], p['w_out']))
    r4 = lambda a: a.reshape(b, N_KV_HEADS, HEAD_DIM, a.shape[-1]).transpose(0, 3, 1, 2)
    keep = min(WINDOW, t)
    state = (r4(kc_t), r4(vc_t), r4(ks_t), r4(vs_t), r4(kw_t[:, :, t - keep:]), r4(vw_t[:, :, t - keep:]),
             u3[:, t - (CONV_W - 1):])
    return y.reshape(b, t, d), state


def _sample_layer(x, p, cache_k_cmp, cache_v_cmp, cache_k_sel, cache_v_sel, kw_state, vw_state,
                  conv_state, page_table):
    b, t, d = x.shape
    assert t == 1
    past_len = page_table.shape[1] * cache_k_cmp.shape[1]
    x2 = _ffn(x.reshape(b, d), p['ffn1_norm'], p['ffn1_w_in'], p['ffn1_w_out'], None, b)
    tabs = _rope_tables(np.full((b,), past_len))
    u, q, gt, kc, vc, ks, vs, kw, vw = _proj(x2, p, tabs, b)
    cn = _conv_sample(conv_state, u, p)
    keys_on_lanes = lambda a: a.transpose(0, 2, 3, 1).reshape(a.shape[0], KV_W, a.shape[1])
    kcmp = _cmp_sample(page_table, keys_on_lanes(cache_k_cmp), p, True)
    vcmp = _cmp_sample(page_table, keys_on_lanes(cache_v_cmp), p, False)
    n_sel = -(-(past_len + 1) // SEL_BLK)
    oc, idx = _attn_s1(q, kcmp, vcmp, past_len, n_sel)
    n_win = kw_state.shape[1]
    o = _attn_s2(idx[:, :N_KV_HEADS, :TOPK], page_table, q, gt, oc, ks, vs, kw, vw,
                 keys_on_lanes(kw_state), keys_on_lanes(vw_state),
                 keys_on_lanes(cache_k_sel), keys_on_lanes(cache_v_sel), past_len)
    y = _ffn(x2, p['ffn2_norm'], p['ffn2_w_in'], p['ffn2_w_out'], p['final_norm'], b,
             mix=(cn, o, p['out_norm_attn'], p['w_out']))
    r4 = lambda a: a.reshape(b, 1, N_KV_HEADS, HEAD_DIM)
    keep = min(WINDOW, n_win + 1)
    new_kw = jnp.concatenate([kw_state, r4(kw)], axis=1)[:, n_win + 1 - keep:]
    new_vw = jnp.concatenate([vw_state, r4(vw)], axis=1)[:, n_win + 1 - keep:]
    new_conv = jnp.concatenate([conv_state, u[:, None, :]], axis=1)[:, 1:]
    return y.reshape(b, 1, d), (r4(kc), r4(vc), r4(ks), r4(vs), new_kw, new_vw, new_conv)


_PARAM_NAMES = ('ffn1_norm', 'ffn1_w_in', 'ffn1_w_out', 'mix_norm', 'w_in', 'conv_w', 'conv_b', 'conv_ln_g',
                'conv_ln_b', 'q_norm', 'k_cmp_norm', 'k_sel_norm', 'k_win_norm', 'cmp_k_pos', 'cmp_k_w1',
                'cmp_k_w2', 'cmp_v_pos', 'cmp_v_w1', 'cmp_v_w2', 'out_norm_conv', 'out_norm_attn', 'w_out',
                'ffn2_norm', 'ffn2_w_in', 'ffn2_w_out', 'final_norm')


def kernel(x_prompt, x_sample, cache_k_cmp, cache_v_cmp, cache_k_sel, cache_v_sel, state_k_win, state_v_win,
           state_conv, page_table, ffn1_norm, ffn1_w_in, ffn1_w_out, mix_norm, w_in, conv_w, conv_b, conv_ln_g,
           conv_ln_b, q_norm, k_cmp_norm, k_sel_norm, k_win_norm, cmp_k_pos, cmp_k_w1, cmp_k_w2, cmp_v_pos,
           cmp_v_w1, cmp_v_w2, out_norm_conv, out_norm_attn, w_out, ffn2_norm, ffn2_w_in, ffn2_w_out, final_norm):
    stacked = dict(zip(_PARAM_NAMES, (
        ffn1_norm, ffn1_w_in, ffn1_w_out, mix_norm, w_in, conv_w, conv_b, conv_ln_g, conv_ln_b, q_norm,
        k_cmp_norm, k_sel_norm, k_win_norm, cmp_k_pos, cmp_k_w1, cmp_k_w2, cmp_v_pos, cmp_v_w1, cmp_v_w2,
        out_norm_conv, out_norm_attn, w_out, ffn2_norm, ffn2_w_in, ffn2_w_out, final_norm)))
    depth = ffn1_norm.shape[0]
    yp, ys = x_prompt, x_sample
    new_p, new_s = [], []
    for l in range(depth):
        p = {k: v[l] for k, v in stacked.items()}
        yp, st_p = _prompt_layer(yp, p)
        new_p.append(st_p)
        ys, st_s = _sample_layer(ys, p, cache_k_cmp[l], cache_v_cmp[l], cache_k_sel[l], cache_v_sel[l],
                                 state_k_win[l], state_v_win[l], state_conv[l], page_table)
        new_s.append(st_s)
    outs_p = [jnp.stack(tup) for tup in zip(*new_p)]
    outs_s = [jnp.stack(tup) for tup in zip(*new_s)]
    return (yp, ys, *outs_p, *outs_s)
```

```python
import functools

import numpy as np
import jax
import jax.numpy as jnp
from jax import lax
from jax.experimental import pallas as pl
from jax.experimental.pallas import tpu as pltpu

F32 = jnp.float32
BF16 = jnp.bfloat16

HEAD_DIM = 64
N_HEADS = 8
N_KV_HEADS = 2
GRP = N_HEADS // N_KV_HEADS
KV_W = N_KV_HEADS * HEAD_DIM
C_CONV = 512
D_ATT = 512
CONV_W = 31
CMP_LEN = 32
CMP_STRIDE = 16
N_SUB = CMP_LEN // CMP_STRIDE
CMP_HID = 2 * HEAD_DIM
SEL_BLK = 64
TOPK = 16
WINDOW = 512
ROT_DIM = HEAD_DIM // 4
ROT_HALF = ROT_DIM // 2
ROPE_THETA = 500000.0
NEG_INF = -1e30
MASK_PENALTY = 2.0 ** 100
FORCE_BONUS = 1e4
EPS = 1e-6
SCALE = HEAD_DIM ** -0.5
LOG2E = 1.4426950408889634
LANES = 128
CHUNK_W = CMP_STRIDE * KV_W
VMEM_LIMIT = 56 * 1024 * 1024


def _params(n_grid_dims):
    return pltpu.CompilerParams(
        dimension_semantics=("arbitrary",) * n_grid_dims, vmem_limit_bytes=VMEM_LIMIT)


def _dot(a, b):
    return jnp.dot(a, b, preferred_element_type=F32)


def _dot_t(a, b):
    return lax.dot_general(a, b, (((1,), (1,)), ((), ())), preferred_element_type=F32)


def _split_dot(x, m_bf16):
    hi = x.astype(BF16)
    lo = (x - hi.astype(F32)).astype(BF16)
    return _dot(hi, m_bf16) + _dot(lo, m_bf16)


def _sigmoid(x):
    return 1.0 / (1.0 + jnp.exp(-x))


def _rms(x, g):
    return x * lax.rsqrt(jnp.mean(x * x, axis=-1, keepdims=True) + EPS) * g


def _group_rms(x, g_tiled, ones_blockdiag):
    ss = _split_dot(x * x, ones_blockdiag)
    return x * lax.rsqrt(ss * (1.0 / HEAD_DIM) + EPS) * g_tiled


def _rope(x, c, sa, sb):
    n = x.shape[-1]
    return x * c + pltpu.roll(x, n - ROT_HALF, 1) * sa + pltpu.roll(x, ROT_HALF, 1) * sb


def _tile_lanes(t, reps):
    return jnp.concatenate([t] * reps, axis=-1) if reps > 1 else t


def _masked_softmax(s, mask):
    s = jnp.where(mask, s, NEG_INF)
    m = jnp.max(s, axis=-1, keepdims=True)
    e = jnp.where(mask, jnp.exp(s - m), 0.0)
    return e / jnp.maximum(jnp.sum(e, axis=-1, keepdims=True), 1e-30)


def _head_to_group_lanes(q, h, lane):
    g, p = h // GRP, h // 2
    qp = q[:, LANES * p:LANES * (p + 1)]
    src = qp if (h % 2) == g else pltpu.roll(qp, HEAD_DIM, 1)
    keep = (lane < HEAD_DIM) if g == 0 else (lane >= HEAD_DIM)
    return jnp.where(keep, src, 0.0)


def _pair_tile(o_even, o_odd, h_even, lane):
    g = h_even // GRP
    a = o_even if g == 0 else pltpu.roll(o_even, HEAD_DIM, 1)
    b = pltpu.roll(o_odd, HEAD_DIM, 1) if g == 0 else o_odd
    return jnp.where(lane < HEAD_DIM, a, b)


def _ffn_body(*refs, n_chunk, tf, ff, final, mixed):
    refs = list(refs)
    x_ref = refs.pop(0)
    x = x_ref[...]
    if mixed:
        c_ref, a_ref, ag_ref, wc_ref, wa_ref = refs[:5]
        refs = refs[5:]
        x = (x + _dot(c_ref[...].astype(BF16), wc_ref[...])
             + _dot(_rms(a_ref[...], ag_ref[...]).astype(BF16), wa_ref[...]))
    g_ref, wi_ref, wo_ref = refs[:3]
    fg_ref = refs[3] if final else None
    o_ref = refs[-1]
    xn = _rms(x, g_ref[...]).astype(BF16)
    acc = jnp.zeros(x.shape, F32)
    for c in range(n_chunk):
        a = _dot(xn, wi_ref[:, c * tf:(c + 1) * tf])
        b = _dot(xn, wi_ref[:, ff + c * tf:ff + (c + 1) * tf])
        h = (a * _sigmoid(a)) * b
        acc = acc + _dot(h.astype(BF16), wo_ref[c * tf:(c + 1) * tf, :])
    y = x + 0.5 * acc
    if final:
        y = _rms(y, fg_ref[...])
    o_ref[...] = y


def _ffn(x2, norm_g, w_in, w_out, final_g, tm, mix=None):
    rows, d = x2.shape
    ff = w_out.shape[0]
    tf = 256
    assert rows % tm == 0 and ff % tf == 0 and w_in.shape == (d, 2 * ff)
    const = lambda i: (0, 0)
    row = lambda i: (i, 0)
    single = pl.Buffered(1)
    in_specs = [pl.BlockSpec((tm, d), row)]
    args = [x2]
    if mix is not None:
        cn, att, att_g, w_mix = mix
        w_mix = w_mix.astype(BF16)
        in_specs += [pl.BlockSpec((tm, C_CONV), row), pl.BlockSpec((tm, D_ATT), row),
                     pl.BlockSpec((1, D_ATT), const), pl.BlockSpec((C_CONV, d), const),
                     pl.BlockSpec((D_ATT, d), const)]
        args += [cn, att, att_g.reshape(1, D_ATT), w_mix[:C_CONV], w_mix[C_CONV:]]
    in_specs += [
        pl.BlockSpec((1, d), const),
        pl.BlockSpec((d, 2 * ff), const, pipeline_mode=single),
        pl.BlockSpec((ff, d), const, pipeline_mode=single),
    ]
    args += [norm_g.reshape(1, d), w_in.astype(BF16), w_out.astype(BF16)]
    if final_g is not None:
        in_specs.append(pl.BlockSpec((1, d), const))
        args.append(final_g.reshape(1, d))
    body = functools.partial(_ffn_body, n_chunk=ff // tf, tf=tf, ff=ff, final=final_g is not None,
                             mixed=mix is not None)
    return pl.pallas_call(
        body, out_shape=jax.ShapeDtypeStruct((rows, d), F32), grid=(rows // tm,),
        in_specs=in_specs, out_specs=pl.BlockSpec((tm, d), lambda i: (i, 0)),
        compiler_params=_params(1), name="ffn")(*args)


_SEG_GLU = 2 * C_CONV
_SEG_Q = _SEG_GLU + D_ATT
N_IN = _SEG_Q + 6 * KV_W + 3 * N_HEADS
N_IN_PAD = _SEG_Q + 7 * KV_W


def _proj_body(x_ref, g_ref, w_ref, qg_ref, ksg_ref, kwg_ref, bd4_ref, bd1_ref, c_ref, sa_ref, sb_ref,
               u_ref, q_ref, gt_ref, kc_ref, vc_ref, *kv_refs, transposed):
    xn = _rms(x_ref[...], g_ref[...]).astype(BF16)

    def seg(lo, width):
        return _dot(xn, w_ref[:, lo:lo + width])

    c, sa, sb = c_ref[...], sa_ref[...], sb_ref[...]
    reps = D_ATT // LANES
    glu = seg(0, _SEG_GLU)
    u_ref[...] = glu[:, :C_CONV] * _sigmoid(glu[:, C_CONV:])
    q = _group_rms(seg(_SEG_GLU, D_ATT), qg_ref[...], bd4_ref[...])
    q_ref[...] = _rope(q, _tile_lanes(c, reps), _tile_lanes(sa, reps), _tile_lanes(sb, reps))
    base = _SEG_Q
    kc = seg(base, KV_W)
    vc = seg(base + KV_W, KV_W)
    ks = _rope(_group_rms(seg(base + 2 * KV_W, KV_W), ksg_ref[...], bd1_ref[...]), c, sa, sb)
    vs = seg(base + 3 * KV_W, KV_W)
    kw = _rope(_group_rms(seg(base + 4 * KV_W, KV_W), kwg_ref[...], bd1_ref[...]), c, sa, sb)
    vw = seg(base + 5 * KV_W, KV_W)
    gt_ref[...] = _sigmoid(seg(base + 6 * KV_W, KV_W))
    kc_ref[...] = kc
    vc_ref[...] = vc
    if transposed:
        kv_refs[0][...] = ks
        kv_refs[1][...] = kw
        for ref, val in zip(kv_refs[2:], (kc, vc, ks, vs, kw, vw)):
            ref[0] = val.T
    else:
        for ref, val in zip(kv_refs, (ks, vs, kw, vw)):
            ref[...] = val


def _blockdiag_ones(width):
    idx = np.arange(width) // HEAD_DIM
    return jnp.asarray(idx[:, None] == idx[None, :], dtype=BF16)


def _rope_tables(pos):
    pos = np.asarray(pos, dtype=np.float64)
    inv = ROPE_THETA ** (-(np.arange(ROT_HALF, dtype=np.float64) * 2.0 / ROT_DIM))
    ang = pos[:, None] * inv[None, :]
    cos, sin = np.cos(ang).astype(np.float32), np.sin(ang).astype(np.float32)
    n = pos.shape[0]
    ones = np.ones((n, HEAD_DIM - ROT_DIM), np.float32)
    zeros = np.zeros((n, HEAD_DIM - ROT_DIM), np.float32)
    z8 = np.zeros((n, ROT_HALF), np.float32)
    c = np.concatenate([cos, cos, ones], axis=1)
    sa = np.concatenate([-sin, z8, zeros], axis=1)
    sb = np.concatenate([z8, sin, zeros], axis=1)
    return tuple(jnp.asarray(np.tile(t, (1, N_KV_HEADS))) for t in (c, sa, sb))


def _proj(x2, p, tabs, tm, seq_len=None):
    rows, d = x2.shape
    n_tab = tabs[0].shape[0] // tm
    w = jnp.pad(p['w_in'], ((0, 0), (0, N_IN_PAD - N_IN))).astype(BF16)
    const = lambda i: (0, 0)
    row = lambda i: (i, 0)
    tab = lambda i: (i % n_tab, 0)
    in_specs = [
        pl.BlockSpec((tm, d), row), pl.BlockSpec((1, d), const),
        pl.BlockSpec((d, N_IN_PAD), const, pipeline_mode=pl.Buffered(1)),
        pl.BlockSpec((1, D_ATT), const), pl.BlockSpec((1, KV_W), const), pl.BlockSpec((1, KV_W), const),
        pl.BlockSpec((D_ATT, D_ATT), const), pl.BlockSpec((KV_W, KV_W), const),
        pl.BlockSpec((tm, LANES), tab), pl.BlockSpec((tm, LANES), tab), pl.BlockSpec((tm, LANES), tab),
    ]
    widths = [C_CONV, D_ATT, LANES, KV_W, KV_W] + [KV_W] * (4 if seq_len is None else 2)
    out_shape = [jax.ShapeDtypeStruct((rows, wd), F32) for wd in widths]
    out_specs = [pl.BlockSpec((tm, wd), row) for wd in widths]
    if seq_len is not None:
        nt = seq_len // tm
        assert seq_len % tm == 0 and rows % seq_len == 0
        out_shape += [jax.ShapeDtypeStruct((rows // seq_len, KV_W, seq_len), F32)] * 6
        out_specs += [pl.BlockSpec((1, KV_W, tm), lambda i: (i // nt, 0, i % nt))] * 6
    args = [x2, p['mix_norm'].reshape(1, d), w,
            jnp.tile(p['q_norm'], N_HEADS).reshape(1, D_ATT),
            jnp.tile(p['k_sel_norm'], N_KV_HEADS).reshape(1, KV_W),
            jnp.tile(p['k_win_norm'], N_KV_HEADS).reshape(1, KV_W),
            _blockdiag_ones(D_ATT), _blockdiag_ones(KV_W), *tabs]
    return pl.pallas_call(
        functools.partial(_proj_body, transposed=seq_len is not None), out_shape=out_shape,
        grid=(rows // tm,), in_specs=in_specs, out_specs=out_specs,
        compiler_params=_params(1), name="proj")(*args)


def _conv_post(acc, lg, lb, og):
    mu = jnp.mean(acc, axis=-1, keepdims=True)
    xc = acc - mu
    var = jnp.mean(xc * xc, axis=-1, keepdims=True)
    y = xc * lax.rsqrt(var + EPS) * lg + lb
    y = y * _sigmoid(y)
    return _rms(y, og)


def _conv_body(u_ref, cw_ref, cb_ref, lg_ref, lb_ref, og_ref, o_ref, *, n_tiles, tt):
    halo = 32
    lead = halo - (CONV_W - 1)

    def tile(win, t0):
        acc = jnp.zeros((tt, C_CONV), F32) + cb_ref[...]
        for r in range(8):
            sh = win if r == 0 else pltpu.roll(win, tt + halo - r, 0)
            for a in range(halo // 8 + 1):
                w = 8 * a + r - lead
                if 0 <= w < CONV_W:
                    acc = acc + sh[8 * a:8 * a + tt, :] * cw_ref[w:w + 1, :]
        o_ref[0, pl.ds(t0, tt), :] = _conv_post(acc, lg_ref[...], lb_ref[...], og_ref[...])

    tile(jnp.concatenate([jnp.zeros((halo, C_CONV), F32), u_ref[0, 0:tt, :]], axis=0), 0)

    def step(i, carry):
        t0 = pl.multiple_of(i * tt, tt)
        tile(u_ref[0, pl.ds(pl.multiple_of(t0 - halo, halo), tt + halo), :], t0)
        return carry

    lax.fori_loop(1, n_tiles, step, 0)


def _conv_prompt(u, p, tt=256):
    b, t, c = u.shape
    cw = jnp.pad(p['conv_w'], ((0, 1), (0, 0)))
    const = lambda i: (0, 0)
    vec = lambda a: a.reshape(1, c)
    blk = pl.BlockSpec((1, t, c), lambda i: (i, 0, 0))
    return pl.pallas_call(
        functools.partial(_conv_body, n_tiles=t // tt, tt=tt),
        out_shape=jax.ShapeDtypeStruct((b, t, c), F32), grid=(b,),
        in_specs=[blk, pl.BlockSpec((32, c), const)] + [pl.BlockSpec((1, c), const)] * 4,
        out_specs=blk, compiler_params=_params(1), name="conv_prompt")(
            u, cw, vec(p['conv_b']), vec(p['conv_ln_g']), vec(p['conv_ln_b']), vec(p['out_norm_conv']))


def _conv_sample_body(st_ref, u_ref, cw_ref, cb_ref, lg_ref, lb_ref, og_ref, o_ref):
    acc = u_ref[...] * cw_ref[CONV_W - 1:CONV_W, :] + cb_ref[...]
    for w in range(CONV_W - 1):
        acc = acc + st_ref[w] * cw_ref[w:w + 1, :]
    o_ref[...] = _conv_post(acc, lg_ref[...], lb_ref[...], og_ref[...])


def _conv_sample(state, u, p):
    b = u.shape[0]
    c = C_CONV
    cw = jnp.pad(p['conv_w'], ((0, 1), (0, 0)))
    vec = lambda a: a.reshape(1, c)
    return pl.pallas_call(
        _conv_sample_body, out_shape=jax.ShapeDtypeStruct((b, c), F32),
        compiler_params=pltpu.CompilerParams(vmem_limit_bytes=VMEM_LIMIT), name="conv_sample")(
            state.transpose(1, 0, 2), u, cw, vec(p['conv_b']), vec(p['conv_ln_g']),
            vec(p['conv_ln_b']), vec(p['out_norm_conv']))


def _compress_weights(pe, w1, w2):
    assert N_SUB == 2
    w1r = w1.reshape(N_SUB, CMP_STRIDE, HEAD_DIM, CMP_HID).transpose(1, 2, 0, 3)
    eye = jnp.eye(N_KV_HEADS, dtype=w1.dtype)
    wf = jnp.einsum('sdmh,gk->sgdmkh', w1r, eye).reshape(CHUNK_W, N_SUB * N_KV_HEADS * CMP_HID)
    per = jnp.broadcast_to(pe.reshape(N_SUB, CMP_STRIDE, 1, HEAD_DIM),
                           (N_SUB, CMP_STRIDE, N_KV_HEADS, HEAD_DIM)).reshape(N_SUB, CHUNK_W)
    w2f = jnp.einsum('hd,gk->ghkd', w2, eye).reshape(N_KV_HEADS * CMP_HID, KV_W)
    return per, wf.astype(BF16), w2f.astype(BF16)


def _position_pair_weights(per, wf):
    half = N_KV_HEADS * CMP_HID
    pairs = CMP_STRIDE // 2
    pe_p = per.reshape(N_SUB * pairs, 2 * KV_W)
    w_p = wf.reshape(pairs, 2 * KV_W, N_SUB, half).transpose(2, 0, 1, 3).reshape(N_SUB * pairs, 2 * KV_W, half)
    return pe_p, w_p


def _compress_tail(a0, a1, w2_ref):
    n = a0.shape[0]
    h = a0 + pltpu.roll(a1, n - 1, 0)
    t = h * _sigmoid(h)
    tok = _dot(t.astype(BF16), w2_ref[...])
    return jnp.where(lax.broadcasted_iota(jnp.int32, tok.shape, 0) < n - 1, tok, 0.0)


def _cmp_key_post(tok, kng, bd1, c, sa, sb):
    return _rope(_group_rms(tok, kng[...], bd1[...]), c[...], sa[...], sb[...])


def _first_layer(rows_at, n_chunk, pe_ref, w1_ref):
    half = N_KV_HEADS * CMP_HID
    pairs = CMP_STRIDE // 2
    a0 = jnp.zeros((n_chunk, half), F32)
    a1 = jnp.zeros((n_chunk, half), F32)
    for sp in range(pairs):
        xs = jnp.concatenate([rows_at(2 * sp), rows_at(2 * sp + 1)], axis=1)
        a0 = a0 + _dot((xs + pe_ref[sp:sp + 1, :]).astype(BF16), w1_ref[sp])
        a1 = a1 + _dot((xs + pe_ref[pairs + sp:pairs + sp + 1, :]).astype(BF16), w1_ref[pairs + sp])
    return a0, a1


def _cmp_prompt_body(xk_ref, xv_ref, pek, w1k, w2k, kng, bd1, c, sa, sb, pev, w1v, w2v, ok_ref, ov_ref):
    n_chunk = ok_ref.shape[1]
    rows_of = lambda ref: lambda s: ref[0, pl.ds(s, n_chunk, stride=CMP_STRIDE), :]
    tok_k = _compress_tail(*_first_layer(rows_of(xk_ref), n_chunk, pek, w1k), w2k)
    tok_v = _compress_tail(*_first_layer(rows_of(xv_ref), n_chunk, pev, w1v), w2v)
    ok_ref[0] = _cmp_key_post(tok_k, kng, bd1, c, sa, sb)
    ov_ref[0] = tok_v.T


def _cmp_weights(p, name):
    per, wf, w2f = _compress_weights(p[f'cmp_{name}_pos'], p[f'cmp_{name}_w1'], p[f'cmp_{name}_w2'])
    return [*_position_pair_weights(per, wf), w2f]


def _cmp_prompt(kc, vc, p):
    b, t, _ = kc.shape
    n_chunk = t // CMP_STRIDE
    args = [*_cmp_weights(p, 'k'), *_cmp_key_consts(p, n_chunk), *_cmp_weights(p, 'v')]
    blk = pl.BlockSpec((1, t, KV_W), lambda i: (i, 0, 0))
    b3 = lambda i: (i, 0, 0)
    const = lambda a: pl.BlockSpec(a.shape, lambda i: (0,) * a.ndim)
    return pl.pallas_call(
        _cmp_prompt_body,
        out_shape=[jax.ShapeDtypeStruct((b, n_chunk, KV_W), F32), jax.ShapeDtypeStruct((b, KV_W, n_chunk), F32)],
        grid=(b,), in_specs=[blk, blk] + [const(a) for a in args],
        out_specs=[pl.BlockSpec((1, n_chunk, KV_W), b3), pl.BlockSpec((1, KV_W, n_chunk), b3)],
        compiler_params=_params(1), name="cmp_prompt")(kc, vc, *args)


def _cmp_key_consts(p, n_chunk):
    cmp_end = np.arange(n_chunk) * CMP_STRIDE + CMP_LEN - 1
    kng = jnp.tile(p['k_cmp_norm'], N_KV_HEADS).reshape(1, KV_W)
    return [kng, _blockdiag_ones(KV_W), *_rope_tables(cmp_end)]


def _cmp_sample_body(pt_ref, cache_hbm, pe_ref, w1_ref, w2_ref, *rest, n_pages, n_batch, is_key):
    key_consts, (o_ref, buf, tok, a_sc, sem) = rest[:-5], rest[-5:]
    b = pl.program_id(0)
    slot = b % 2
    page_len = cache_hbm.shape[2]
    n_pos = n_pages * page_len
    unroll = 8 if n_pages % 8 == 0 else 1

    def page_copy(page, i, sl):
        dst = pl.ds(pl.multiple_of(i * page_len, page_len), page_len)
        return pltpu.make_async_copy(cache_hbm.at[page], buf.at[sl, :, dst], sem.at[sl])

    def gather(seq, sl):
        def issue(i, carry):
            page_copy(pt_ref[seq * n_pages + i], i, sl).start()
            return carry
        lax.fori_loop(0, n_pages, issue, 0, unroll=unroll)

    @pl.when(b == 0)
    def _():
        gather(0, 0)

    @pl.when(b + 1 < n_batch)
    def _():
        gather(b + 1, 1 - slot)

    def wait(i, carry):
        page_copy(0, i, slot).wait()
        return carry
    lax.fori_loop(0, n_pages, wait, 0, unroll=unroll)

    n_blk = 4 if n_pos % (4 * 8 * page_len) == 0 else 1
    pos_blk = n_pos // n_blk
    chunk_blk = pos_blk // CMP_STRIDE
    xw = min(pos_blk, 8 * page_len)
    half_rows = CMP_STRIDE // 2
    for j in range(n_blk):
        p0 = j * pos_blk
        for i in range(pos_blk // xw):
            o = p0 + i * xw
            rows = buf[slot, :, o:o + xw].T.reshape(xw // CMP_STRIDE, CMP_STRIDE, KV_W)
            for h in range(2):
                tok[h, o // 2:(o + xw) // 2, :] = rows[:, h * half_rows:(h + 1) * half_rows, :].reshape(
                    xw // 2, KV_W)
        rows_at = lambda s: tok[s // half_rows, pl.ds(p0 // 2 + s % half_rows, chunk_blk, stride=half_rows), :]
        a0, a1 = _first_layer(rows_at, chunk_blk, pe_ref, w1_ref)
        a_sc[0, j * chunk_blk:(j + 1) * chunk_blk, :] = a0
        a_sc[1, j * chunk_blk:(j + 1) * chunk_blk, :] = a1
    out = _compress_tail(a_sc[0], a_sc[1], w2_ref)
    o_ref[0] = _cmp_key_post(out, *key_consts) if is_key else out


def _cmp_sample(page_table, cache_t, p, is_key):
    b, n_pages = page_table.shape
    page_len = cache_t.shape[2]
    n_pos = n_pages * page_len
    n_chunk = n_pos // CMP_STRIDE
    name = 'k' if is_key else 'v'
    args = _cmp_weights(p, name) + (_cmp_key_consts(p, n_chunk) if is_key else [])
    const = lambda a: pl.BlockSpec(a.shape, lambda i, pt: (0,) * a.ndim)
    grid_spec = pltpu.PrefetchScalarGridSpec(
        num_scalar_prefetch=1, grid=(b,),
        in_specs=[pl.BlockSpec(memory_space=pl.ANY)] + [const(a) for a in args],
        out_specs=pl.BlockSpec((1, n_chunk, KV_W), lambda i, pt: (i, 0, 0)),
        scratch_shapes=[pltpu.VMEM((2, KV_W, n_pos), F32), pltpu.VMEM((2, n_pos // 2, KV_W), F32),
                        pltpu.VMEM((N_SUB, n_chunk, N_KV_HEADS * CMP_HID), F32),
                        pltpu.SemaphoreType.DMA((2,))])
    return pl.pallas_call(
        functools.partial(_cmp_sample_body, n_pages=n_pages, n_batch=b, is_key=is_key),
        out_shape=jax.ShapeDtypeStruct((b, n_chunk, KV_W), F32), grid_spec=grid_spec,
        compiler_params=_params(1), name=f"cmp_sample_{name}")(page_table.reshape(-1), cache_t, *args)


def _cmp_to_sel(n_chunk, n_cmp, n_sel, width):
    cs = np.arange(n_chunk)[:, None] * CMP_STRIDE
    ss = np.arange(width)[None, :] * SEL_BLK
    ov = np.clip(np.minimum(cs + CMP_LEN, ss + SEL_BLK) - np.maximum(cs, ss), 0, None) / CMP_LEN
    ov = ov * (np.arange(n_chunk)[:, None] < n_cmp) * (np.arange(width)[None, :] < n_sel)
    return jnp.asarray(ov, dtype=BF16)


def _attn_qlanes_body(q_ref, gt_ref, kc_ref, vct_ref, ks_ref, vst_ref, kw_ref, vwt_ref, ovt_ref, eblk_ref,
                      o_ref, qa_sc, oct_sc, m_sc, l_sc, acc_sc, s_sc,
                      *, t, tq, kc_len, n_cmp, n_sel):
    off = pl.program_id(1) * tq
    cols = GRP * tq
    lane = lax.broadcasted_iota(jnp.int32, (tq, LANES), 1)
    qpos = off + lax.broadcasted_iota(jnp.int32, (1, tq), 1)
    q = q_ref[0] * (SCALE * LOG2E)
    for h in range(N_HEADS):
        g, r = divmod(h, GRP)
        qa_sc[g, :KV_W, r * tq:(r + 1) * tq] = _head_to_group_lanes(q, h, lane).T.astype(BF16)

    def attend(jobs):
        def get(ref, st, col):
            parts = [ref[st, :, c] for c in col]
            return parts[0] if len(parts) == 1 else jnp.concatenate(parts, axis=1)

        def put(ref, st, col, val):
            o = 0
            for c in col:
                ref[st, :, c] = val[:, o:o + c.stop - c.start]
                o += c.stop - c.start

        stats = []
        for st, k, vt, rhs, bias, col in jobs:
            n, w = k.shape[0], rhs.shape[1]
            s = _dot(k, rhs)
            if bias is not None:
                s = s + bias()
            s_sc[st, :n, :w] = s
            m_old = get(m_sc, st, col)
            m_new = jnp.maximum(m_old, jnp.max(s, axis=0, keepdims=True))
            put(m_sc, st, col, m_new)
            stats.append((m_new, jnp.exp2(m_old - m_new)))
        probs = []
        for (st, k, vt, rhs, bias, col), (m_new, alpha) in zip(jobs, stats):
            p = jnp.exp2(s_sc[st, :k.shape[0], :rhs.shape[1]] - m_new)
            put(l_sc, st, col, alpha * get(l_sc, st, col) + jnp.sum(p, axis=0, keepdims=True))
            probs.append(p.astype(BF16))
        for (st, k, vt, rhs, bias, col), (m_new, alpha), p in zip(jobs, stats, probs):
            put(acc_sc, st, col, alpha * get(acc_sc, st, col) + _dot(vt, p))

    sel_state = lambda g: g
    win_state = lambda g: N_KV_HEADS + g
    m_sc[...] = jnp.full(m_sc.shape, NEG_INF, F32)
    l_sc[...] = jnp.zeros(l_sc.shape, F32)
    acc_sc[...] = jnp.zeros(acc_sc.shape, F32)

    n_chunk = kc_ref.shape[1]
    kcm = kc_ref[0].astype(BF16)
    vct = vct_ref[0].astype(BF16)
    nrow = lax.broadcasted_iota(jnp.int32, (n_chunk, tq), 0)
    mask_1 = jnp.logical_and(nrow * CMP_STRIDE + (CMP_LEN - 1) <= qpos, nrow < n_cmp)
    mask_c = jnp.concatenate([mask_1.astype(F32)] * GRP, axis=1) > 0.5
    psum = []
    for g in range(N_KV_HEADS):
        sc = jnp.where(mask_c, _dot(kcm, qa_sc[g, :KV_W, :]), NEG_INF)
        e = jnp.where(mask_c, jnp.exp2(sc - jnp.max(sc, axis=0, keepdims=True)), 0.0)
        pc = e / jnp.maximum(jnp.sum(e, axis=0, keepdims=True), 1e-30)
        oct_sc[g] = _dot(vct, pc.astype(BF16))
        psum.append(sum(pc[:, r * tq:(r + 1) * tq] for r in range(GRP)))

    n_sel_pad = -(-n_sel // 8) * 8
    jrow = lax.broadcasted_iota(jnp.int32, (n_sel_pad, tq), 0)
    cur = qpos // SEL_BLK
    forced = jnp.logical_or(jrow == 0, jnp.logical_or(jrow == cur, jrow == cur - 1))
    bonus = jnp.where(forced, FORCE_BONUS, 0.0)
    k_eff = min(TOPK, n_sel)
    ovt = ovt_ref[...]
    for g in range(N_KV_HEADS):
        hi = psum[g].astype(BF16)
        lo = (psum[g] - hi.astype(F32)).astype(BF16)
        imp = _dot(ovt, hi) + _dot(ovt, lo)
        score = jnp.where(jrow <= cur, imp + bonus, NEG_INF)
        rank = jnp.zeros((n_sel_pad, tq), F32)
        for j in range(n_sel):
            row = score[j:j + 1, :]
            tie = jnp.where(jrow > j, 1.0, 0.0)
            rank = rank + jnp.where(row > score, 1.0, jnp.where(row == score, tie, 0.0))
        chosen = jnp.logical_and(rank < k_eff, jrow <= cur)
        for part, keep in enumerate((jnp.logical_and(chosen, jrow < off // SEL_BLK), chosen)):
            r0 = KV_W + part * LANES
            pen = jnp.where(keep, 0.0, -MASK_PENALTY).astype(BF16)
            qa_sc[g, r0:r0 + n_sel_pad, :] = jnp.concatenate([pen] * GRP, axis=1)
            qa_sc[g, r0 + n_sel_pad:r0 + LANES, :] = jnp.zeros((LANES - n_sel_pad, cols), BF16)

    tq_off = pl.multiple_of(off, tq)
    half = tq // 2
    all_cols = [slice(0, cols)]
    lo_cols = [slice(r * tq, r * tq + half) for r in range(GRP)]
    hi_cols = [slice(r * tq + half, (r + 1) * tq) for r in range(GRP)]

    def triangle(keep_front, key0, col):
        width = tq if col is all_cols else half
        q0 = half if col is hi_cols else 0
        n_cols = width * GRP
        kk = key0 + lax.broadcasted_iota(jnp.int32, (half, n_cols), 0)
        qq = q0 + (lax.broadcasted_iota(jnp.int32, (half, n_cols), 1) & (width - 1))
        return jnp.where((kk > qq) if keep_front else (kk <= qq), 0.0, -jnp.inf)

    def take_cols(x, col):
        return x if col is all_cols else jnp.concatenate([x[:, c] for c in col], axis=1)

    def half_jobs(st, k, vt, rhs, keep_front):
        cols_a, cols_b = (lo_cols, all_cols) if keep_front else (all_cols, hi_cols)
        job = lambda key0, col: (st, k[key0:key0 + half], vt[:, key0:key0 + half], take_cols(rhs, col),
                                 functools.partial(triangle, keep_front, key0, col), col)
        return job(0, cols_a), job(half, cols_b)

    k_own = jnp.concatenate(
        [ks_ref[0, pl.ds(tq_off, tq), :].astype(BF16), eblk_ref[pl.ds(tq_off, tq), :]], axis=1)
    vt_own = vst_ref[0, :, pl.ds(tq_off, tq)].astype(BF16)
    kw_own = kw_ref[0, pl.ds(tq_off, tq), :].astype(BF16)
    vwt_own = vwt_ref[0, :, pl.ds(tq_off, tq)].astype(BF16)
    groups = range(N_KV_HEADS)
    q_own = lambda g: jnp.concatenate([qa_sc[g, :KV_W, :], qa_sc[g, KV_W + LANES:, :]], axis=0)
    own = ([half_jobs(sel_state(g), k_own, vt_own, q_own(g), False) for g in groups]
           + [half_jobs(win_state(g), kw_own, vwt_own, qa_sc[g, :KV_W, :], False) for g in groups])
    attend([first for first, _ in own])
    attend([second for _, second in own])

    for ci in range(t // kc_len):
        @pl.when(ci * kc_len < off)
        def _(ci=ci):
            base = ci * kc_len
            k_aug = jnp.concatenate(
                [ks_ref[0, base:base + kc_len, :].astype(BF16), eblk_ref[base:base + kc_len, :]], axis=1)
            vt = vst_ref[0, :, base:base + kc_len].astype(BF16)
            attend([(sel_state(g), k_aug, vt, qa_sc[g, :KV_W + LANES, :], None, all_cols) for g in groups])

    def window_keys(start, n, front):
        start = pl.multiple_of(start, tq)
        kw = kw_ref[0, pl.ds(start, n), :].astype(BF16)
        vwt = vwt_ref[0, :, pl.ds(start, n)].astype(BF16)
        if front:
            halves = [half_jobs(win_state(g), kw, vwt, qa_sc[g, :KV_W, :], True) for g in groups]
            attend([first for first, _ in halves])
            attend([second for _, second in halves])
        else:
            attend([(win_state(g), kw, vwt, qa_sc[g, :KV_W, :], None, all_cols) for g in groups])

    @pl.when(off >= WINDOW)
    def _():
        window_keys(off - WINDOW, tq, True)
        if WINDOW > tq:
            window_keys(off - WINDOW + tq, WINDOW - tq, False)

    for i in range(1, WINDOW // tq):
        @pl.when(jnp.logical_and(off < WINDOW, off - i * tq >= 0))
        def _(i=i):
            window_keys(off - i * tq, tq, False)

    gtt = gt_ref[0].T
    outs = []
    for g in range(N_KV_HEADS):
        o_w = acc_sc[win_state(g)] / jnp.maximum(l_sc[win_state(g)], 1e-30)
        o_s = acc_sc[sel_state(g)] / jnp.maximum(l_sc[sel_state(g)], 1e-30)
        o_c = oct_sc[g]
        for r in range(GRP):
            h = g * GRP + r
            cs = slice(r * tq, (r + 1) * tq)
            o_h = (gtt[3 * h:3 * h + 1, :] * o_c[:, cs] + gtt[3 * h + 1:3 * h + 2, :] * o_s[:, cs]
                   + gtt[3 * h + 2:3 * h + 3, :] * o_w[:, cs])
            outs.append(o_h.T)
    o_ref[0] = jnp.concatenate(
        [_pair_tile(outs[2 * p], outs[2 * p + 1], 2 * p, lane) for p in range(N_HEADS // 2)], axis=-1)


def _attn_qlanes(q, gt, kcmp, vcmp_t, ks, vs_t, kw, vw_t, tq=512, kc_len=512):
    b, t, _ = q.shape
    n_chunk = kcmp.shape[1]
    n_cmp = n_chunk - N_SUB + 1
    n_sel = -(-t // SEL_BLK)
    assert t % kc_len == 0 and t >= WINDOW + tq and n_sel <= LANES and n_chunk % LANES == 0
    assert tq % LANES == 0 and WINDOW % tq == 0 and tq % SEL_BLK == 0
    n_sel_pad = -(-n_sel // 8) * 8
    ov_t = _cmp_to_sel(n_chunk, n_cmp, n_sel, n_sel_pad).T
    assert tq & (tq - 1) == 0
    eblk = jnp.asarray(np.arange(t)[:, None] // SEL_BLK == np.arange(LANES)[None, :], dtype=BF16)
    qblk = lambda i, j: (i, j, 0)
    full = lambda i, j: (i, 0, 0)
    const = lambda a: pl.BlockSpec(a.shape, lambda i, j: (0,) * a.ndim)
    tok = pl.BlockSpec((1, t, KV_W), full)
    tr = pl.BlockSpec((1, KV_W, t), full)
    cols = GRP * tq
    n_states = 2 * N_KV_HEADS
    assert kc_len >= WINDOW - tq
    body = functools.partial(_attn_qlanes_body, t=t, tq=tq, kc_len=kc_len, n_cmp=n_cmp, n_sel=n_sel)
    return pl.pallas_call(
        body, out_shape=jax.ShapeDtypeStruct((b, t, D_ATT), F32), grid=(b, t // tq),
        in_specs=[pl.BlockSpec((1, tq, D_ATT), qblk), pl.BlockSpec((1, tq, LANES), qblk),
                  pl.BlockSpec((1, n_chunk, KV_W), full), pl.BlockSpec((1, KV_W, n_chunk), full),
                  tok, tr, tok, tr, const(ov_t), const(eblk)],
        out_specs=pl.BlockSpec((1, tq, D_ATT), qblk),
        scratch_shapes=[pltpu.VMEM((N_KV_HEADS, KV_W + 2 * LANES, cols), BF16),
                        pltpu.VMEM((N_KV_HEADS, KV_W, cols), F32), pltpu.VMEM((n_states, 1, cols), F32),
                        pltpu.VMEM((n_states, 1, cols), F32), pltpu.VMEM((n_states, KV_W, cols), F32),
                        pltpu.VMEM((n_states, kc_len, cols), F32)],
        compiler_params=_params(2), name="attn_prompt")(
            q, gt, kcmp, vcmp_t, ks, vs_t, kw, vw_t, ov_t, eblk)


def _sample_heads(q_row):
    lane = lax.broadcasted_iota(jnp.int32, (1, LANES), 1)
    return jnp.concatenate([_head_to_group_lanes(q_row, h, lane) for h in range(N_HEADS)], axis=0)


def _attn_s1_body(q_ref, kc_ref, vc_ref, ov_ref, oc_ref, idx_ref, *, n_cmp, n_sel, q_pos):
    qh = _sample_heads(q_ref[0])
    n_chunk = kc_ref.shape[1]
    ncol = lax.broadcasted_iota(jnp.int32, (N_HEADS, n_chunk), 1)
    mask = jnp.logical_and(ncol * CMP_STRIDE + (CMP_LEN - 1) <= q_pos, ncol < n_cmp)
    pc = _masked_softmax(_dot_t(qh.astype(BF16), kc_ref[0].astype(BF16)) * SCALE, mask)
    oc_ref[0] = _dot(pc.astype(BF16), vc_ref[0].astype(BF16))
    rows = [jnp.sum(pc[g * GRP:(g + 1) * GRP, :], axis=0, keepdims=True) for g in range(N_KV_HEADS)]
    psum = jnp.concatenate(rows + [jnp.zeros((N_HEADS - N_KV_HEADS, n_chunk), F32)], axis=0)
    imp = _split_dot(psum, ov_ref[...])
    width = imp.shape[1]
    jl = lax.broadcasted_iota(jnp.int32, (N_HEADS, width), 1)
    cur = q_pos // SEL_BLK
    forced = jnp.logical_or(jl == 0, jnp.logical_or(jl == cur, jl == cur - 1))
    score = jnp.where(jl <= cur, imp + jnp.where(forced, FORCE_BONUS, 0.0), NEG_INF)
    n_rows = -(-n_sel // 8) * 8
    ii = lax.broadcasted_iota(jnp.int32, (n_rows, width), 0)
    jj = lax.broadcasted_iota(jnp.int32, (n_rows, width), 1)
    slot = lax.broadcasted_iota(jnp.int32, (n_rows, LANES), 1).astype(F32)
    blk = lax.broadcasted_iota(jnp.int32, (n_rows, LANES), 0).astype(F32)
    tie = jnp.where(jj < ii, 1.0, 0.0)
    out_rows = []
    for g in range(N_KV_HEADS):
        row = jnp.broadcast_to(score[g:g + 1, :], (n_rows, width))
        col = jnp.sum(jnp.where(ii == jj, row, 0.0), axis=1, keepdims=True)
        beats = jnp.where(row > col, 1.0, jnp.where(row == col, tie, 0.0))
        rank = jnp.sum(beats, axis=1, keepdims=True)
        out_rows.append(jnp.sum(jnp.where(rank == slot, blk, 0.0), axis=0, keepdims=True))
    out_rows.append(jnp.zeros((N_HEADS - N_KV_HEADS, LANES), F32))
    idx_ref[0] = jnp.concatenate(out_rows, axis=0).astype(jnp.int32)


def _attn_s1(q, kcmp, vcmp, q_pos, n_sel):
    b = q.shape[0]
    n_chunk = kcmp.shape[1]
    n_cmp = n_chunk - N_SUB + 1
    width = -(-n_sel // LANES) * LANES
    ov = _cmp_to_sel(n_chunk, n_cmp, n_sel, width)
    blk3 = lambda i: (i, 0, 0)
    cm = pl.BlockSpec((1, n_chunk, KV_W), blk3)
    out = pl.BlockSpec((1, N_HEADS, LANES), blk3)
    return pl.pallas_call(
        functools.partial(_attn_s1_body, n_cmp=n_cmp, n_sel=n_sel, q_pos=q_pos),
        out_shape=[jax.ShapeDtypeStruct((b, N_HEADS, LANES), F32),
                   jax.ShapeDtypeStruct((b, N_HEADS, LANES), jnp.int32)],
        grid=(b,),
        in_specs=[pl.BlockSpec((1, 1, D_ATT), blk3), cm, cm, pl.BlockSpec(ov.shape, lambda i: (0, 0))],
        out_specs=[out, out], compiler_params=_params(1), name="attn_sample_select")(
            q.reshape(b, 1, D_ATT), kcmp, vcmp, ov)


def _extra_key_attention(qh, k_t, v_t, mask, k_new, v_new):
    qf = qh.astype(F32)
    s = jnp.where(mask, _dot(qh, k_t.astype(BF16)) * SCALE, NEG_INF)
    s_new = jnp.sum(qf * k_new, axis=-1, keepdims=True) * SCALE
    m = jnp.maximum(jnp.max(s, axis=-1, keepdims=True), s_new)
    e = jnp.where(mask, jnp.exp(s - m), 0.0)
    e_new = jnp.exp(s_new - m)
    denom = jnp.maximum(jnp.sum(e, axis=-1, keepdims=True) + e_new, 1e-30)
    return (_dot_t(e.astype(BF16), v_t.astype(BF16)) + e_new * v_new) / denom


def _attn_s2_body(idx_ref, pt_ref, q_ref, gt_ref, oc_ref, ksn_ref, vsn_ref, kwn_ref, vwn_ref,
                  kws_ref, vws_ref, kcache, vcache, o_ref, kbuf, vbuf, sem,
                  *, n_pages, n_batch, n_cache_blk, q_pos):
    b = pl.program_id(0)
    slot = b % 2
    page_len = kcache.shape[2]
    per_page = page_len // SEL_BLK

    def block_id(seq, g, k):
        return idx_ref[(seq * N_KV_HEADS + g) * TOPK + k]

    def copies(seq, sl, g, k):
        blk = jnp.minimum(block_id(seq, g, k), n_cache_blk - 1)
        page = pt_ref[seq * n_pages + blk // per_page]
        dst = pl.ds(k * page_len, page_len)
        return (pltpu.make_async_copy(kcache.at[page], kbuf.at[sl, g, :, dst], sem.at[sl, 0]),
                pltpu.make_async_copy(vcache.at[page], vbuf.at[sl, g, :, dst], sem.at[sl, 1]))

    def for_all_copies(seq, sl, fn):
        for g in range(N_KV_HEADS):
            for k in range(TOPK):
                for cp in copies(seq, sl, g, k):
                    fn(cp)

    @pl.when(b == 0)
    def _():
        for_all_copies(0, 0, lambda cp: cp.start())

    @pl.when(b + 1 < n_batch)
    def _():
        for_all_copies(b + 1, 1 - slot, lambda cp: cp.start())

    for_all_copies(b, slot, lambda cp: cp.wait())
    blks = [[block_id(b, g, k) for k in range(TOPK)] for g in range(N_KV_HEADS)]

    qh = _sample_heads(q_ref[0]).astype(BF16)
    n_keys = TOPK * page_len
    row_grp = lax.broadcasted_iota(jnp.int32, (N_HEADS, n_keys), 0) // GRP
    key = lax.broadcasted_iota(jnp.int32, (N_HEADS, n_keys), 1)
    key_slot = key // page_len
    blk_in_page = (key % page_len) // SEL_BLK
    keep = jnp.zeros((N_HEADS, n_keys), F32)
    for g in range(N_KV_HEADS):
        for k in range(TOPK):
            blk = blks[g][k]
            in_cache = jnp.where(blk < n_cache_blk, 1.0, 0.0)
            hit = jnp.where(blk_in_page == blk % per_page, in_cache, 0.0)
            keep = jnp.where(jnp.logical_and(row_grp == g, key_slot == k), hit, keep)
    mask_s = keep > 0.5
    o_sel = [_extra_key_attention(qh, kbuf.at[slot, g][...], vbuf.at[slot, g][...], mask_s,
                                  ksn_ref[0], vsn_ref[0])
             for g in range(N_KV_HEADS)]
    out_grp = lax.broadcasted_iota(jnp.int32, (N_HEADS, LANES), 0) // GRP
    o_s = jnp.where(out_grp == 0, o_sel[0], o_sel[1])

    n_win = kws_ref.shape[2]
    kpos = (q_pos - n_win) + lax.broadcasted_iota(jnp.int32, (N_HEADS, n_win), 1)
    dist = q_pos - kpos
    mask_w = jnp.logical_and(dist >= 0, dist < WINDOW)
    o_w = _extra_key_attention(qh, kws_ref[0], vws_ref[0], mask_w, kwn_ref[0], vwn_ref[0])

    lane8 = lax.broadcasted_iota(jnp.int32, (N_HEADS, LANES), 1)
    head8 = lax.broadcasted_iota(jnp.int32, (N_HEADS, LANES), 0)
    gt = jnp.broadcast_to(gt_ref[0], (N_HEADS, LANES))
    gate = [jnp.sum(jnp.where(lane8 == 3 * head8 + j, gt, 0.0), axis=-1, keepdims=True) for j in range(3)]
    o = gate[0] * oc_ref[0] + gate[1] * o_s + gate[2] * o_w
    lane = lax.broadcasted_iota(jnp.int32, (1, LANES), 1)
    o_ref[0] = jnp.concatenate(
        [_pair_tile(o[2 * p:2 * p + 1, :], o[2 * p + 1:2 * p + 2, :], 2 * p, lane)
         for p in range(N_HEADS // 2)], axis=-1)


def _attn_s2(idx, page_table, q, gt, oc, ks_new, vs_new, kw_new, vw_new, kw_state_t, vw_state_t,
             cache_k_t, cache_v_t, q_pos):
    b, n_pages = page_table.shape
    page_len = cache_k_t.shape[2]
    n_win = kw_state_t.shape[2]
    assert n_win <= WINDOW
    blk3 = lambda i, *_: (i, 0, 0)
    row = lambda w: pl.BlockSpec((1, 1, w), blk3)
    grid_spec = pltpu.PrefetchScalarGridSpec(
        num_scalar_prefetch=2, grid=(b,),
        in_specs=[row(D_ATT), row(LANES), pl.BlockSpec((1, N_HEADS, LANES), blk3),
                  row(KV_W), row(KV_W), row(KV_W), row(KV_W),
                  pl.BlockSpec((1, KV_W, n_win), blk3), pl.BlockSpec((1, KV_W, n_win), blk3),
                  pl.BlockSpec(memory_space=pl.ANY), pl.BlockSpec(memory_space=pl.ANY)],
        out_specs=row(D_ATT),
        scratch_shapes=[pltpu.VMEM((2, N_KV_HEADS, KV_W, TOPK * page_len), F32),
                        pltpu.VMEM((2, N_KV_HEADS, KV_W, TOPK * page_len), F32),
                        pltpu.SemaphoreType.DMA((2, 2))])
    r3 = lambda a: a.reshape(b, 1, a.shape[-1])
    body = functools.partial(_attn_s2_body, n_pages=n_pages, n_batch=b,
                             n_cache_blk=n_pages * (page_len // SEL_BLK),
                             q_pos=q_pos)
    return pl.pallas_call(
        body, out_shape=jax.ShapeDtypeStruct((b, 1, D_ATT), F32), grid_spec=grid_spec,
        compiler_params=_params(1), name="attn_sample")(
            idx.reshape(-1), page_table.reshape(-1), r3(q), r3(gt), oc, r3(ks_new), r3(vs_new),
            r3(kw_new), r3(vw_new), kw_state_t, vw_state_t, cache_k_t, cache_v_t).reshape(b, D_ATT)


def _prompt_layer(x, p, tm=512, tm_proj=1024):
    b, t, d = x.shape
    x2 = _ffn(x.reshape(b * t, d), p['ffn1_norm'], p['ffn1_w_in'], p['ffn1_w_out'], None, tm)
    u, q, gt, kc, vc, ks, kw, kc_t, vc_t, ks_t, vs_t, kw_t, vw_t = _proj(
        x2, p, _rope_tables(np.arange(t)), tm_proj, seq_len=t)
    r3 = lambda a: a.reshape(b, t, a.shape[-1])
    u3 = r3(u)
    cn = _conv_prompt(u3, p)
    kcmp, vcmp_t = _cmp_prompt(r3(kc), r3(vc), p)
    o = _attn_qlanes(r3(q), r3(gt), kcmp, vcmp_t, r3(ks), vs_t, r3(kw), vw_t)
    y = _ffn(x2, p['ffn2_norm'], p['ffn2_w_in'], p['ffn2_w_out'], p['final_norm'], tm,
             mix=(cn.reshape(b * t, C_CONV), o.reshape(b * t, D_ATT), p['out_norm_attn'], p['w_out']))
    r4 = lambda a: a.reshape(b, N_KV_HEADS, HEAD_DIM, a.shape[-1]).transpose(0, 3, 1, 2)
    keep = min(WINDOW, t)
    state = (r4(kc_t), r4(vc_t), r4(ks_t), r4(vs_t), r4(kw_t[:, :, t - keep:]), r4(vw_t[:, :, t - keep:]),
             u3[:, t - (CONV_W - 1):])
    return y.reshape(b, t, d), state


def _sample_layer(x, p, cache_k_cmp, cache_v_cmp, cache_k_sel, cache_v_sel, kw_state, vw_state,
                  conv_state, page_table):
    b, t, d = x.shape
    assert t == 1
    past_len = page_table.shape[1] * cache_k_cmp.shape[1]
    x2 = _ffn(x.reshape(b, d), p['ffn1_norm'], p['ffn1_w_in'], p['ffn1_w_out'], None, b)
    tabs = _rope_tables(np.full((b,), past_len))
    u, q, gt, kc, vc, ks, vs, kw, vw = _proj(x2, p, tabs, b)
    cn = _conv_sample(conv_state, u, p)
    keys_on_lanes = lambda a: a.transpose(0, 2, 3, 1).reshape(a.shape[0], KV_W, a.shape[1])
    kcmp = _cmp_sample(page_table, keys_on_lanes(cache_k_cmp), p, True)
    vcmp = _cmp_sample(page_table, keys_on_lanes(cache_v_cmp), p, False)
    n_sel = -(-(past_len + 1) // SEL_BLK)
    oc, idx = _attn_s1(q, kcmp, vcmp, past_len, n_sel)
    n_win = kw_state.shape[1]
    o = _attn_s2(idx[:, :N_KV_HEADS, :TOPK], page_table, q, gt, oc, ks, vs, kw, vw,
                 keys_on_lanes(kw_state), keys_on_lanes(vw_state),
                 keys_on_lanes(cache_k_sel), keys_on_lanes(cache_v_sel), past_len)
    y = _ffn(x2, p['ffn2_norm'], p['ffn2_w_in'], p['ffn2_w_out'], p['final_norm'], b,
             mix=(cn, o, p['out_norm_attn'], p['w_out']))
    r4 = lambda a: a.reshape(b, 1, N_KV_HEADS, HEAD_DIM)
    keep = min(WINDOW, n_win + 1)
    new_kw = jnp.concatenate([kw_state, r4(kw)], axis=1)[:, n_win + 1 - keep:]
    new_vw = jnp.concatenate([vw_state, r4(vw)], axis=1)[:, n_win + 1 - keep:]
    new_conv = jnp.concatenate([conv_state, u[:, None, :]], axis=1)[:, 1:]
    return y.reshape(b, 1, d), (r4(kc), r4(vc), r4(ks), r4(vs), new_kw, new_vw, new_conv)


_PARAM_NAMES = ('ffn1_norm', 'ffn1_w_in', 'ffn1_w_out', 'mix_norm', 'w_in', 'conv_w', 'conv_b', 'conv_ln_g',
                'conv_ln_b', 'q_norm', 'k_cmp_norm', 'k_sel_norm', 'k_win_norm', 'cmp_k_pos', 'cmp_k_w1',
                'cmp_k_w2', 'cmp_v_pos', 'cmp_v_w1', 'cmp_v_w2', 'out_norm_conv', 'out_norm_attn', 'w_out',
                'ffn2_norm', 'ffn2_w_in', 'ffn2_w_out', 'final_norm')


def kernel(x_prompt, x_sample, cache_k_cmp, cache_v_cmp, cache_k_sel, cache_v_sel, state_k_win, state_v_win,
           state_conv, page_table, ffn1_norm, ffn1_w_in, ffn1_w_out, mix_norm, w_in, conv_w, conv_b, conv_ln_g,
           conv_ln_b, q_norm, k_cmp_norm, k_sel_norm, k_win_norm, cmp_k_pos, cmp_k_w1, cmp_k_w2, cmp_v_pos,
           cmp_v_w1, cmp_v_w2, out_norm_conv, out_norm_attn, w_out, ffn2_norm, ffn2_w_in, ffn2_w_out, final_norm):
    stacked = dict(zip(_PARAM_NAMES, (
        ffn1_norm, ffn1_w_in, ffn1_w_out, mix_norm, w_in, conv_w, conv_b, conv_ln_g, conv_ln_b, q_norm,
        k_cmp_norm, k_sel_norm, k_win_norm, cmp_k_pos, cmp_k_w1, cmp_k_w2, cmp_v_pos, cmp_v_w1, cmp_v_w2,
        out_norm_conv, out_norm_attn, w_out, ffn2_norm, ffn2_w_in, ffn2_w_out, final_norm)))
    depth = ffn1_norm.shape[0]
    yp, ys = x_prompt, x_sample
    new_p, new_s = [], []
    for l in range(depth):
        p = {k: v[l] for k, v in stacked.items()}
        yp, st_p = _prompt_layer(yp, p)
        new_p.append(st_p)
        ys, st_s = _sample_layer(ys, p, cache_k_cmp[l], cache_v_cmp[l], cache_k_sel[l], cache_v_sel[l],
                                 state_k_win[l], state_v_win[l], state_conv[l], page_table)
        new_s.append(st_s)
    outs_p = [jnp.stack(tup) for tup in zip(*new_p)]
    outs_s = [jnp.stack(tup) for tup in zip(*new_s)]
    return (yp, ys, *outs_p, *outs_s)
```

```python
import functools

import numpy as np
import jax
import jax.numpy as jnp
from jax import lax
from jax.experimental import pallas as pl
from jax.experimental.pallas import tpu as pltpu

F32 = jnp.float32
BF16 = jnp.bfloat16

HEAD_DIM = 64
N_HEADS = 8
N_KV_HEADS = 2
GRP = N_HEADS // N_KV_HEADS
KV_W = N_KV_HEADS * HEAD_DIM
C_CONV = 512
D_ATT = 512
CONV_W = 31
CMP_LEN = 32
CMP_STRIDE = 16
N_SUB = CMP_LEN // CMP_STRIDE
CMP_HID = 2 * HEAD_DIM
SEL_BLK = 64
TOPK = 16
WINDOW = 512
ROT_DIM = HEAD_DIM // 4
ROT_HALF = ROT_DIM // 2
ROPE_THETA = 500000.0
NEG_INF = -1e30
MASK_PENALTY = 2.0 ** 100
FORCE_BONUS = 1e4
EPS = 1e-6
SCALE = HEAD_DIM ** -0.5
LOG2E = 1.4426950408889634
LANES = 128
CHUNK_W = CMP_STRIDE * KV_W
VMEM_LIMIT = 56 * 1024 * 1024


def _params(n_grid_dims):
    return pltpu.CompilerParams(
        dimension_semantics=("arbitrary",) * n_grid_dims, vmem_limit_bytes=VMEM_LIMIT)


def _dot(a, b):
    return jnp.dot(a, b, preferred_element_type=F32)


def _dot_t(a, b):
    return lax.dot_general(a, b, (((1,), (1,)), ((), ())), preferred_element_type=F32)


def _split_dot(x, m_bf16):
    hi = x.astype(BF16)
    lo = (x - hi.astype(F32)).astype(BF16)
    return _dot(hi, m_bf16) + _dot(lo, m_bf16)


def _sigmoid(x):
    return 1.0 / (1.0 + jnp.exp(-x))


def _rms(x, g):
    return x * lax.rsqrt(jnp.mean(x * x, axis=-1, keepdims=True) + EPS) * g


def _group_rms(x, g_tiled, ones_blockdiag):
    ss = _split_dot(x * x, ones_blockdiag)
    return x * lax.rsqrt(ss * (1.0 / HEAD_DIM) + EPS) * g_tiled


def _rope(x, c, sa, sb):
    n = x.shape[-1]
    return x * c + pltpu.roll(x, n - ROT_HALF, 1) * sa + pltpu.roll(x, ROT_HALF, 1) * sb


def _tile_lanes(t, reps):
    return jnp.concatenate([t] * reps, axis=-1) if reps > 1 else t


def _masked_softmax(s, mask):
    s = jnp.where(mask, s, NEG_INF)
    m = jnp.max(s, axis=-1, keepdims=True)
    e = jnp.where(mask, jnp.exp(s - m), 0.0)
    return e / jnp.maximum(jnp.sum(e, axis=-1, keepdims=True), 1e-30)


def _head_to_group_lanes(q, h, lane):
    g, p = h // GRP, h // 2
    qp = q[:, LANES * p:LANES * (p + 1)]
    src = qp if (h % 2) == g else pltpu.roll(qp, HEAD_DIM, 1)
    keep = (lane < HEAD_DIM) if g == 0 else (lane >= HEAD_DIM)
    return jnp.where(keep, src, 0.0)


def _pair_tile(o_even, o_odd, h_even, lane):
    g = h_even // GRP
    a = o_even if g == 0 else pltpu.roll(o_even, HEAD_DIM, 1)
    b = pltpu.roll(o_odd, HEAD_DIM, 1) if g == 0 else o_odd
    return jnp.where(lane < HEAD_DIM, a, b)


def _ffn_body(*refs, n_chunk, tf, ff, final, mixed):
    refs = list(refs)
    x_ref = refs.pop(0)
    x = x_ref[...]
    if mixed:
        c_ref, a_ref, ag_ref, wc_ref, wa_ref = refs[:5]
        refs = refs[5:]
        x = (x + _dot(c_ref[...].astype(BF16), wc_ref[...])
             + _dot(_rms(a_ref[...], ag_ref[...]).astype(BF16), wa_ref[...]))
    g_ref, wi_ref, wo_ref = refs[:3]
    fg_ref = refs[3] if final else None
    o_ref = refs[-1]
    xn = _rms(x, g_ref[...]).astype(BF16)
    acc = jnp.zeros(x.shape, F32)
    for c in range(n_chunk):
        a = _dot(xn, wi_ref[:, c * tf:(c + 1) * tf])
        b = _dot(xn, wi_ref[:, ff + c * tf:ff + (c + 1) * tf])
        h = (a * _sigmoid(a)) * b
        acc = acc + _dot(h.astype(BF16), wo_ref[c * tf:(c + 1) * tf, :])
    y = x + 0.5 * acc
    if final:
        y = _rms(y, fg_ref[...])
    o_ref[...] = y


def _ffn(x2, norm_g, w_in, w_out, final_g, tm, mix=None):
    rows, d = x2.shape
    ff = w_out.shape[0]
    tf = 256
    assert rows % tm == 0 and ff % tf == 0 and w_in.shape == (d, 2 * ff)
    const = lambda i: (0, 0)
    row = lambda i: (i, 0)
    single = pl.Buffered(1)
    in_specs = [pl.BlockSpec((tm, d), row)]
    args = [x2]
    if mix is not None:
        cn, att, att_g, w_mix = mix
        w_mix = w_mix.astype(BF16)
        in_specs += [pl.BlockSpec((tm, C_CONV), row), pl.BlockSpec((tm, D_ATT), row),
                     pl.BlockSpec((1, D_ATT), const), pl.BlockSpec((C_CONV, d), const),
                     pl.BlockSpec((D_ATT, d), const)]
        args += [cn, att, att_g.reshape(1, D_ATT), w_mix[:C_CONV], w_mix[C_CONV:]]
    in_specs += [
        pl.BlockSpec((1, d), const),
        pl.BlockSpec((d, 2 * ff), const, pipeline_mode=single),
        pl.BlockSpec((ff, d), const, pipeline_mode=single),
    ]
    args += [norm_g.reshape(1, d), w_in.astype(BF16), w_out.astype(BF16)]
    if final_g is not None:
        in_specs.append(pl.BlockSpec((1, d), const))
        args.append(final_g.reshape(1, d))
    body = functools.partial(_ffn_body, n_chunk=ff // tf, tf=tf, ff=ff, final=final_g is not None,
                             mixed=mix is not None)
    return pl.pallas_call(
        body, out_shape=jax.ShapeDtypeStruct((rows, d), F32), grid=(rows // tm,),
        in_specs=in_specs, out_specs=pl.BlockSpec((tm, d), lambda i: (i, 0)),
        compiler_params=_params(1), name="ffn")(*args)


_SEG_GLU = 2 * C_CONV
_SEG_Q = _SEG_GLU + D_ATT
N_IN = _SEG_Q + 6 * KV_W + 3 * N_HEADS
N_IN_PAD = _SEG_Q + 7 * KV_W


def _proj_body(x_ref, g_ref, w_ref, qg_ref, ksg_ref, kwg_ref, bd4_ref, bd1_ref, c_ref, sa_ref, sb_ref,
               u_ref, q_ref, gt_ref, kc_ref, vc_ref, *kv_refs, transposed):
    xn = _rms(x_ref[...], g_ref[...]).astype(BF16)

    def seg(lo, width):
        return _dot(xn, w_ref[:, lo:lo + width])

    c, sa, sb = c_ref[...], sa_ref[...], sb_ref[...]
    reps = D_ATT // LANES
    glu = seg(0, _SEG_GLU)
    u_ref[...] = glu[:, :C_CONV] * _sigmoid(glu[:, C_CONV:])
    q = _group_rms(seg(_SEG_GLU, D_ATT), qg_ref[...], bd4_ref[...])
    q_ref[...] = _rope(q, _tile_lanes(c, reps), _tile_lanes(sa, reps), _tile_lanes(sb, reps))
    base = _SEG_Q
    kc = seg(base, KV_W)
    vc = seg(base + KV_W, KV_W)
    ks = _rope(_group_rms(seg(base + 2 * KV_W, KV_W), ksg_ref[...], bd1_ref[...]), c, sa, sb)
    vs = seg(base + 3 * KV_W, KV_W)
    kw = _rope(_group_rms(seg(base + 4 * KV_W, KV_W), kwg_ref[...], bd1_ref[...]), c, sa, sb)
    vw = seg(base + 5 * KV_W, KV_W)
    gt_ref[...] = _sigmoid(seg(base + 6 * KV_W, KV_W))
    kc_ref[...] = kc
    vc_ref[...] = vc
    if transposed:
        kv_refs[0][...] = ks
        kv_refs[1][...] = kw
        for ref, val in zip(kv_refs[2:], (kc, vc, ks, vs, kw, vw)):
            ref[0] = val.T
    else:
        for ref, val in zip(kv_refs, (ks, vs, kw, vw)):
            ref[...] = val


def _blockdiag_ones(width):
    idx = np.arange(width) // HEAD_DIM
    return jnp.asarray(idx[:, None] == idx[None, :], dtype=BF16)


def _rope_tables(pos):
    pos = np.asarray(pos, dtype=np.float64)
    inv = ROPE_THETA ** (-(np.arange(ROT_HALF, dtype=np.float64) * 2.0 / ROT_DIM))
    ang = pos[:, None] * inv[None, :]
    cos, sin = np.cos(ang).astype(np.float32), np.sin(ang).astype(np.float32)
    n = pos.shape[0]
    ones = np.ones((n, HEAD_DIM - ROT_DIM), np.float32)
    zeros = np.zeros((n, HEAD_DIM - ROT_DIM), np.float32)
    z8 = np.zeros((n, ROT_HALF), np.float32)
    c = np.concatenate([cos, cos, ones], axis=1)
    sa = np.concatenate([-sin, z8, zeros], axis=1)
    sb = np.concatenate([z8, sin, zeros], axis=1)
    return tuple(jnp.asarray(np.tile(t, (1, N_KV_HEADS))) for t in (c, sa, sb))


def _proj(x2, p, tabs, tm, seq_len=None):
    rows, d = x2.shape
    n_tab = tabs[0].shape[0] // tm
    w = jnp.pad(p['w_in'], ((0, 0), (0, N_IN_PAD - N_IN))).astype(BF16)
    const = lambda i: (0, 0)
    row = lambda i: (i, 0)
    tab = lambda i: (i % n_tab, 0)
    in_specs = [
        pl.BlockSpec((tm, d), row), pl.BlockSpec((1, d), const),
        pl.BlockSpec((d, N_IN_PAD), const, pipeline_mode=pl.Buffered(1)),
        pl.BlockSpec((1, D_ATT), const), pl.BlockSpec((1, KV_W), const), pl.BlockSpec((1, KV_W), const),
        pl.BlockSpec((D_ATT, D_ATT), const), pl.BlockSpec((KV_W, KV_W), const),
        pl.BlockSpec((tm, LANES), tab), pl.BlockSpec((tm, LANES), tab), pl.BlockSpec((tm, LANES), tab),
    ]
    widths = [C_CONV, D_ATT, LANES, KV_W, KV_W] + [KV_W] * (4 if seq_len is None else 2)
    out_shape = [jax.ShapeDtypeStruct((rows, wd), F32) for wd in widths]
    out_specs = [pl.BlockSpec((tm, wd), row) for wd in widths]
    if seq_len is not None:
        nt = seq_len // tm
        assert seq_len % tm == 0 and rows % seq_len == 0
        out_shape += [jax.ShapeDtypeStruct((rows // seq_len, KV_W, seq_len), F32)] * 6
        out_specs += [pl.BlockSpec((1, KV_W, tm), lambda i: (i // nt, 0, i % nt))] * 6
    args = [x2, p['mix_norm'].reshape(1, d), w,
            jnp.tile(p['q_norm'], N_HEADS).reshape(1, D_ATT),
            jnp.tile(p['k_sel_norm'], N_KV_HEADS).reshape(1, KV_W),
            jnp.tile(p['k_win_norm'], N_KV_HEADS).reshape(1, KV_W),
            _blockdiag_ones(D_ATT), _blockdiag_ones(KV_W), *tabs]
    return pl.pallas_call(
        functools.partial(_proj_body, transposed=seq_len is not None), out_shape=out_shape,
        grid=(rows // tm,), in_specs=in_specs, out_specs=out_specs,
        compiler_params=_params(1), name="proj")(*args)


def _conv_post(acc, lg, lb, og):
    mu = jnp.mean(acc, axis=-1, keepdims=True)
    xc = acc - mu
    var = jnp.mean(xc * xc, axis=-1, keepdims=True)
    y = xc * lax.rsqrt(var + EPS) * lg + lb
    y = y * _sigmoid(y)
    return _rms(y, og)


def _conv_body(u_ref, cw_ref, cb_ref, lg_ref, lb_ref, og_ref, o_ref, *, n_tiles, tt):
    halo = 32
    lead = halo - (CONV_W - 1)

    def tile(win, t0):
        acc = jnp.zeros((tt, C_CONV), F32) + cb_ref[...]
        for r in range(8):
            sh = win if r == 0 else pltpu.roll(win, tt + halo - r, 0)
            for a in range(halo // 8 + 1):
                w = 8 * a + r - lead
                if 0 <= w < CONV_W:
                    acc = acc + sh[8 * a:8 * a + tt, :] * cw_ref[w:w + 1, :]
        o_ref[0, pl.ds(t0, tt), :] = _conv_post(acc, lg_ref[...], lb_ref[...], og_ref[...])

    tile(jnp.concatenate([jnp.zeros((halo, C_CONV), F32), u_ref[0, 0:tt, :]], axis=0), 0)

    def step(i, carry):
        t0 = pl.multiple_of(i * tt, tt)
        tile(u_ref[0, pl.ds(pl.multiple_of(t0 - halo, halo), tt + halo), :], t0)
        return carry

    lax.fori_loop(1, n_tiles, step, 0)


def _conv_prompt(u, p, tt=256):
    b, t, c = u.shape
    cw = jnp.pad(p['conv_w'], ((0, 1), (0, 0)))
    const = lambda i: (0, 0)
    vec = lambda a: a.reshape(1, c)
    blk = pl.BlockSpec((1, t, c), lambda i: (i, 0, 0))
    return pl.pallas_call(
        functools.partial(_conv_body, n_tiles=t // tt, tt=tt),
        out_shape=jax.ShapeDtypeStruct((b, t, c), F32), grid=(b,),
        in_specs=[blk, pl.BlockSpec((32, c), const)] + [pl.BlockSpec((1, c), const)] * 4,
        out_specs=blk, compiler_params=_params(1), name="conv_prompt")(
            u, cw, vec(p['conv_b']), vec(p['conv_ln_g']), vec(p['conv_ln_b']), vec(p['out_norm_conv']))


def _conv_sample_body(st_ref, u_ref, cw_ref, cb_ref, lg_ref, lb_ref, og_ref, o_ref):
    acc = u_ref[...] * cw_ref[CONV_W - 1:CONV_W, :] + cb_ref[...]
    for w in range(CONV_W - 1):
        acc = acc + st_ref[w] * cw_ref[w:w + 1, :]
    o_ref[...] = _conv_post(acc, lg_ref[...], lb_ref[...], og_ref[...])


def _conv_sample(state, u, p):
    b = u.shape[0]
    c = C_CONV
    cw = jnp.pad(p['conv_w'], ((0, 1), (0, 0)))
    vec = lambda a: a.reshape(1, c)
    return pl.pallas_call(
        _conv_sample_body, out_shape=jax.ShapeDtypeStruct((b, c), F32),
        compiler_params=pltpu.CompilerParams(vmem_limit_bytes=VMEM_LIMIT), name="conv_sample")(
            state.transpose(1, 0, 2), u, cw, vec(p['conv_b']), vec(p['conv_ln_g']),
            vec(p['conv_ln_b']), vec(p['out_norm_conv']))


def _compress_weights(pe, w1, w2):
    assert N_SUB == 2
    w1r = w1.reshape(N_SUB, CMP_STRIDE, HEAD_DIM, CMP_HID).transpose(1, 2, 0, 3)
    eye = jnp.eye(N_KV_HEADS, dtype=w1.dtype)
    wf = jnp.einsum('sdmh,gk->sgdmkh', w1r, eye).reshape(CHUNK_W, N_SUB * N_KV_HEADS * CMP_HID)
    per = jnp.broadcast_to(pe.reshape(N_SUB, CMP_STRIDE, 1, HEAD_DIM),
                           (N_SUB, CMP_STRIDE, N_KV_HEADS, HEAD_DIM)).reshape(N_SUB, CHUNK_W)
    w2f = jnp.einsum('hd,gk->ghkd', w2, eye).reshape(N_KV_HEADS * CMP_HID, KV_W)
    return per, wf.astype(BF16), w2f.astype(BF16)


def _position_pair_weights(per, wf):
    half = N_KV_HEADS * CMP_HID
    pairs = CMP_STRIDE // 2
    pe_p = per.reshape(N_SUB * pairs, 2 * KV_W)
    w_p = wf.reshape(pairs, 2 * KV_W, N_SUB, half).transpose(2, 0, 1, 3).reshape(N_SUB * pairs, 2 * KV_W, half)
    return pe_p, w_p


def _compress_tail(a0, a1, w2_ref):
    n = a0.shape[0]
    h = a0 + pltpu.roll(a1, n - 1, 0)
    t = h * _sigmoid(h)
    tok = _dot(t.astype(BF16), w2_ref[...])
    return jnp.where(lax.broadcasted_iota(jnp.int32, tok.shape, 0) < n - 1, tok, 0.0)


def _cmp_key_post(tok, kng, bd1, c, sa, sb):
    return _rope(_group_rms(tok, kng[...], bd1[...]), c[...], sa[...], sb[...])


def _first_layer(rows_at, n_chunk, pe_ref, w1_ref):
    half = N_KV_HEADS * CMP_HID
    pairs = CMP_STRIDE // 2
    a0 = jnp.zeros((n_chunk, half), F32)
    a1 = jnp.zeros((n_chunk, half), F32)
    for sp in range(pairs):
        xs = jnp.concatenate([rows_at(2 * sp), rows_at(2 * sp + 1)], axis=1)
        a0 = a0 + _dot((xs + pe_ref[sp:sp + 1, :]).astype(BF16), w1_ref[sp])
        a1 = a1 + _dot((xs + pe_ref[pairs + sp:pairs + sp + 1, :]).astype(BF16), w1_ref[pairs + sp])
    return a0, a1


def _cmp_prompt_body(xk_ref, xv_ref, pek, w1k, w2k, kng, bd1, c, sa, sb, pev, w1v, w2v, ok_ref, ov_ref):
    n_chunk = ok_ref.shape[1]
    rows_of = lambda ref: lambda s: ref[0, pl.ds(s, n_chunk, stride=CMP_STRIDE), :]
    tok_k = _compress_tail(*_first_layer(rows_of(xk_ref), n_chunk, pek, w1k), w2k)
    tok_v = _compress_tail(*_first_layer(rows_of(xv_ref), n_chunk, pev, w1v), w2v)
    ok_ref[0] = _cmp_key_post(tok_k, kng, bd1, c, sa, sb)
    ov_ref[0] = tok_v.T


def _cmp_weights(p, name):
    per, wf, w2f = _compress_weights(p[f'cmp_{name}_pos'], p[f'cmp_{name}_w1'], p[f'cmp_{name}_w2'])
    return [*_position_pair_weights(per, wf), w2f]


def _cmp_prompt(kc, vc, p):
    b, t, _ = kc.shape
    n_chunk = t // CMP_STRIDE
    args = [*_cmp_weights(p, 'k'), *_cmp_key_consts(p, n_chunk), *_cmp_weights(p, 'v')]
    blk = pl.BlockSpec((1, t, KV_W), lambda i: (i, 0, 0))
    b3 = lambda i: (i, 0, 0)
    const = lambda a: pl.BlockSpec(a.shape, lambda i: (0,) * a.ndim)
    return pl.pallas_call(
        _cmp_prompt_body,
        out_shape=[jax.ShapeDtypeStruct((b, n_chunk, KV_W), F32), jax.ShapeDtypeStruct((b, KV_W, n_chunk), F32)],
        grid=(b,), in_specs=[blk, blk] + [const(a) for a in args],
        out_specs=[pl.BlockSpec((1, n_chunk, KV_W), b3), pl.BlockSpec((1, KV_W, n_chunk), b3)],
        compiler_params=_params(1), name="cmp_prompt")(kc, vc, *args)


def _cmp_key_consts(p, n_chunk):
    cmp_end = np.arange(n_chunk) * CMP_STRIDE + CMP_LEN - 1
    kng = jnp.tile(p['k_cmp_norm'], N_KV_HEADS).reshape(1, KV_W)
    return [kng, _blockdiag_ones(KV_W), *_rope_tables(cmp_end)]


def _cmp_sample_body(pt_ref, cache_hbm, pe_ref, w1_ref, w2_ref, *rest, n_pages, n_batch, is_key):
    key_consts, (o_ref, buf, tok, a_sc, sem) = rest[:-5], rest[-5:]
    b = pl.program_id(0)
    slot = b % 2
    page_len = cache_hbm.shape[2]
    n_pos = n_pages * page_len
    unroll = 8 if n_pages % 8 == 0 else 1

    def page_copy(page, i, sl):
        dst = pl.ds(pl.multiple_of(i * page_len, page_len), page_len)
        return pltpu.make_async_copy(cache_hbm.at[page], buf.at[sl, :, dst], sem.at[sl])

    def gather(seq, sl):
        def issue(i, carry):
            page_copy(pt_ref[seq * n_pages + i], i, sl).start()
            return carry
        lax.fori_loop(0, n_pages, issue, 0, unroll=unroll)

    @pl.when(b == 0)
    def _():
        gather(0, 0)

    @pl.when(b + 1 < n_batch)
    def _():
        gather(b + 1, 1 - slot)

    def wait(i, carry):
        page_copy(0, i, slot).wait()
        return carry
    lax.fori_loop(0, n_pages, wait, 0, unroll=unroll)

    n_blk = 4 if n_pos % (4 * 8 * page_len) == 0 else 1
    pos_blk = n_pos // n_blk
    chunk_blk = pos_blk // CMP_STRIDE
    xw = min(pos_blk, 8 * page_len)
    half_rows = CMP_STRIDE // 2
    for j in range(n_blk):
        p0 = j * pos_blk
        for i in range(pos_blk // xw):
            o = p0 + i * xw
            rows = buf[slot, :, o:o + xw].T.reshape(xw // CMP_STRIDE, CMP_STRIDE, KV_W)
            for h in range(2):
                tok[h, o // 2:(o + xw) // 2, :] = rows[:, h * half_rows:(h + 1) * half_rows, :].reshape(
                    xw // 2, KV_W)
        rows_at = lambda s: tok[s // half_rows, pl.ds(p0 // 2 + s % half_rows, chunk_blk, stride=half_rows), :]
        a0, a1 = _first_layer(rows_at, chunk_blk, pe_ref, w1_ref)
        a_sc[0, j * chunk_blk:(j + 1) * chunk_blk, :] = a0
        a_sc[1, j * chunk_blk:(j + 1) * chunk_blk, :] = a1
    out = _compress_tail(a_sc[0], a_sc[1], w2_ref)
    o_ref[0] = _cmp_key_post(out, *key_consts) if is_key else out


def _cmp_sample(page_table, cache_t, p, is_key):
    b, n_pages = page_table.shape
    page_len = cache_t.shape[2]
    n_pos = n_pages * page_len
    n_chunk = n_pos // CMP_STRIDE
    name = 'k' if is_key else 'v'
    args = _cmp_weights(p, name) + (_cmp_key_consts(p, n_chunk) if is_key else [])
    const = lambda a: pl.BlockSpec(a.shape, lambda i, pt: (0,) * a.ndim)
    grid_spec = pltpu.PrefetchScalarGridSpec(
        num_scalar_prefetch=1, grid=(b,),
        in_specs=[pl.BlockSpec(memory_space=pl.ANY)] + [const(a) for a in args],
        out_specs=pl.BlockSpec((1, n_chunk, KV_W), lambda i, pt: (i, 0, 0)),
        scratch_shapes=[pltpu.VMEM((2, KV_W, n_pos), F32), pltpu.VMEM((2, n_pos // 2, KV_W), F32),
                        pltpu.VMEM((N_SUB, n_chunk, N_KV_HEADS * CMP_HID), F32),
                        pltpu.SemaphoreType.DMA((2,))])
    return pl.pallas_call(
        functools.partial(_cmp_sample_body, n_pages=n_pages, n_batch=b, is_key=is_key),
        out_shape=jax.ShapeDtypeStruct((b, n_chunk, KV_W), F32), grid_spec=grid_spec,
        compiler_params=_params(1), name=f"cmp_sample_{name}")(page_table.reshape(-1), cache_t, *args)


def _cmp_to_sel(n_chunk, n_cmp, n_sel, width):
    cs = np.arange(n_chunk)[:, None] * CMP_STRIDE
    ss = np.arange(width)[None, :] * SEL_BLK
    ov = np.clip(np.minimum(cs + CMP_LEN, ss + SEL_BLK) - np.maximum(cs, ss), 0, None) / CMP_LEN
    ov = ov * (np.arange(n_chunk)[:, None] < n_cmp) * (np.arange(width)[None, :] < n_sel)
    return jnp.asarray(ov, dtype=BF16)


def _attn_qlanes_body(q_ref, gt_ref, kc_ref, vct_ref, ks_ref, vst_ref, kw_ref, vwt_ref, ovt_ref, eblk_ref,
                      o_ref, qa_sc, oct_sc, m_sc, l_sc, acc_sc, s_sc,
                      *, t, tq, kc_len, n_cmp, n_sel):
    off = pl.program_id(1) * tq
    cols = GRP * tq
    lane = lax.broadcasted_iota(jnp.int32, (tq, LANES), 1)
    qpos = off + lax.broadcasted_iota(jnp.int32, (1, tq), 1)
    q = q_ref[0] * (SCALE * LOG2E)
    for h in range(N_HEADS):
        g, r = divmod(h, GRP)
        qa_sc[g, :KV_W, r * tq:(r + 1) * tq] = _head_to_group_lanes(q, h, lane).T.astype(BF16)

    def attend(jobs):
        def get(ref, st, col):
            parts = [ref[st, :, c] for c in col]
            return parts[0] if len(parts) == 1 else jnp.concatenate(parts, axis=1)

        def put(ref, st, col, val):
            o = 0
            for c in col:
                ref[st, :, c] = val[:, o:o + c.stop - c.start]
                o += c.stop - c.start

        stats = []
        for st, k, vt, rhs, bias, col in jobs:
            n, w = k.shape[0], rhs.shape[1]
            s = _dot(k, rhs)
            if bias is not None:
                s = s + bias()
            s_sc[st, :n, :w] = s
            m_old = get(m_sc, st, col)
            m_new = jnp.maximum(m_old, jnp.max(s, axis=0, keepdims=True))
            put(m_sc, st, col, m_new)
            stats.append((m_new, jnp.exp2(m_old - m_new)))
        probs = []
        for (st, k, vt, rhs, bias, col), (m_new, alpha) in zip(jobs, stats):
            p = jnp.exp2(s_sc[st, :k.shape[0], :rhs.shape[1]] - m_new)
            put(l_sc, st, col, alpha * get(l_sc, st, col) + jnp.sum(p, axis=0, keepdims=True))
            probs.append(p.astype(BF16))
        for (st, k, vt, rhs, bias, col), (m_new, alpha), p in zip(jobs, stats, probs):
            put(acc_sc, st, col, alpha * get(acc_sc, st, col) + _dot(vt, p))

    sel_state = lambda g: g
    win_state = lambda g: N_KV_HEADS + g
    m_sc[...] = jnp.full(m_sc.shape, NEG_INF, F32)
    l_sc[...] = jnp.zeros(l_sc.shape, F32)
    acc_sc[...] = jnp.zeros(acc_sc.shape, F32)

    n_chunk = kc_ref.shape[1]
    kcm = kc_ref[0].astype(BF16)
    vct = vct_ref[0].astype(BF16)
    nrow = lax.broadcasted_iota(jnp.int32, (n_chunk, tq), 0)
    mask_1 = jnp.logical_and(nrow * CMP_STRIDE + (CMP_LEN - 1) <= qpos, nrow < n_cmp)
    mask_c = jnp.concatenate([mask_1.astype(F32)] * GRP, axis=1) > 0.5
    psum = []
    for g in range(N_KV_HEADS):
        sc = jnp.where(mask_c, _dot(kcm, qa_sc[g, :KV_W, :]), NEG_INF)
        e = jnp.where(mask_c, jnp.exp2(sc - jnp.max(sc, axis=0, keepdims=True)), 0.0)
        pc = e / jnp.maximum(jnp.sum(e, axis=0, keepdims=True), 1e-30)
        oct_sc[g] = _dot(vct, pc.astype(BF16))
        psum.append(sum(pc[:, r * tq:(r + 1) * tq] for r in range(GRP)))

    n_sel_pad = -(-n_sel // 8) * 8
    jrow = lax.broadcasted_iota(jnp.int32, (n_sel_pad, tq), 0)
    cur = qpos // SEL_BLK
    forced = jnp.logical_or(jrow == 0, jnp.logical_or(jrow == cur, jrow == cur - 1))
    bonus = jnp.where(forced, FORCE_BONUS, 0.0)
    k_eff = min(TOPK, n_sel)
    ovt = ovt_ref[...]
    for g in range(N_KV_HEADS):
        hi = psum[g].astype(BF16)
        lo = (psum[g] - hi.astype(F32)).astype(BF16)
        imp = _dot(ovt, hi) + _dot(ovt, lo)
        score = jnp.where(jrow <= cur, imp + bonus, NEG_INF)
        rank = jnp.zeros((n_sel_pad, tq), F32)
        for j in range(n_sel):
            row = score[j:j + 1, :]
            tie = jnp.where(jrow > j, 1.0, 0.0)
            rank = rank + jnp.where(row > score, 1.0, jnp.where(row == score, tie, 0.0))
        chosen = jnp.logical_and(rank < k_eff, jrow <= cur)
        for part, keep in enumerate((jnp.logical_and(chosen, jrow < off // SEL_BLK), chosen)):
            r0 = KV_W + part * LANES
            pen = jnp.where(keep, 0.0, -MASK_PENALTY).astype(BF16)
            qa_sc[g, r0:r0 + n_sel_pad, :] = jnp.concatenate([pen] * GRP, axis=1)
            qa_sc[g, r0 + n_sel_pad:r0 + LANES, :] = jnp.zeros((LANES - n_sel_pad, cols), BF16)

    tq_off = pl.multiple_of(off, tq)
    half = tq // 2
    all_cols = [slice(0, cols)]
    lo_cols = [slice(r * tq, r * tq + half) for r in range(GRP)]
    hi_cols = [slice(r * tq + half, (r + 1) * tq) for r in range(GRP)]

    def triangle(keep_front, key0, col):
        width = tq if col is all_cols else half
        q0 = half if col is hi_cols else 0
        n_cols = width * GRP
        kk = key0 + lax.broadcasted_iota(jnp.int32, (half, n_cols), 0)
        qq = q0 + (lax.broadcasted_iota(jnp.int32, (half, n_cols), 1) & (width - 1))
        return jnp.where((kk > qq) if keep_front else (kk <= qq), 0.0, -jnp.inf)

    def take_cols(x, col):
        return x if col is all_cols else jnp.concatenate([x[:, c] for c in col], axis=1)

    def half_jobs(st, k, vt, rhs, keep_front):
        cols_a, cols_b = (lo_cols, all_cols) if keep_front else (all_cols, hi_cols)
        job = lambda key0, col: (st, k[key0:key0 + half], vt[:, key0:key0 + half], take_cols(rhs, col),
                                 functools.partial(triangle, keep_front, key0, col), col)
        return job(0, cols_a), job(half, cols_b)

    k_own = jnp.concatenate(
        [ks_ref[0, pl.ds(tq_off, tq), :].astype(BF16), eblk_ref[pl.ds(tq_off, tq), :]], axis=1)
    vt_own = vst_ref[0, :, pl.ds(tq_off, tq)].astype(BF16)
    kw_own = kw_ref[0, pl.ds(tq_off, tq), :].astype(BF16)
    vwt_own = vwt_ref[0, :, pl.ds(tq_off, tq)].astype(BF16)
    groups = range(N_KV_HEADS)
    q_own = lambda g: jnp.concatenate([qa_sc[g, :KV_W, :], qa_sc[g, KV_W + LANES:, :]], axis=0)
    own = ([half_jobs(sel_state(g), k_own, vt_own, q_own(g), False) for g in groups]
           + [half_jobs(win_state(g), kw_own, vwt_own, qa_sc[g, :KV_W, :], False) for g in groups])
    attend([first for first, _ in own])
    attend([second for _, second in own])

    for ci in range(t // kc_len):
        @pl.when(ci * kc_len < off)
        def _(ci=ci):
            base = ci * kc_len
            k_aug = jnp.concatenate(
                [ks_ref[0, base:base + kc_len, :].astype(BF16), eblk_ref[base:base + kc_len, :]], axis=1)
            vt = vst_ref[0, :, base:base + kc_len].astype(BF16)
            attend([(sel_state(g), k_aug, vt, qa_sc[g, :KV_W + LANES, :], None, all_cols) for g in groups])

    def window_keys(start, n, front):
        start = pl.multiple_of(start, tq)
        kw = kw_ref[0, pl.ds(start, n), :].astype(BF16)
        vwt = vwt_ref[0, :, pl.ds(start, n)].astype(BF16)
        if front:
            halves = [half_jobs(win_state(g), kw, vwt, qa_sc[g, :KV_W, :], True) for g in groups]
            attend([first for first, _ in halves])
            attend([second for _, second in halves])
        else:
            attend([(win_state(g), kw, vwt, qa_sc[g, :KV_W, :], None, all_cols) for g in groups])

    @pl.when(off >= WINDOW)
    def _():
        window_keys(off - WINDOW, tq, True)
        if WINDOW > tq:
            window_keys(off - WINDOW + tq, WINDOW - tq, False)

    for i in range(1, WINDOW // tq):
        @pl.when(jnp.logical_and(off < WINDOW, off - i * tq >= 0))
        def _(i=i):
            window_keys(off - i * tq, tq, False)

    gtt = gt_ref[0].T
    outs = []
    for g in range(N_KV_HEADS):
        o_w = acc_sc[win_state(g)] / jnp.maximum(l_sc[win_state(g)], 1e-30)
        o_s = acc_sc[sel_state(g)] / jnp.maximum(l_sc[sel_state(g)], 1e-30)
        o_c = oct_sc[g]
        for r in range(GRP):
            h = g * GRP + r
            cs = slice(r * tq, (r + 1) * tq)
            o_h = (gtt[3 * h:3 * h + 1, :] * o_c[:, cs] + gtt[3 * h + 1:3 * h + 2, :] * o_s[:, cs]
                   + gtt[3 * h + 2:3 * h + 3, :] * o_w[:, cs])
            outs.append(o_h.T)
    o_ref[0] = jnp.concatenate(
        [_pair_tile(outs[2 * p], outs[2 * p + 1], 2 * p, lane) for p in range(N_HEADS // 2)], axis=-1)


def _attn_qlanes(q, gt, kcmp, vcmp_t, ks, vs_t, kw, vw_t, tq=512, kc_len=512):
    b, t, _ = q.shape
    n_chunk = kcmp.shape[1]
    n_cmp = n_chunk - N_SUB + 1
    n_sel = -(-t // SEL_BLK)
    assert t % kc_len == 0 and t >= WINDOW + tq and n_sel <= LANES and n_chunk % LANES == 0
    assert tq % LANES == 0 and WINDOW % tq == 0 and tq % SEL_BLK == 0
    n_sel_pad = -(-n_sel // 8) * 8
    ov_t = _cmp_to_sel(n_chunk, n_cmp, n_sel, n_sel_pad).T
    assert tq & (tq - 1) == 0
    eblk = jnp.asarray(np.arange(t)[:, None] // SEL_BLK == np.arange(LANES)[None, :], dtype=BF16)
    qblk = lambda i, j: (i, j, 0)
    full = lambda i, j: (i, 0, 0)
    const = lambda a: pl.BlockSpec(a.shape, lambda i, j: (0,) * a.ndim)
    tok = pl.BlockSpec((1, t, KV_W), full)
    tr = pl.BlockSpec((1, KV_W, t), full)
    cols = GRP * tq
    n_states = 2 * N_KV_HEADS
    assert kc_len >= WINDOW - tq
    body = functools.partial(_attn_qlanes_body, t=t, tq=tq, kc_len=kc_len, n_cmp=n_cmp, n_sel=n_sel)
    return pl.pallas_call(
        body, out_shape=jax.ShapeDtypeStruct((b, t, D_ATT), F32), grid=(b, t // tq),
        in_specs=[pl.BlockSpec((1, tq, D_ATT), qblk), pl.BlockSpec((1, tq, LANES), qblk),
                  pl.BlockSpec((1, n_chunk, KV_W), full), pl.BlockSpec((1, KV_W, n_chunk), full),
                  tok, tr, tok, tr, const(ov_t), const(eblk)],
        out_specs=pl.BlockSpec((1, tq, D_ATT), qblk),
        scratch_shapes=[pltpu.VMEM((N_KV_HEADS, KV_W + 2 * LANES, cols), BF16),
                        pltpu.VMEM((N_KV_HEADS, KV_W, cols), F32), pltpu.VMEM((n_states, 1, cols), F32),
                        pltpu.VMEM((n_states, 1, cols), F32), pltpu.VMEM((n_states, KV_W, cols), F32),
                        pltpu.VMEM((n_states, kc_len, cols), F32)],
        compiler_params=_params(2), name="attn_prompt")(
            q, gt, kcmp, vcmp_t, ks, vs_t, kw, vw_t, ov_t, eblk)


def _sample_heads(q_row):
    lane = lax.broadcasted_iota(jnp.int32, (1, LANES), 1)
    return jnp.concatenate([_head_to_group_lanes(q_row, h, lane) for h in range(N_HEADS)], axis=0)


def _attn_s1_body(q_ref, kc_ref, vc_ref, ov_ref, oc_ref, idx_ref, *, n_cmp, n_sel, q_pos):
    qh = _sample_heads(q_ref[0])
    n_chunk = kc_ref.shape[1]
    ncol = lax.broadcasted_iota(jnp.int32, (N_HEADS, n_chunk), 1)
    mask = jnp.logical_and(ncol * CMP_STRIDE + (CMP_LEN - 1) <= q_pos, ncol < n_cmp)
    pc = _masked_softmax(_dot_t(qh.astype(BF16), kc_ref[0].astype(BF16)) * SCALE, mask)
    oc_ref[0] = _dot(pc.astype(BF16), vc_ref[0].astype(BF16))
    rows = [jnp.sum(pc[g * GRP:(g + 1) * GRP, :], axis=0, keepdims=True) for g in range(N_KV_HEADS)]
    psum = jnp.concatenate(rows + [jnp.zeros((N_HEADS - N_KV_HEADS, n_chunk), F32)], axis=0)
    imp = _split_dot(psum, ov_ref[...])
    width = imp.shape[1]
    jl = lax.broadcasted_iota(jnp.int32, (N_HEADS, width), 1)
    cur = q_pos // SEL_BLK
    forced = jnp.logical_or(jl == 0, jnp.logical_or(jl == cur, jl == cur - 1))
    score = jnp.where(jl <= cur, imp + jnp.where(forced, FORCE_BONUS, 0.0), NEG_INF)
    n_rows = -(-n_sel // 8) * 8
    ii = lax.broadcasted_iota(jnp.int32, (n_rows, width), 0)
    jj = lax.broadcasted_iota(jnp.int32, (n_rows, width), 1)
    slot = lax.broadcasted_iota(jnp.int32, (n_rows, LANES), 1).astype(F32)
    blk = lax.broadcasted_iota(jnp.int32, (n_rows, LANES), 0).astype(F32)
    tie = jnp.where(jj < ii, 1.0, 0.0)
    out_rows = []
    for g in range(N_KV_HEADS):
        row = jnp.broadcast_to(score[g:g + 1, :], (n_rows, width))
        col = jnp.sum(jnp.where(ii == jj, row, 0.0), axis=1, keepdims=True)
        beats = jnp.where(row > col, 1.0, jnp.where(row == col, tie, 0.0))
        rank = jnp.sum(beats, axis=1, keepdims=True)
        out_rows.append(jnp.sum(jnp.where(rank == slot, blk, 0.0), axis=0, keepdims=True))
    out_rows.append(jnp.zeros((N_HEADS - N_KV_HEADS, LANES), F32))
    idx_ref[0] = jnp.concatenate(out_rows, axis=0).astype(jnp.int32)


def _attn_s1(q, kcmp, vcmp, q_pos, n_sel):
    b = q.shape[0]
    n_chunk = kcmp.shape[1]
    n_cmp = n_chunk - N_SUB + 1
    width = -(-n_sel // LANES) * LANES
    ov = _cmp_to_sel(n_chunk, n_cmp, n_sel, width)
    blk3 = lambda i: (i, 0, 0)
    cm = pl.BlockSpec((1, n_chunk, KV_W), blk3)
    out = pl.BlockSpec((1, N_HEADS, LANES), blk3)
    return pl.pallas_call(
        functools.partial(_attn_s1_body, n_cmp=n_cmp, n_sel=n_sel, q_pos=q_pos),
        out_shape=[jax.ShapeDtypeStruct((b, N_HEADS, LANES), F32),
                   jax.ShapeDtypeStruct((b, N_HEADS, LANES), jnp.int32)],
        grid=(b,),
        in_specs=[pl.BlockSpec((1, 1, D_ATT), blk3), cm, cm, pl.BlockSpec(ov.shape, lambda i: (0, 0))],
        out_specs=[out, out], compiler_params=_params(1), name="attn_sample_select")(
            q.reshape(b, 1, D_ATT), kcmp, vcmp, ov)


def _extra_key_attention(qh, k_t, v_t, mask, k_new, v_new):
    qf = qh.astype(F32)
    s = jnp.where(mask, _dot(qh, k_t.astype(BF16)) * SCALE, NEG_INF)
    s_new = jnp.sum(qf * k_new, axis=-1, keepdims=True) * SCALE
    m = jnp.maximum(jnp.max(s, axis=-1, keepdims=True), s_new)
    e = jnp.where(mask, jnp.exp(s - m), 0.0)
    e_new = jnp.exp(s_new - m)
    denom = jnp.maximum(jnp.sum(e, axis=-1, keepdims=True) + e_new, 1e-30)
    return (_dot_t(e.astype(BF16), v_t.astype(BF16)) + e_new * v_new) / denom


def _attn_s2_body(idx_ref, pt_ref, q_ref, gt_ref, oc_ref, ksn_ref, vsn_ref, kwn_ref, vwn_ref,
                  kws_ref, vws_ref, kcache, vcache, o_ref, kwo_ref, vwo_ref, kbuf, vbuf, sem,
                  *, n_pages, n_batch, n_cache_blk, q_pos):
    b = pl.program_id(0)
    slot = b % 2
    page_len = kcache.shape[2]
    per_page = page_len // SEL_BLK

    def block_id(seq, g, k):
        return idx_ref[(seq * N_KV_HEADS + g) * TOPK + k]

    def copies(seq, sl, g, k):
        blk = jnp.minimum(block_id(seq, g, k), n_cache_blk - 1)
        page = pt_ref[seq * n_pages + blk // per_page]
        dst = pl.ds(k * page_len, page_len)
        return (pltpu.make_async_copy(kcache.at[page], kbuf.at[sl, g, :, dst], sem.at[sl, 0]),
                pltpu.make_async_copy(vcache.at[page], vbuf.at[sl, g, :, dst], sem.at[sl, 1]))

    def for_all_copies(seq, sl, fn):
        for g in range(N_KV_HEADS):
            for k in range(TOPK):
                for cp in copies(seq, sl, g, k):
                    fn(cp)

    @pl.when(b == 0)
    def _():
        for_all_copies(0, 0, lambda cp: cp.start())

    @pl.when(b + 1 < n_batch)
    def _():
        for_all_copies(b + 1, 1 - slot, lambda cp: cp.start())

    for_all_copies(b, slot, lambda cp: cp.wait())
    blks = [[block_id(b, g, k) for k in range(TOPK)] for g in range(N_KV_HEADS)]

    qh = _sample_heads(q_ref[0]).astype(BF16)
    n_keys = TOPK * page_len
    row_grp = lax.broadcasted_iota(jnp.int32, (N_HEADS, n_keys), 0) // GRP
    key = lax.broadcasted_iota(jnp.int32, (N_HEADS, n_keys), 1)
    key_slot = key // page_len
    blk_in_page = (key % page_len) // SEL_BLK
    keep = jnp.zeros((N_HEADS, n_keys), F32)
    for g in range(N_KV_HEADS):
        for k in range(TOPK):
            blk = blks[g][k]
            in_cache = jnp.where(blk < n_cache_blk, 1.0, 0.0)
            hit = jnp.where(blk_in_page == blk % per_page, in_cache, 0.0)
            keep = jnp.where(jnp.logical_and(row_grp == g, key_slot == k), hit, keep)
    mask_s = keep > 0.5
    o_sel = [_extra_key_attention(qh, kbuf.at[slot, g][...], vbuf.at[slot, g][...], mask_s,
                                  ksn_ref[0], vsn_ref[0])
             for g in range(N_KV_HEADS)]
    out_grp = lax.broadcasted_iota(jnp.int32, (N_HEADS, LANES), 0) // GRP
    o_s = jnp.where(out_grp == 0, o_sel[0], o_sel[1])

    n_win = kws_ref.shape[2]
    kpos = (q_pos - n_win) + lax.broadcasted_iota(jnp.int32, (N_HEADS, n_win), 1)
    dist = q_pos - kpos
    mask_w = jnp.logical_and(dist >= 0, dist < WINDOW)
    o_w = _extra_key_attention(qh, kws_ref[0], vws_ref[0], mask_w, kwn_ref[0], vwn_ref[0])

    lane8 = lax.broadcasted_iota(jnp.int32, (N_HEADS, LANES), 1)
    head8 = lax.broadcasted_iota(jnp.int32, (N_HEADS, LANES), 0)
    gt = jnp.broadcast_to(gt_ref[0], (N_HEADS, LANES))
    gate = [jnp.sum(jnp.where(lane8 == 3 * head8 + j, gt, 0.0), axis=-1, keepdims=True) for j in range(3)]
    o = gate[0] * oc_ref[0] + gate[1] * o_s + gate[2] * o_w
    lane = lax.broadcasted_iota(jnp.int32, (1, LANES), 1)
    o_ref[0] = jnp.concatenate(
        [_pair_tile(o[2 * p:2 * p + 1, :], o[2 * p + 1:2 * p + 2, :], 2 * p, lane)
         for p in range(N_HEADS // 2)], axis=-1)

    eye = (lax.broadcasted_iota(jnp.int32, (KV_W, KV_W), 0)
           == lax.broadcasted_iota(jnp.int32, (KV_W, KV_W), 1))
    last = lax.broadcasted_iota(jnp.int32, (KV_W, n_win), 1) == n_win - 1
    for state_ref, new_ref, out_ref in ((kws_ref, kwn_ref, kwo_ref), (vws_ref, vwn_ref, vwo_ref)):
        col = jnp.sum(jnp.where(eye, jnp.broadcast_to(new_ref[0], (KV_W, KV_W)), 0.0), axis=1, keepdims=True)
        out_ref[0] = jnp.where(last, col, pltpu.roll(state_ref[0], n_win - 1, 1))


def _attn_s2(idx, page_table, q, gt, oc, ks_new, vs_new, kw_new, vw_new, kw_state_t, vw_state_t,
             cache_k_t, cache_v_t, q_pos):
    b, n_pages = page_table.shape
    page_len = cache_k_t.shape[2]
    n_win = kw_state_t.shape[2]
    assert n_win == WINDOW
    blk3 = lambda i, *_: (i, 0, 0)
    row = lambda w: pl.BlockSpec((1, 1, w), blk3)
    grid_spec = pltpu.PrefetchScalarGridSpec(
        num_scalar_prefetch=2, grid=(b,),
        in_specs=[row(D_ATT), row(LANES), pl.BlockSpec((1, N_HEADS, LANES), blk3),
                  row(KV_W), row(KV_W), row(KV_W), row(KV_W),
                  pl.BlockSpec((1, KV_W, n_win), blk3), pl.BlockSpec((1, KV_W, n_win), blk3),
                  pl.BlockSpec(memory_space=pl.ANY), pl.BlockSpec(memory_space=pl.ANY)],
        out_specs=[row(D_ATT), pl.BlockSpec((1, KV_W, n_win), blk3), pl.BlockSpec((1, KV_W, n_win), blk3)],
        scratch_shapes=[pltpu.VMEM((2, N_KV_HEADS, KV_W, TOPK * page_len), F32),
                        pltpu.VMEM((2, N_KV_HEADS, KV_W, TOPK * page_len), F32),
                        pltpu.SemaphoreType.DMA((2, 2))])
    r3 = lambda a: a.reshape(b, 1, a.shape[-1])
    body = functools.partial(_attn_s2_body, n_pages=n_pages, n_batch=b,
                             n_cache_blk=n_pages * (page_len // SEL_BLK),
                             q_pos=q_pos)
    state = jax.ShapeDtypeStruct((b, KV_W, n_win), F32)
    o, kw_next, vw_next = pl.pallas_call(
        body, out_shape=[jax.ShapeDtypeStruct((b, 1, D_ATT), F32), state, state], grid_spec=grid_spec,
        compiler_params=_params(1), name="attn_sample")(
            idx.reshape(-1), page_table.reshape(-1), r3(q), r3(gt), oc, r3(ks_new), r3(vs_new),
            r3(kw_new), r3(vw_new), kw_state_t, vw_state_t, cache_k_t, cache_v_t)
    return o.reshape(b, D_ATT), kw_next, vw_next


def _prompt_layer(x, p, tm=512, tm_proj=1024):
    b, t, d = x.shape
    x2 = _ffn(x.reshape(b * t, d), p['ffn1_norm'], p['ffn1_w_in'], p['ffn1_w_out'], None, tm)
    u, q, gt, kc, vc, ks, kw, kc_t, vc_t, ks_t, vs_t, kw_t, vw_t = _proj(
        x2, p, _rope_tables(np.arange(t)), tm_proj, seq_len=t)
    r3 = lambda a: a.reshape(b, t, a.shape[-1])
    u3 = r3(u)
    cn = _conv_prompt(u3, p)
    kcmp, vcmp_t = _cmp_prompt(r3(kc), r3(vc), p)
    o = _attn_qlanes(r3(q), r3(gt), kcmp, vcmp_t, r3(ks), vs_t, r3(kw), vw_t)
    y = _ffn(x2, p['ffn2_norm'], p['ffn2_w_in'], p['ffn2_w_out'], p['final_norm'], tm,
             mix=(cn.reshape(b * t, C_CONV), o.reshape(b * t, D_ATT), p['out_norm_attn'], p['w_out']))
    r4 = lambda a: a.reshape(b, N_KV_HEADS, HEAD_DIM, a.shape[-1]).transpose(0, 3, 1, 2)
    keep = min(WINDOW, t)
    state = (r4(kc_t), r4(vc_t), r4(ks_t), r4(vs_t), r4(kw_t[:, :, t - keep:]), r4(vw_t[:, :, t - keep:]),
             u3[:, t - (CONV_W - 1):])
    return y.reshape(b, t, d), state


def _sample_layer(x, p, cache_k_cmp, cache_v_cmp, cache_k_sel, cache_v_sel, kw_state, vw_state,
                  conv_state, page_table):
    b, t, d = x.shape
    assert t == 1
    past_len = page_table.shape[1] * cache_k_cmp.shape[1]
    x2 = _ffn(x.reshape(b, d), p['ffn1_norm'], p['ffn1_w_in'], p['ffn1_w_out'], None, b)
    tabs = _rope_tables(np.full((b,), past_len))
    u, q, gt, kc, vc, ks, vs, kw, vw = _proj(x2, p, tabs, b)
    cn = _conv_sample(conv_state, u, p)
    keys_on_lanes = lambda a: a.transpose(0, 2, 3, 1).reshape(a.shape[0], KV_W, a.shape[1])
    kcmp = _cmp_sample(page_table, keys_on_lanes(cache_k_cmp), p, True)
    vcmp = _cmp_sample(page_table, keys_on_lanes(cache_v_cmp), p, False)
    n_sel = -(-(past_len + 1) // SEL_BLK)
    oc, idx = _attn_s1(q, kcmp, vcmp, past_len, n_sel)
    o, kw_next, vw_next = _attn_s2(
        idx[:, :N_KV_HEADS, :TOPK], page_table, q, gt, oc, ks, vs, kw, vw,
        keys_on_lanes(kw_state), keys_on_lanes(vw_state),
        keys_on_lanes(cache_k_sel), keys_on_lanes(cache_v_sel), past_len)
    y = _ffn(x2, p['ffn2_norm'], p['ffn2_w_in'], p['ffn2_w_out'], p['final_norm'], b,
             mix=(cn, o, p['out_norm_attn'], p['w_out']))
    r4 = lambda a: a.reshape(b, 1, N_KV_HEADS, HEAD_DIM)
    time_major = lambda a: a.reshape(b, N_KV_HEADS, HEAD_DIM, a.shape[-1]).transpose(0, 3, 1, 2)
    new_conv = jnp.concatenate([conv_state, u[:, None, :]], axis=1)[:, 1:]
    return y.reshape(b, 1, d), (r4(kc), r4(vc), r4(ks), r4(vs), time_major(kw_next), time_major(vw_next),
                                new_conv)


_PARAM_NAMES = ('ffn1_norm', 'ffn1_w_in', 'ffn1_w_out', 'mix_norm', 'w_in', 'conv_w', 'conv_b', 'conv_ln_g',
                'conv_ln_b', 'q_norm', 'k_cmp_norm', 'k_sel_norm', 'k_win_norm', 'cmp_k_pos', 'cmp_k_w1',
                'cmp_k_w2', 'cmp_v_pos', 'cmp_v_w1', 'cmp_v_w2', 'out_norm_conv', 'out_norm_attn', 'w_out',
                'ffn2_norm', 'ffn2_w_in', 'ffn2_w_out', 'final_norm')


def kernel(x_prompt, x_sample, cache_k_cmp, cache_v_cmp, cache_k_sel, cache_v_sel, state_k_win, state_v_win,
           state_conv, page_table, ffn1_norm, ffn1_w_in, ffn1_w_out, mix_norm, w_in, conv_w, conv_b, conv_ln_g,
           conv_ln_b, q_norm, k_cmp_norm, k_sel_norm, k_win_norm, cmp_k_pos, cmp_k_w1, cmp_k_w2, cmp_v_pos,
           cmp_v_w1, cmp_v_w2, out_norm_conv, out_norm_attn, w_out, ffn2_norm, ffn2_w_in, ffn2_w_out, final_norm):
    stacked = dict(zip(_PARAM_NAMES, (
        ffn1_norm, ffn1_w_in, ffn1_w_out, mix_norm, w_in, conv_w, conv_b, conv_ln_g, conv_ln_b, q_norm,
        k_cmp_norm, k_sel_norm, k_win_norm, cmp_k_pos, cmp_k_w1, cmp_k_w2, cmp_v_pos, cmp_v_w1, cmp_v_w2,
        out_norm_conv, out_norm_attn, w_out, ffn2_norm, ffn2_w_in, ffn2_w_out, final_norm)))
    depth = ffn1_norm.shape[0]
    yp, ys = x_prompt, x_sample
    new_p, new_s = [], []
    for l in range(depth):
        p = {k: v[l] for k, v in stacked.items()}
        yp, st_p = _prompt_layer(yp, p)
        new_p.append(st_p)
        ys, st_s = _sample_layer(ys, p, cache_k_cmp[l], cache_v_cmp[l], cache_k_sel[l], cache_v_sel[l],
                                 state_k_win[l], state_v_win[l], state_conv[l], page_table)
        new_s.append(st_s)
    outs_p = [jnp.stack(tup) for tup in zip(*new_p)]
    outs_s = [jnp.stack(tup) for tup in zip(*new_s)]
    return (yp, ys, *outs_p, *outs_s)
```
